```python
import math
import jax, jax.numpy as jnp
from jax import lax
import numpy as np

D_MODEL = 1024
BATCH = 2
SEQ = 8192
DEPTH = 4
DEC_BATCH = 32
DEC_SEQ = 1
PAST_LEN = 8192
PAGE_SIZE = 128

H_A = 4
DK_A = 32
DV_A = 64
GATE_RANK = 16
GATE_TAU = 16.0
GLA_CHUNK = 64
H_D = 12
DH_D = 64
DIL_CONFIGS = ((128, 1), (512, 4), (2048, 16))
MAX_WINDOW = 2048
N_BUCKETS = 32
MAX_DISTANCE = 2048
N_EXPERTS = 16
N_EXPERT_GROUPS = 4
TOP_K = 2
D_FF_EXPERT = 512
EPS = 1e-6

MIX_WIDTH = H_A * DV_A + H_D * DH_D
PROJ_SIZES = (H_A * DK_A, H_A * DK_A, H_A * DV_A, GATE_RANK, H_A * DV_A, H_D * DH_D, H_D * DH_D, H_D * DH_D)
D_IN = sum(PROJ_SIZES)

kernel_name = "hymba_gla_dilated_moe_step"


def rmsnorm(x, g):
    xf = x.astype(jnp.float32)
    y = xf * lax.rsqrt(jnp.mean(xf * xf, axis=-1, keepdims=True) + EPS)
    return (y * g.astype(jnp.float32)).astype(x.dtype)


def rel_bucket(dist):
    max_exact = N_BUCKETS // 2
    d = jnp.maximum(dist, 0)
    df = jnp.maximum(d, 1).astype(jnp.float32)
    large = max_exact + (jnp.log(df / max_exact) / math.log(MAX_DISTANCE / max_exact)
                         * (N_BUCKETS - max_exact)).astype(jnp.int32)
    large = jnp.minimum(large, N_BUCKETS - 1)
    return jnp.where(d < max_exact, d, large)


def split_proj(proj):
    offs = np.cumsum(PROJ_SIZES)[:-1].tolist()
    return jnp.split(proj, offs, axis=-1)


def gla_chunked(q, k, v, log_a, s0, chunk):
    B, S, H, DK = q.shape
    n = S // chunk

    def to_chunks(t):
        return jnp.moveaxis(t.astype(jnp.float32).reshape(B, n, chunk, H, t.shape[-1]), 1, 0)

    qc, kc, vc, gc = to_chunks(q), to_chunks(k), to_chunks(v), to_chunks(log_a)
    causal = jnp.tril(jnp.ones((chunk, chunk), bool))[None, :, :, None, None]

    def step(s, inp):
        qi, ki, vi, gi = inp
        b = jnp.cumsum(gi, axis=1)
        diff = jnp.where(causal, b[:, :, None] - b[:, None, :], -jnp.inf)
        att = jnp.einsum('bthk,bshk,btshk->bhts', qi, ki, jnp.exp(diff))
        o = jnp.einsum('bhts,bshv->bthv', att, vi) + jnp.einsum('bthk,bhkv->bthv', qi * jnp.exp(b), s)
        b_last = b[:, -1]
        k_dec = ki * jnp.exp(b_last[:, None] - b)
        s_new = jnp.exp(b_last)[..., None] * s + jnp.einsum('bshk,bshv->bhkv', k_dec, vi)
        return s_new, o

    s_fin, o = lax.scan(step, s0, (qc, kc, vc, gc))
    o = jnp.moveaxis(o, 0, 1).reshape(B, S, H, -1)
    return o, s_fin


def combine_branches(outs, lses):
    w = jax.nn.softmax(jnp.stack(lses, axis=0), axis=0)
    return jnp.einsum('cbsh,cbshd->bshd', w, jnp.stack(outs, axis=0))


def dilated_prompt(q, k, v, rel_bias):
    B, S, H, D = q.shape
    scale = DH_D ** -0.5
    outs, lses = [], []
    for (w, d) in DIL_CONFIGS:
        blk = w // d
        span = d * blk
        Sp = -(-S // span) * span
        nb = Sp // span

        def split(t):
            t = jnp.pad(t, ((0, 0), (0, Sp - S), (0, 0), (0, 0))).reshape(B, Sp // d, d, H, D)
            return jnp.moveaxis(t, 2, 1).reshape(B, d, nb, blk, H, D)

        def with_prev(t):
            prev = jnp.pad(t, ((0, 0), (0, 0), (1, 0), (0, 0), (0, 0), (0, 0)))[:, :, :-1]
            return jnp.concatenate([prev, t], axis=3)

        qb = split(q)
        kw, vw = with_prev(split(k)), with_prev(split(v))
        i = jnp.arange(blk)[:, None]
        j = jnp.arange(2 * blk)[None, :]
        delta = blk + i - j
        band = (delta >= 0) & (delta <= blk)
        first = (jnp.arange(nb) == 0)[:, None, None]
        valid = band[None] & ~(first & (j < blk)[None])
        bias = jnp.moveaxis(rel_bias[rel_bucket(delta * d)], -1, 0).astype(jnp.float32)
        logits = jnp.einsum('brnqhd,brnkhd->brnhqk', qb, kw) * scale + bias[None, None, None]
        logits = jnp.where(valid[None, None, :, None], logits, -jnp.inf)
        m = jnp.max(logits, axis=-1, keepdims=True)
        p = jnp.exp(logits - m)
        den = jnp.sum(p, axis=-1, keepdims=True)
        o = jnp.einsum('brnhqk,brnkhd->brnqhd', p / den, vw)
        lse = jnp.moveaxis((m + jnp.log(den))[..., 0], -1, -2)
        o = jnp.moveaxis(o.reshape(B, d, Sp // d, H, D), 1, 2).reshape(B, Sp, H, D)[:, :S]
        lse = jnp.moveaxis(lse.reshape(B, d, Sp // d, H), 1, 2).reshape(B, Sp, H)[:, :S]
        outs.append(o)
        lses.append(lse)
    return combine_branches(outs, lses)


def dilated_sample(q, k_new, v_new, buf_k, buf_v, rel_bias):
    Bd, T, H, D = q.shape
    L = buf_k.shape[1]
    scale = DH_D ** -0.5
    k_all = jnp.concatenate([buf_k.astype(jnp.float32), k_new], axis=1)
    v_all = jnp.concatenate([buf_v.astype(jnp.float32), v_new], axis=1)
    t = jnp.arange(T)
    outs, lses = [], []
    for (w, d) in DIL_CONFIGS:
        jj = jnp.arange(w // d + 1)
        idx = L + t[:, None] - jj[None, :] * d
        valid = idx >= 0
        idxc = jnp.maximum(idx, 0)
        kg, vg = k_all[:, idxc], v_all[:, idxc]
        bias = rel_bias[rel_bucket(jj * d)].astype(jnp.float32)
        logits = jnp.einsum('bthd,btjhd->bhtj', q, kg) * scale + bias.T[None, :, None, :]
        logits = jnp.where(valid[None, None], logits, -jnp.inf)
        m = jnp.max(logits, axis=-1, keepdims=True)
        p = jnp.exp(logits - m)
        den = jnp.sum(p, axis=-1, keepdims=True)
        outs.append(jnp.einsum('bhtj,btjhd->bthd', p / den, vg))
        lses.append(jnp.transpose((m + jnp.log(den))[..., 0], (0, 2, 1)))
    return combine_branches(outs, lses)


def moe(h, router_w, router_b, wg, wu, wd):
    B, S, D = h.shape
    x = h.reshape(-1, D)
    n = x.shape[0]
    per_group = N_EXPERTS // N_EXPERT_GROUPS
    scores = jax.nn.sigmoid(jnp.matmul(x, router_w).astype(jnp.float32))
    sel = scores + router_b.astype(jnp.float32)
    group_score = lax.top_k(sel.reshape(n, N_EXPERT_GROUPS, per_group), 2)[0].sum(-1)
    gmask = jax.nn.one_hot(jnp.argmax(group_score, axis=-1), N_EXPERT_GROUPS, dtype=jnp.bool_)
    masked = jnp.where(jnp.repeat(gmask, per_group, axis=1), sel, -jnp.inf)
    _, top_idx = lax.top_k(masked, TOP_K)
    top_w = jnp.take_along_axis(scores, top_idx, axis=-1)
    top_w = top_w / jnp.sum(top_w, axis=-1, keepdims=True)
    gate = jnp.einsum('nk,nke->ne', top_w, jax.nn.one_hot(top_idx, N_EXPERTS, dtype=jnp.float32)).astype(h.dtype)
    hid = jax.nn.silu(jnp.einsum('nd,edf->nef', x, wg)) * jnp.einsum('nd,edf->nef', x, wu)
    y = jnp.einsum('nef,efd->nd', hid * gate[:, :, None], wd)
    return y.reshape(B, S, D)


def trunk(x, c, gla_state, win_k, win_v, prompt,
          w_ada, b_ada, norm1_g, norm2_g, w_in, w_gate_up, b_gate, gla_norm_g, w_o,
          rel_bias, router_w, router_b, moe_w_gate, moe_w_up, moe_w_down, final_norm_g):
    B, S, _ = x.shape
    cs = jax.nn.silu(c)
    new_s, new_k, new_v = [], [], []
    for l in range(DEPTH):
        mod = (jnp.matmul(cs, w_ada[l]) + b_ada[l])[:, None, :]
        sh1, sc1, g1, sh2, sc2, g2 = jnp.split(mod, 6, axis=-1)
        h = rmsnorm(x, norm1_g[l]) * (1 + sc1) + sh1
        q_a, k_a, v_a, g_lr, r_a, q_d, k_d, v_d = split_proj(jnp.matmul(h, w_in[l]))
        q_a = q_a.reshape(B, S, H_A, DK_A) * (DK_A ** -0.5)
        k_a = k_a.reshape(B, S, H_A, DK_A)
        v_a = v_a.reshape(B, S, H_A, DV_A)
        log_a = jax.nn.log_sigmoid((jnp.matmul(g_lr, w_gate_up[l]) + b_gate[l]).astype(jnp.float32)) / GATE_TAU
        log_a = log_a.reshape(B, S, H_A, DK_A)
        if prompt:
            s0 = jnp.zeros((B, H_A, DK_A, DV_A), jnp.float32)
        else:
            s0 = gla_state[l].astype(jnp.float32)
        o_a, s_fin = gla_chunked(q_a, k_a, v_a, log_a, s0, math.gcd(S, GLA_CHUNK))
        o_a = rmsnorm(o_a, gla_norm_g[l].reshape(H_A, DV_A)).astype(x.dtype).reshape(B, S, H_A * DV_A)
        o_a = o_a * jax.nn.silu(r_a)
        q_d = q_d.reshape(B, S, H_D, DH_D).astype(jnp.float32)
        k_d = k_d.reshape(B, S, H_D, DH_D).astype(jnp.float32)
        v_d = v_d.reshape(B, S, H_D, DH_D).astype(jnp.float32)
        if prompt:
            o_d = dilated_prompt(q_d, k_d, v_d, rel_bias)
            keep = min(MAX_WINDOW, S)
            new_k.append(k_d[:, S - keep:].astype(x.dtype))
            new_v.append(v_d[:, S - keep:].astype(x.dtype))
        else:
            o_d = dilated_sample(q_d, k_d, v_d, win_k[l], win_v[l], rel_bias)
            new_k.append(k_d.astype(x.dtype))
            new_v.append(v_d.astype(x.dtype))
        new_s.append(s_fin.astype(x.dtype))
        mix = jnp.concatenate([o_a, o_d.astype(x.dtype).reshape(B, S, H_D * DH_D)], axis=-1)
        x = x + g1 * jnp.matmul(mix, w_o[l])
        h = rmsnorm(x, norm2_g[l]) * (1 + sc2) + sh2
        x = x + g2 * moe(h, router_w, router_b, moe_w_gate[l], moe_w_up[l], moe_w_down[l])
    y = rmsnorm(x, final_norm_g)
    return y, jnp.stack(new_s), jnp.stack(new_k), jnp.stack(new_v)


def setup_inputs(seed: int = 0) -> dict:
    key = jax.random.key(seed)
    ks = jax.random.split(key, 32)
    f32 = jnp.float32
    win = min(MAX_WINDOW, PAST_LEN)

    def nrm(k, shape, s):
        return jax.random.normal(k, shape, f32) * s

    return {
        "x_prompt": nrm(ks[0], (BATCH, SEQ, D_MODEL), 1.0),
        "x_sample": nrm(ks[1], (DEC_BATCH, DEC_SEQ, D_MODEL), 1.0),
        "state_gla": nrm(ks[2], (DEPTH, DEC_BATCH, H_A, DK_A, DV_A), 1.0),
        "cache_win_k": nrm(ks[3], (DEPTH, DEC_BATCH, win, H_D, DH_D), 1.0),
        "cache_win_v": nrm(ks[4], (DEPTH, DEC_BATCH, win, H_D, DH_D), 1.0),
        "c_prompt": nrm(ks[5], (BATCH, D_MODEL), 1.0),
        "c_sample": nrm(ks[6], (DEC_BATCH, D_MODEL), 1.0),
        "w_ada": nrm(ks[7], (DEPTH, D_MODEL, 6 * D_MODEL), 0.3 * D_MODEL ** -0.5),
        "b_ada": nrm(ks[8], (DEPTH, 6 * D_MODEL), 0.02),
        "norm1_g": 1.0 + nrm(ks[9], (DEPTH, D_MODEL), 0.02),
        "norm2_g": 1.0 + nrm(ks[10], (DEPTH, D_MODEL), 0.02),
        "w_in": nrm(ks[11], (DEPTH, D_MODEL, D_IN), D_MODEL ** -0.5),
        "w_gate_up": nrm(ks[12], (DEPTH, GATE_RANK, H_A * DK_A), GATE_RANK ** -0.5),
        "b_gate": nrm(ks[13], (DEPTH, H_A * DK_A), 0.1),
        "gla_norm_g": 1.0 + nrm(ks[14], (DEPTH, H_A * DV_A), 0.02),
        "w_o": nrm(ks[15], (DEPTH, MIX_WIDTH, D_MODEL), MIX_WIDTH ** -0.5),
        "rel_bias": nrm(ks[16], (N_BUCKETS, H_D), 0.5),
        "router_w": nrm(ks[17], (D_MODEL, N_EXPERTS), D_MODEL ** -0.5),
        "router_b": nrm(ks[18], (N_EXPERTS,), 0.01),
        "moe_w_gate": nrm(ks[19], (DEPTH, N_EXPERTS, D_MODEL, D_FF_EXPERT), D_MODEL ** -0.5),
        "moe_w_up": nrm(ks[20], (DEPTH, N_EXPERTS, D_MODEL, D_FF_EXPERT), D_MODEL ** -0.5),
        "moe_w_down": nrm(ks[21], (DEPTH, N_EXPERTS, D_FF_EXPERT, D_MODEL), D_FF_EXPERT ** -0.5),
        "final_norm_g": 1.0 + nrm(ks[22], (D_MODEL,), 0.02),
    }


def reference(x_prompt, x_sample, state_gla, cache_win_k, cache_win_v, c_prompt, c_sample,
              w_ada, b_ada, norm1_g, norm2_g, w_in, w_gate_up, b_gate, gla_norm_g, w_o,
              rel_bias, router_w, router_b, moe_w_gate, moe_w_up, moe_w_down, final_norm_g):
    y_prompt, gla_state_p, win_k_p, win_v_p = trunk(
        x_prompt, c_prompt, None, None, None, True,
        w_ada, b_ada, norm1_g, norm2_g, w_in, w_gate_up, b_gate, gla_norm_g, w_o,
        rel_bias, router_w, router_b, moe_w_gate, moe_w_up, moe_w_down, final_norm_g)
    y_sample, gla_state_s, new_k_s, new_v_s = trunk(
        x_sample, c_sample, state_gla, cache_win_k, cache_win_v, False,
        w_ada, b_ada, norm1_g, norm2_g, w_in, w_gate_up, b_gate, gla_norm_g, w_o,
        rel_bias, router_w, router_b, moe_w_gate, moe_w_up, moe_w_down, final_norm_g)
    return (y_prompt, y_sample, gla_state_p, win_k_p, win_v_p, gla_state_s, new_k_s, new_v_s)
```

```python
import functools
import math

import jax
import jax.numpy as jnp
import numpy as np
from jax import lax
from jax.experimental import pallas as pl
from jax.experimental.pallas import tpu as pltpu

f32 = jnp.float32
bf16 = jnp.bfloat16

H_A, DK_A, DV_A = 4, 32, 64
GATE_RANK = 16
GATE_TAU = 16.0
H_D, DH_D = 12, 64
DIL_CONFIGS = ((128, 1), (512, 4), (2048, 16))
N_BUCKETS = 32
MAX_DISTANCE = 2048
N_EXPERTS = 16
N_EXPERT_GROUPS = 4
EPS = 1e-6

QA = H_A * DK_A
VA = H_A * DV_A
QD = H_D * DH_D
A_WIDTH = 2 * QA + 2 * VA + 128
W_IN_COLS = A_WIDTH + 3 * QD
BLK = 128
N_PAIRS = H_D // 2
NEG = -1e30
GLA_SUB = 16
GLA_ROWS = 128
EXP_CLAMP = 80.0
VMEM_LIMIT = 48 * 1024 * 1024

HIGHEST = lax.Precision.HIGHEST
NT = (((1,), (1,)), ((), ()))
TN = (((0,), (0,)), ((), ()))


def _cparams(*sem):
    return pltpu.CompilerParams(dimension_semantics=sem, vmem_limit_bytes=VMEM_LIMIT)


def _resident(shape, index_map):
    return pl.BlockSpec(shape, index_map, pipeline_mode=pl.Buffered(1))


def _silu(x):
    return x * jax.nn.sigmoid(x)


def _rmsnorm(x, g):
    return x * lax.rsqrt(jnp.mean(x * x, axis=-1, keepdims=True) + EPS) * g


def _ada_kernel(c_ref, w_ref, b_ref, o_ref):
    cs = _silu(c_ref[...]).astype(bf16)
    o_ref[...] = jnp.dot(cs, w_ref[...].astype(bf16), preferred_element_type=f32) + b_ref[...]


def _ada_call(c, w_ada, b_ada):
    depth, d, n6 = w_ada.shape
    m = c.shape[0]
    tn = 1536
    return pl.pallas_call(
        _ada_kernel,
        grid=(depth, n6 // tn),
        in_specs=[
            pl.BlockSpec((m, d), lambda l, j: (0, 0)),
            pl.BlockSpec((None, d, tn), lambda l, j: (l, 0, j)),
            pl.BlockSpec((None, 1, tn), lambda l, j: (l, 0, j)),
        ],
        out_specs=pl.BlockSpec((None, m, tn), lambda l, j: (l, 0, j)),
        out_shape=jax.ShapeDtypeStruct((depth, m, n6), f32),
        compiler_params=_cparams("arbitrary", "arbitrary"),
        name="adaln",
    )(c, w_ada, b_ada.reshape(depth, 1, n6))


def _inproj_kernel(prompt, x_ref, sc_ref, sh_ref, g_ref, w_ref, *outs):
    h = _rmsnorm(x_ref[...], g_ref[...])
    h = (h * (1.0 + sc_ref[...]) + sh_ref[...]).astype(bf16)

    def proj(lo, hi):
        return jnp.dot(h, w_ref[:, lo:hi], preferred_element_type=f32)

    a_ref, q_ref, k_ref, v_ref = outs[:4]
    a_ref[...] = proj(0, A_WIDTH)
    q = proj(A_WIDTH, A_WIDTH + QD)
    k = proj(A_WIDTH + QD, A_WIDTH + 2 * QD)
    v = proj(A_WIDTH + 2 * QD, A_WIDTH + 3 * QD)
    if prompt:
        kw_ref, vw_ref = outs[4:]
        q_ref[...] = (q * (DH_D ** -0.5)).astype(bf16)
        k_ref[...] = k.astype(bf16)
        v_ref[...] = v.astype(bf16)
        kw_ref[...] = k
        vw_ref[...] = v
    else:
        q_ref[...] = q
        k_ref[...] = k
        v_ref[...] = v


def _mod_spec(arr, tiles_per_batch):
    rows, d = arr.shape[1], arr.shape[2]
    if tiles_per_batch:
        return pl.BlockSpec((None, rows, d), lambda i, *_: (i // tiles_per_batch, 0, 0))
    return pl.BlockSpec((None, rows, d), lambda i, *_: (0, 0, 0))


def _inproj_call(x2d, sc, sh, g, w, tm, tiles_per_batch, keep_tiles):
    n, d = x2d.shape
    prompt = bool(tiles_per_batch)
    qkv_dt = bf16 if prompt else f32
    row = lambda i: (i, 0)
    out_specs = [pl.BlockSpec((tm, A_WIDTH), row)] + [pl.BlockSpec((tm, QD), row)] * 3
    out_shape = [jax.ShapeDtypeStruct((n, A_WIDTH), f32)] + [jax.ShapeDtypeStruct((n, QD), qkv_dt)] * 3
    if prompt:
        nb = n // (tm * tiles_per_batch)
        first_keep = tiles_per_batch - keep_tiles

        def win(i):
            return (i // tiles_per_batch, jnp.maximum(i % tiles_per_batch - first_keep + 1, 0), 0, 0)

        out_specs += [pl.BlockSpec((None, None, tm, QD), win)] * 2
        out_shape += [jax.ShapeDtypeStruct((nb, keep_tiles + 1, tm, QD), f32)] * 2
    return pl.pallas_call(
        functools.partial(_inproj_kernel, prompt),
        grid=(n // tm,),
        in_specs=[
            pl.BlockSpec((tm, d), row),
            _mod_spec(sc, tiles_per_batch),
            _mod_spec(sh, tiles_per_batch),
            pl.BlockSpec((1, d), lambda i: (0, 0)),
            _resident((d, W_IN_COLS), lambda i: (0, 0)),
        ],
        out_specs=out_specs,
        out_shape=out_shape,
        compiler_params=_cparams("arbitrary"),
        name="inproj_prompt" if prompt else "inproj_sample",
    )(x2d, sc, sh, g, w)


def _log_gate(glr, wg_ref, bg_ref):
    x = jnp.dot(glr.astype(bf16), wg_ref[...].astype(bf16), preferred_element_type=f32) + bg_ref[...]
    return (jnp.minimum(x, 0.0) - jnp.log1p(jnp.exp(-jnp.abs(x)))) * (1.0 / GATE_TAU)


def _gla_prompt_kernel(a_ref, wg_ref, bg_ref, gn_ref, o_ref, s_ref, st_ref):
    R, C = GLA_ROWS, GLA_SUB

    @pl.when(pl.program_id(1) == 0)
    def _():
        st_ref[...] = jnp.zeros_like(st_ref)

    def iota(shape, dim, shift=0):
        return lax.shift_right_logical(lax.broadcasted_iota(jnp.int32, shape, dim), shift)

    sub, lk, lv = int(math.log2(C)), int(math.log2(DK_A)), int(math.log2(DV_A))
    ri = lax.bitwise_and(iota((H_A * R, R), 0), R - 1)
    ci = iota((H_A * R, R), 1)
    same4 = lax.shift_right_logical(ri, sub) == lax.shift_right_logical(ci, sub)
    causal4 = same4 & (ci <= ri)
    tri = jnp.where(causal4, 1.0, 0.0).astype(f32)[0:R]
    bsum = jnp.where(same4, 1.0, 0.0).astype(f32)[0:R]
    lane_q = iota((1, QA), 1, lk)
    lane_v = iota((1, VA), 1, lv)
    bd = iota((VA, QA), 0, lv) == iota((VA, QA), 1, lk)
    hmean = jnp.where(iota((VA, VA), 0, lv) == iota((VA, VA), 1, lv), 1.0 / DV_A, 0.0).astype(f32)

    def block(c, st):
        r0 = pl.multiple_of(c * R, R)
        blk = a_ref[pl.ds(r0, R), :]
        q = blk[:, 0:QA] * (DK_A ** -0.5)
        k = blk[:, QA:2 * QA]
        v = blk[:, 2 * QA:2 * QA + VA]
        r = blk[:, 2 * QA + VA:2 * QA + 2 * VA]
        g = _log_gate(blk[:, 2 * QA + 2 * VA:], wg_ref, bg_ref)
        b = jnp.dot(tri, g, precision=HIGHEST, preferred_element_type=f32)
        bt = jnp.dot(bsum, g, precision=HIGHEST, preferred_element_type=f32)
        qb = q * jnp.exp(b)
        kd = (k * jnp.exp(bt - b)).astype(bf16)
        ki = (k * jnp.exp(jnp.minimum(-b, EXP_CLAMP))).astype(bf16)
        vb = v.astype(bf16)
        qs = jnp.concatenate([jnp.where(lane_q == h, qb, 0.0) for h in range(H_A)], axis=0).astype(bf16)
        att = lax.dot_general(qs, ki, NT, preferred_element_type=f32)
        att = jnp.where(causal4, att, 0.0).astype(bf16)
        res = jnp.dot(att, vb, preferred_element_type=f32)
        o = jnp.where(lane_v == 0, res[0:R], 0.0)
        for h in range(1, H_A):
            o = o + jnp.where(lane_v == h, res[h * R:(h + 1) * R], 0.0)
        qbb = qb.astype(bf16)
        parts = []
        for i in range(R // C):
            sl = slice(i * C, (i + 1) * C)
            parts.append(lax.dot_general(qbb[sl], st.astype(bf16), NT, preferred_element_type=f32))
            kv = lax.dot_general(vb[sl], kd[sl], TN, preferred_element_type=f32)
            st = st * jnp.exp(bt[i * C:i * C + 1, :]) + jnp.where(bd, kv, 0.0)
        o = o + jnp.concatenate(parts, axis=0)
        ms = jnp.dot(o * o, hmean, precision=HIGHEST, preferred_element_type=f32)
        y = o * lax.rsqrt(ms + EPS) * gn_ref[...]
        o_ref[pl.ds(r0, R), :] = (y * _silu(r)).astype(o_ref.dtype)
        return st

    st = lax.fori_loop(0, a_ref.shape[0] // R, block, st_ref[...])
    st_ref[...] = st
    s_ref[...] = st


def _gla_prompt_call(a, wg, bg, gn, nb, tm):
    n = a.shape[0]
    tpb = n // nb // tm
    return pl.pallas_call(
        _gla_prompt_kernel,
        grid=(nb, tpb),
        in_specs=[
            pl.BlockSpec((tm, A_WIDTH), lambda b, j: (b * tpb + j, 0)),
            pl.BlockSpec((128, QA), lambda b, j: (0, 0)),
            pl.BlockSpec((1, QA), lambda b, j: (0, 0)),
            pl.BlockSpec((1, VA), lambda b, j: (0, 0)),
        ],
        out_specs=[
            pl.BlockSpec((tm, VA), lambda b, j: (b * tpb + j, 0)),
            pl.BlockSpec((None, VA, QA), lambda b, j: (b, 0, 0)),
        ],
        out_shape=[jax.ShapeDtypeStruct((n, VA), bf16), jax.ShapeDtypeStruct((nb, VA, QA), f32)],
        scratch_shapes=[pltpu.VMEM((VA, QA), f32)],
        compiler_params=_cparams("arbitrary", "arbitrary"),
        name="gla_prompt",
    )(a, wg, bg, gn)


def _state_from_blockdiag(st):
    b = st.shape[0]
    st5 = st.reshape(b, H_A, DV_A, H_A, DK_A)
    return jnp.stack([jnp.swapaxes(st5[:, h, :, h, :], -1, -2) for h in range(H_A)], axis=1)


def _gla_sample_kernel(a_ref, at_ref, wgt_ref, bgt_ref, gn_ref, s_ref, o_ref, so_ref):
    nb = a_ref.shape[0]
    qt = at_ref[0:QA, :] * (DK_A ** -0.5)
    kt = at_ref[QA:2 * QA, :]
    xt = jnp.dot(wgt_ref[...].astype(bf16), at_ref[2 * QA + 2 * VA:, :].astype(bf16), preferred_element_type=f32) + bgt_ref[...]
    et = jnp.exp((jnp.minimum(xt, 0.0) - jnp.log1p(jnp.exp(-jnp.abs(xt)))) * (1.0 / GATE_TAU))
    gn = gn_ref[...]
    for t in range(nb):
        row = a_ref[t:t + 1, :]
        vt = jnp.concatenate(
            [jnp.broadcast_to(row[:, 2 * QA + h * DV_A:2 * QA + (h + 1) * DV_A], (DK_A, DV_A)) for h in range(H_A)], axis=0)
        s_new = s_ref[t] * et[:, t:t + 1] + kt[:, t:t + 1] * vt
        so_ref[t] = s_new
        o = jnp.sum((qt[:, t:t + 1] * s_new).reshape(H_A, DK_A, DV_A), axis=1)
        y = _rmsnorm(o, gn)
        r = jnp.concatenate([row[:, 2 * QA + VA + h * DV_A:2 * QA + VA + (h + 1) * DV_A] for h in range(H_A)], axis=0)
        o_ref[t] = y * _silu(r)


def _gla_sample_call(a, wg, bg, gn, state):
    nb = a.shape[0]
    return pl.pallas_call(
        _gla_sample_kernel,
        out_shape=[jax.ShapeDtypeStruct((nb, H_A, DV_A), f32), jax.ShapeDtypeStruct((nb, QA, DV_A), f32)],
        compiler_params=pltpu.CompilerParams(vmem_limit_bytes=VMEM_LIMIT),
        name="gla_sample",
    )(a, a.T, wg.T, bg.reshape(QA, 1), gn.reshape(H_A, DV_A), state.reshape(nb, QA, DV_A))


def _rel_bucket_np(dist):
    max_exact = N_BUCKETS // 2
    d = np.maximum(dist, 0)
    df = np.maximum(d, 1).astype(np.float32)
    large = max_exact + (np.log(df / np.float32(max_exact)) / np.float32(math.log(MAX_DISTANCE / max_exact))
                         * np.float32(N_BUCKETS - max_exact)).astype(np.int32)
    large = np.minimum(large, N_BUCKETS - 1)
    return np.where(d < max_exact, d, large).astype(np.int32)


def _prompt_bias(rel_bias, dil):
    i = np.arange(BLK)[:, None]
    j = np.arange(2 * BLK)[None, :]
    delta = BLK + i - j
    band = (delta >= 0) & (delta <= BLK)
    tbl = jnp.take(rel_bias.astype(f32), jnp.asarray(_rel_bucket_np(delta * dil)), axis=0)
    tbl = jnp.where(jnp.asarray(band)[:, :, None], tbl, NEG)
    return jnp.transpose(tbl, (2, 0, 1)).reshape(N_PAIRS, 2 * BLK, 2 * BLK)


def _dil_prompt_kernel(q_ref, kp_ref, kc_ref, vp_ref, vc_ref, bias_ref, o_ref, lse_ref):
    lane = lax.broadcasted_iota(jnp.int32, (1, 2 * DH_D), 1)
    lo = lane < DH_D
    col = lax.broadcasted_iota(jnp.int32, (1, 2 * BLK), 1)
    is_first = (pl.program_id(2) == 0).astype(f32)
    first = jnp.where(col < BLK, NEG, 0.0) * is_first
    lse_acc = jnp.zeros((BLK, 128), f32)
    lane_h = lax.broadcasted_iota(jnp.int32, (1, 128), 1)
    zero = jnp.zeros((), bf16)
    for p in range(N_PAIRS):
        cols = slice(p * 2 * DH_D, (p + 1) * 2 * DH_D)
        qp = q_ref[:, cols]
        q2 = jnp.concatenate([jnp.where(lo, qp, zero), jnp.where(lo, zero, qp)], axis=0)
        k2 = jnp.concatenate([kp_ref[:, cols], kc_ref[:, cols]], axis=0)
        v2 = jnp.concatenate([vp_ref[:, cols], vc_ref[:, cols]], axis=0)
        s = lax.dot_general(q2, k2, NT, preferred_element_type=f32) + bias_ref[p] + first
        m = jnp.max(s, axis=-1, keepdims=True)
        e = jnp.exp(s - m)
        den = jnp.sum(e, axis=-1, keepdims=True)
        pv = jnp.dot(e.astype(bf16), v2, preferred_element_type=f32) / den
        o_ref[:, cols] = jnp.where(lo, pv[:BLK], pv[BLK:]).astype(o_ref.dtype)
        lse = m + jnp.log(den)
        lse_acc = jnp.where(lane_h == 2 * p, lse[:BLK], lse_acc)
        lse_acc = jnp.where(lane_h == 2 * p + 1, lse[BLK:], lse_acc)
    lse_ref[...] = lse_acc


def _dil_prompt_call(q, k, v, bias, nb, dil):
    n = q.shape[0]
    s = n // nb
    rows = s // dil
    nblk = rows // BLK
    view = lambda t: t.reshape(nb, rows, dil * t.shape[-1])
    cur = pl.BlockSpec((None, BLK, QD), lambda b, r, i: (b, i, r))
    prev = pl.BlockSpec((None, BLK, QD), lambda b, r, i: (b, jnp.maximum(i - 1, 0), r))
    o, lse = pl.pallas_call(
        _dil_prompt_kernel,
        grid=(nb, dil, nblk),
        in_specs=[cur, prev, cur, prev, cur, _resident((N_PAIRS, 2 * BLK, 2 * BLK), lambda b, r, i: (0, 0, 0))],
        out_specs=[cur, pl.BlockSpec((None, BLK, 128), lambda b, r, i: (b, i, r))],
        out_shape=[jax.ShapeDtypeStruct((nb, rows, dil * QD), bf16), jax.ShapeDtypeStruct((nb, rows, dil * 128), f32)],
        compiler_params=_cparams("arbitrary", "arbitrary", "arbitrary"),
        name=f"dilated_prompt_d{dil}",
    )(view(q), view(k), view(k), view(v), view(v), bias)
    return o.reshape(n, QD), lse.reshape(n, 128)


def _head_expand_np():
    e = np.zeros((128, QD), np.float32)
    for h in range(H_D):
        e[h, h * DH_D:(h + 1) * DH_D] = 1.0
    return e


def _expand(w, e_ref):
    hi = w.astype(bf16)
    lo = (w - hi.astype(f32)).astype(bf16)
    return jnp.dot(hi, e_ref[...], preferred_element_type=f32) + jnp.dot(lo, e_ref[...], preferred_element_type=f32)


def _combine_kernel(o1_ref, o2_ref, o3_ref, l1_ref, l2_ref, l3_ref, e_ref, out_ref):
    l1, l2, l3 = l1_ref[...], l2_ref[...], l3_ref[...]
    m = jnp.maximum(jnp.maximum(l1, l2), l3)
    e1, e2, e3 = jnp.exp(l1 - m), jnp.exp(l2 - m), jnp.exp(l3 - m)
    z = e1 + e2 + e3
    acc = _expand(e1 / z, e_ref) * o1_ref[...].astype(f32)
    acc = acc + _expand(e2 / z, e_ref) * o2_ref[...].astype(f32)
    acc = acc + _expand(e3 / z, e_ref) * o3_ref[...].astype(f32)
    out_ref[...] = acc.astype(out_ref.dtype)


def _combine_call(os, ls, tm):
    n = os[0].shape[0]
    row = lambda i: (i, 0)
    return pl.pallas_call(
        _combine_kernel,
        grid=(n // tm,),
        in_specs=[pl.BlockSpec((tm, QD), row)] * 3 + [pl.BlockSpec((tm, 128), row)] * 3
        + [pl.BlockSpec((128, QD), lambda i: (0, 0))],
        out_specs=pl.BlockSpec((tm, QD), row),
        out_shape=jax.ShapeDtypeStruct((n, QD), bf16),
        compiler_params=_cparams("arbitrary"),
        name="branch_combine",
    )(*os, *ls, jnp.asarray(_head_expand_np(), bf16))


def _sample_bias(rel_bias):
    rb = rel_bias.astype(f32)
    rows = []
    for (_, dil) in DIL_CONFIGS:
        t = jnp.take(rb, jnp.asarray(_rel_bucket_np((BLK - np.arange(BLK)) * dil)), axis=0)
        rows.append(jnp.pad(t, ((0, 0), (0, 128 - H_D))))
    b0 = jnp.pad(jnp.take(rb, jnp.asarray(_rel_bucket_np(np.zeros((1,), np.int64))), axis=0), ((0, 0), (0, 128 - H_D)))
    return jnp.stack(rows), b0


def _dil_sample_kernel(q_ref, kn_ref, vn_ref, k1_ref, k2_ref, k3_ref, v1_ref, v2_ref, v3_ref,
                       bias_ref, b0_ref, e_ref, et_ref, o_ref):
    q = q_ref[...] * (DH_D ** -0.5)
    kn, vn = kn_ref[...], vn_ref[...]

    def rows8(fn, x):
        if x.shape[0] == 1:
            return fn(jnp.broadcast_to(x, (8, x.shape[1])))[0:1]
        return fn(x)

    def per_head(x):
        return rows8(lambda t: jnp.dot(t.astype(bf16), et_ref[...], preferred_element_type=f32), x)

    def expand(w):
        return rows8(lambda t: _expand(t, e_ref), w)

    l0 = per_head(q * kn) + b0_ref[...]
    outs, lses = [], []
    for c, (k_ref, v_ref) in enumerate(((k1_ref, v1_ref), (k2_ref, v2_ref), (k3_ref, v3_ref))):
        lg = per_head(k_ref[...] * q) + bias_ref[c]
        m = jnp.maximum(jnp.max(lg, axis=0, keepdims=True), l0)
        p = jnp.exp(lg - m)
        p0 = jnp.exp(l0 - m)
        den = jnp.sum(p, axis=0, keepdims=True) + p0
        num = jnp.sum(expand(p) * v_ref[...], axis=0, keepdims=True) + expand(p0) * vn
        outs.append(num / expand(den))
        lses.append(m + jnp.log(den))
    m = jnp.maximum(jnp.maximum(lses[0], lses[1]), lses[2])
    es = [jnp.exp(l - m) for l in lses]
    z = es[0] + es[1] + es[2]
    o_ref[...] = expand(es[0] / z) * outs[0] + expand(es[1] / z) * outs[1] + expand(es[2] / z) * outs[2]


def _dil_sample_call(q, kn, vn, cache_k, cache_v, layer, bias, b0):
    nb = q.shape[0]
    depth, _, length = cache_k.shape[:3]
    row3 = pl.BlockSpec((None, 1, QD), lambda b: (b, 0, 0))
    views, specs = [], []
    for cache in (cache_k, cache_v):
        for (_, dil) in DIL_CONFIGS:
            rows = length // dil
            views.append(cache.reshape(depth * nb, rows, dil * QD))
            specs.append(pl.BlockSpec((None, BLK, QD), functools.partial(
                lambda b, last: (layer * nb + b, last, 0), last=rows // BLK - 1)))
    e = _head_expand_np()
    const = lambda shape: pl.BlockSpec(shape, lambda b: (0,) * len(shape))
    out = pl.pallas_call(
        _dil_sample_kernel,
        grid=(nb,),
        in_specs=[row3, row3, row3] + specs + [const((3, BLK, 128)), const((1, 128)), const((128, QD)), const((QD, 128))],
        out_specs=row3,
        out_shape=jax.ShapeDtypeStruct((nb, 1, QD), f32),
        compiler_params=_cparams("arbitrary"),
        name="dilated_sample",
    )(q.reshape(nb, 1, QD), kn.reshape(nb, 1, QD), vn.reshape(nb, 1, QD), *views, bias, b0,
      jnp.asarray(e, bf16), jnp.asarray(e.T, bf16))
    return out.reshape(nb, QD)


def _route(logits_t, rb):
    per_group = N_EXPERTS // N_EXPERT_GROUPS
    sc = jax.nn.sigmoid(logits_t)
    sel = sc + rb
    sel_r = [sel[e:e + 1, :] for e in range(N_EXPERTS)]
    sc_r = [sc[e:e + 1, :] for e in range(N_EXPERTS)]

    def beats(j, i, vals):
        return (vals[j] >= vals[i]) if j < i else (vals[j] > vals[i])

    gs = []
    for g in range(N_EXPERT_GROUPS):
        a, b, c, d = sel_r[per_group * g:per_group * (g + 1)]
        m1, n1, m2, n2 = jnp.maximum(a, b), jnp.minimum(a, b), jnp.maximum(c, d), jnp.minimum(c, d)
        gs.append(jnp.maximum(m1, m2) + jnp.maximum(jnp.minimum(m1, m2), jnp.maximum(n1, n2)))
    nums = []
    for g in range(N_EXPERT_GROUPS):
        grank = sum(beats(j, g, gs).astype(f32) for j in range(N_EXPERT_GROUPS) if j != g)
        vals = sel_r[per_group * g:per_group * (g + 1)]
        for i in range(per_group):
            rank = sum(beats(j, i, vals).astype(f32) for j in range(per_group) if j != i)
            pick = (grank == 0.0) & (rank < 2.0)
            nums.append(jnp.where(pick, sc_r[per_group * g + i], 0.0))
    den = sum(nums)
    return jnp.concatenate(nums, axis=0) / den


def _wo_kernel(x_ref, oa_ref, md_ref, wo_ref, g1_ref, ng_ref, sc_ref, sh_ref, rwt_ref, rb_ref,
               x1_ref, h_ref, gate_ref):
    y = jnp.dot(oa_ref[...].astype(bf16), wo_ref[0:VA, :], preferred_element_type=f32)
    y = y + jnp.dot(md_ref[...].astype(bf16), wo_ref[VA:, :], preferred_element_type=f32)
    x1 = x_ref[...] + g1_ref[...] * y
    x1_ref[...] = x1
    h = _rmsnorm(x1, ng_ref[...]) * (1.0 + sc_ref[...]) + sh_ref[...]
    h_ref[...] = h.astype(bf16)
    logits_t = lax.dot_general(rwt_ref[...], h, NT, precision=HIGHEST, preferred_element_type=f32)
    gate_ref[...] = _route(logits_t, rb_ref[...])


def _wo_call(x2d, oa, md, wo, g1, ng, sc, sh, rwt, rb, tm, tiles_per_batch):
    n, d = x2d.shape
    row = lambda i: (i, 0)
    const = lambda shape: pl.BlockSpec(shape, lambda i: (0,) * len(shape))
    return pl.pallas_call(
        _wo_kernel,
        grid=(n // tm,),
        in_specs=[
            pl.BlockSpec((tm, d), row),
            pl.BlockSpec((tm, VA), row),
            pl.BlockSpec((tm, QD), row),
            _resident((d, d), lambda i: (0, 0)),
            _mod_spec(g1, tiles_per_batch),
            const((1, d)),
            _mod_spec(sc, tiles_per_batch),
            _mod_spec(sh, tiles_per_batch),
            const((N_EXPERTS, d)),
            const((N_EXPERTS, 1)),
        ],
        out_specs=[pl.BlockSpec((tm, d), row), pl.BlockSpec((tm, d), row), pl.BlockSpec((N_EXPERTS, tm), lambda i: (0, i))],
        out_shape=[jax.ShapeDtypeStruct((n, d), f32), jax.ShapeDtypeStruct((n, d), bf16),
                   jax.ShapeDtypeStruct((N_EXPERTS, n), f32)],
        compiler_params=_cparams("arbitrary"),
        name="wo_router",
    )(x2d, oa, md, wo, g1, ng, sc, sh, rwt, rb)


def _moe_dense_kernel(final, h_ref, gate_ref, wg_ref, wu_ref, wd_ref, x1_ref, g2_ref, fg_ref, out_ref, acc_ref):
    e = pl.program_id(1)

    @pl.when(e == 0)
    def _():
        acc_ref[...] = jnp.zeros_like(acc_ref)

    h = h_ref[...]
    hid = _silu(jnp.dot(h, wg_ref[...], preferred_element_type=f32)) * jnp.dot(h, wu_ref[...], preferred_element_type=f32)
    lane = lax.broadcasted_iota(jnp.int32, (1, N_EXPERTS), 1)
    ge = jnp.sum(jnp.where(lane == e, gate_ref[...], 0.0), axis=-1, keepdims=True)
    acc_ref[...] += jnp.dot((hid * ge).astype(bf16), wd_ref[...], preferred_element_type=f32)

    @pl.when(e == pl.num_programs(1) - 1)
    def _():
        x2 = x1_ref[...] + g2_ref[...] * acc_ref[...]
        out_ref[...] = _rmsnorm(x2, fg_ref[...]) if final else x2


def _moe_dense_call(h, gate, wg, wu, wd, x1, g2, fg, final, tm, tiles_per_batch):
    n, d = x1.shape
    ne, _, ff = wg.shape
    row = lambda i, e: (i, 0)
    return pl.pallas_call(
        functools.partial(_moe_dense_kernel, final),
        grid=(n // tm, ne),
        in_specs=[
            pl.BlockSpec((tm, d), row),
            pl.BlockSpec((tm, N_EXPERTS), row),
            pl.BlockSpec((None, d, ff), lambda i, e: (e, 0, 0)),
            pl.BlockSpec((None, d, ff), lambda i, e: (e, 0, 0)),
            pl.BlockSpec((None, ff, d), lambda i, e: (e, 0, 0)),
            pl.BlockSpec((tm, d), row),
            _mod_spec(g2, tiles_per_batch),
            pl.BlockSpec((1, d), lambda i, e: (0, 0)),
        ],
        out_specs=pl.BlockSpec((tm, d), row),
        out_shape=jax.ShapeDtypeStruct((n, d), f32),
        scratch_shapes=[pltpu.VMEM((tm, d), f32)],
        compiler_params=_cparams("arbitrary", "arbitrary"),
        name="moe_dense",
    )(h, gate, wg, wu, wd, x1, g2, fg)


def _reorder_w_in(w_in):
    o = np.cumsum((0, QA, QA, VA, GATE_RANK, VA, QD, QD, QD))
    pad = jnp.zeros(w_in.shape[:-1] + (128 - GATE_RANK,), w_in.dtype)
    parts = [w_in[..., o[0]:o[3]], w_in[..., o[4]:o[5]], w_in[..., o[3]:o[4]], pad, w_in[..., o[5]:o[8]]]
    return jnp.concatenate(parts, axis=-1).astype(bf16)


def _trunk(x, mods, prompt, gla_state, cache_k, cache_v, p):
    nb, s, d = x.shape
    n = nb * s
    depth = p["w_in"].shape[0]
    sh1, sc1, g1, sh2, sc2, g2 = mods
    if prompt:
        tm = 512
        tpb = s // tm
        keep = min(2048, s)
        keep_tiles = keep // tm
    else:
        tm, tpb, keep_tiles = n, 0, 0
    x2d = x.reshape(n, d)
    new_s, new_k, new_v = [], [], []
    for l in range(depth):
        outs = _inproj_call(x2d, sc1[l], sh1[l], p["norm1_g"][l], p["w_in"][l], tm, tpb, keep_tiles)
        a, q, k, v = outs[:4]
        if prompt:
            oa, st = _gla_prompt_call(a, p["wg"][l], p["bg"][l], p["gn"][l], nb, tm)
            new_s.append(_state_from_blockdiag(st))
            new_k.append(outs[4][:, 1:].reshape(nb, keep, H_D, DH_D))
            new_v.append(outs[5][:, 1:].reshape(nb, keep, H_D, DH_D))
            os, ls = zip(*[_dil_prompt_call(q, k, v, p["bias_p"][c], nb, dil) for c, (_, dil) in enumerate(DIL_CONFIGS)])
            md = _combine_call(os, ls, tm)
        else:
            oa, st = _gla_sample_call(a, p["wg"][l], p["bg"][l], p["gn"][l], gla_state[l])
            oa = oa.reshape(n, VA)
            new_s.append(st.reshape(nb, H_A, DK_A, DV_A))
            new_k.append(k.reshape(nb, s, H_D, DH_D))
            new_v.append(v.reshape(nb, s, H_D, DH_D))
            md = _dil_sample_call(q, k, v, cache_k, cache_v, l, *p["bias_s"])
        x1, h2, gate_t = _wo_call(x2d, oa, md, p["w_o"][l], g1[l], p["norm2_g"][l], sc2[l], sh2[l],
                                  p["rwt"], p["rb"], tm, tpb)
        x2d = _moe_dense_call(h2, gate_t.T, p["moe_wg"][l], p["moe_wu"][l], p["moe_wd"][l], x1, g2[l],
                              p["final_g"], l == depth - 1, tm, tpb)
    return x2d.reshape(nb, s, d), jnp.stack(new_s), jnp.stack(new_k), jnp.stack(new_v)


def kernel(x_prompt, x_sample, state_gla, cache_win_k, cache_win_v, c_prompt, c_sample, w_ada, b_ada, norm1_g, norm2_g, w_in, w_gate_up, b_gate, gla_norm_g, w_o, rel_bias, router_w, router_b, moe_w_gate, moe_w_up, moe_w_down, final_norm_g):
    nbp, seq, d = x_prompt.shape
    nbs, dec_seq, _ = x_sample.shape
    depth = w_in.shape[0]
    assert dec_seq == 1 and cache_win_k.shape[2] == DIL_CONFIGS[-1][0], "sample path: one new token over a full window"
    assert seq % DIL_CONFIGS[-1][0] == 0, "prompt length must be a whole number of the widest window"

    c_all = jnp.concatenate([c_prompt, c_sample], axis=0)
    c_all = jnp.pad(c_all, ((0, -c_all.shape[0] % 8), (0, 0)))
    mod = _ada_call(c_all, w_ada, b_ada)
    mods_p = [m[:, :nbp, None, :] for m in jnp.split(mod, 6, axis=-1)]
    mods_s = [m[:, None, nbp:nbp + nbs, :] for m in jnp.split(mod, 6, axis=-1)]

    p = {
        "w_in": _reorder_w_in(w_in),
        "norm1_g": norm1_g.reshape(depth, 1, d),
        "norm2_g": norm2_g.reshape(depth, 1, d),
        "wg": jnp.pad(w_gate_up, ((0, 0), (0, 128 - GATE_RANK), (0, 0))),
        "bg": b_gate.reshape(depth, 1, QA),
        "gn": gla_norm_g.reshape(depth, 1, VA),
        "w_o": w_o.astype(bf16),
        "rwt": router_w.T,
        "rb": router_b.reshape(N_EXPERTS, 1),
        "moe_wg": moe_w_gate.astype(bf16),
        "moe_wu": moe_w_up.astype(bf16),
        "moe_wd": moe_w_down.astype(bf16),
        "final_g": final_norm_g.reshape(1, d),
        "bias_p": [_prompt_bias(rel_bias, dil) for (_, dil) in DIL_CONFIGS],
        "bias_s": _sample_bias(rel_bias),
    }
    y_p, s_p, k_p, v_p = _trunk(x_prompt, mods_p, True, None, None, None, p)
    y_s, s_s, k_s, v_s = _trunk(x_sample, mods_s, False, state_gla, cache_win_k, cache_win_v, p)
    return (y_p, y_s, s_p, k_p, v_p, s_s, k_s, v_s)
```

```python
import functools
import math

import jax
import jax.numpy as jnp
import numpy as np
from jax import lax
from jax.experimental import pallas as pl
from jax.experimental.pallas import tpu as pltpu

f32 = jnp.float32
bf16 = jnp.bfloat16

H_A, DK_A, DV_A = 4, 32, 64
GATE_RANK = 16
GATE_TAU = 16.0
H_D, DH_D = 12, 64
DIL_CONFIGS = ((128, 1), (512, 4), (2048, 16))
N_BUCKETS = 32
MAX_DISTANCE = 2048
N_EXPERTS = 16
N_EXPERT_GROUPS = 4
EPS = 1e-6

QA = H_A * DK_A
VA = H_A * DV_A
QD = H_D * DH_D
A_WIDTH = 2 * QA + 2 * VA + 128
W_IN_COLS = A_WIDTH + 3 * QD
BLK = 128
N_PAIRS = H_D // 2
NEG = -1e30
GLA_SUB = 16
GLA_ROWS = 128
EXP_CLAMP = 80.0
VMEM_LIMIT = 48 * 1024 * 1024

HIGHEST = lax.Precision.HIGHEST
NT = (((1,), (1,)), ((), ()))
TN = (((0,), (0,)), ((), ()))


def _cparams(*sem):
    return pltpu.CompilerParams(dimension_semantics=sem, vmem_limit_bytes=VMEM_LIMIT)


def _resident(shape, index_map):
    return pl.BlockSpec(shape, index_map, pipeline_mode=pl.Buffered(1))


def _silu(x):
    return x * jax.nn.sigmoid(x)


def _rmsnorm(x, g):
    return x * lax.rsqrt(jnp.mean(x * x, axis=-1, keepdims=True) + EPS) * g


def _ada_kernel(c_ref, w_ref, b_ref, o_ref):
    cs = _silu(c_ref[...]).astype(bf16)
    o_ref[...] = jnp.dot(cs, w_ref[...].astype(bf16), preferred_element_type=f32) + b_ref[...]


def _ada_call(c, w_ada, b_ada):
    depth, d, n6 = w_ada.shape
    m = c.shape[0]
    tn = 1536
    return pl.pallas_call(
        _ada_kernel,
        grid=(depth, n6 // tn),
        in_specs=[
            pl.BlockSpec((m, d), lambda l, j: (0, 0)),
            pl.BlockSpec((None, d, tn), lambda l, j: (l, 0, j)),
            pl.BlockSpec((None, 1, tn), lambda l, j: (l, 0, j)),
        ],
        out_specs=pl.BlockSpec((None, m, tn), lambda l, j: (l, 0, j)),
        out_shape=jax.ShapeDtypeStruct((depth, m, n6), f32),
        compiler_params=_cparams("arbitrary", "arbitrary"),
        name="adaln",
    )(c, w_ada, b_ada.reshape(depth, 1, n6))


def _inproj_kernel(prompt, x_ref, sc_ref, sh_ref, g_ref, w_ref, *outs):
    h = _rmsnorm(x_ref[...], g_ref[...])
    h = (h * (1.0 + sc_ref[...]) + sh_ref[...]).astype(bf16)

    def proj(lo, hi):
        return jnp.dot(h, w_ref[:, lo:hi], preferred_element_type=f32)

    a_ref, q_ref, k_ref, v_ref = outs[:4]
    a_ref[...] = proj(0, A_WIDTH)
    q = proj(A_WIDTH, A_WIDTH + QD)
    k = proj(A_WIDTH + QD, A_WIDTH + 2 * QD)
    v = proj(A_WIDTH + 2 * QD, A_WIDTH + 3 * QD)
    if prompt:
        kw_ref, vw_ref = outs[4:]
        q_ref[...] = (q * (DH_D ** -0.5)).astype(bf16)
        k_ref[...] = k.astype(bf16)
        v_ref[...] = v.astype(bf16)
        kw_ref[...] = k
        vw_ref[...] = v
    else:
        q_ref[...] = q
        k_ref[...] = k
        v_ref[...] = v


def _mod_spec(arr, tiles_per_batch):
    rows, d = arr.shape[1], arr.shape[2]
    if tiles_per_batch:
        return pl.BlockSpec((None, rows, d), lambda i, *_: (i // tiles_per_batch, 0, 0))
    return pl.BlockSpec((None, rows, d), lambda i, *_: (0, 0, 0))


def _inproj_call(x2d, sc, sh, g, w, tm, tiles_per_batch, keep_tiles):
    n, d = x2d.shape
    prompt = bool(tiles_per_batch)
    qkv_dt = bf16 if prompt else f32
    row = lambda i: (i, 0)
    out_specs = [pl.BlockSpec((tm, A_WIDTH), row)] + [pl.BlockSpec((tm, QD), row)] * 3
    out_shape = [jax.ShapeDtypeStruct((n, A_WIDTH), f32)] + [jax.ShapeDtypeStruct((n, QD), qkv_dt)] * 3
    if prompt:
        nb = n // (tm * tiles_per_batch)
        first_keep = tiles_per_batch - keep_tiles

        def win(i):
            return (i // tiles_per_batch, jnp.maximum(i % tiles_per_batch - first_keep + 1, 0), 0, 0)

        out_specs += [pl.BlockSpec((None, None, tm, QD), win)] * 2
        out_shape += [jax.ShapeDtypeStruct((nb, keep_tiles + 1, tm, QD), f32)] * 2
    return pl.pallas_call(
        functools.partial(_inproj_kernel, prompt),
        grid=(n // tm,),
        in_specs=[
            pl.BlockSpec((tm, d), row),
            _mod_spec(sc, tiles_per_batch),
            _mod_spec(sh, tiles_per_batch),
            pl.BlockSpec((1, d), lambda i: (0, 0)),
            _resident((d, W_IN_COLS), lambda i: (0, 0)),
        ],
        out_specs=out_specs,
        out_shape=out_shape,
        compiler_params=_cparams("arbitrary"),
        name="inproj_prompt" if prompt else "inproj_sample",
    )(x2d, sc, sh, g, w)


def _log_gate(glr, wg_ref, bg_ref):
    x = jnp.dot(glr.astype(bf16), wg_ref[...].astype(bf16), preferred_element_type=f32) + bg_ref[...]
    return (jnp.minimum(x, 0.0) - jnp.log1p(jnp.exp(-jnp.abs(x)))) * (1.0 / GATE_TAU)


def _gla_prompt_kernel(a_ref, wg_ref, bg_ref, gn_ref, o_ref, s_ref, st_ref):
    R, C = GLA_ROWS, GLA_SUB

    @pl.when(pl.program_id(1) == 0)
    def _():
        st_ref[...] = jnp.zeros_like(st_ref)

    def iota(shape, dim, shift=0):
        return lax.shift_right_logical(lax.broadcasted_iota(jnp.int32, shape, dim), shift)

    sub, lk, lv = int(math.log2(C)), int(math.log2(DK_A)), int(math.log2(DV_A))
    ri = lax.bitwise_and(iota((H_A * R, R), 0), R - 1)
    ci = iota((H_A * R, R), 1)
    same4 = lax.shift_right_logical(ri, sub) == lax.shift_right_logical(ci, sub)
    causal4 = same4 & (ci <= ri)
    tri = jnp.where(causal4, 1.0, 0.0).astype(f32)[0:R]
    bsum = jnp.where(same4, 1.0, 0.0).astype(f32)[0:R]
    lane_q = iota((1, QA), 1, lk)
    lane_v = iota((1, VA), 1, lv)
    bd = iota((VA, QA), 0, lv) == iota((VA, QA), 1, lk)
    hmean = jnp.where(iota((VA, VA), 0, lv) == iota((VA, VA), 1, lv), 1.0 / DV_A, 0.0).astype(f32)

    def block(c, st):
        r0 = pl.multiple_of(c * R, R)
        blk = a_ref[pl.ds(r0, R), :]
        q = blk[:, 0:QA] * (DK_A ** -0.5)
        k = blk[:, QA:2 * QA]
        v = blk[:, 2 * QA:2 * QA + VA]
        r = blk[:, 2 * QA + VA:2 * QA + 2 * VA]
        g = _log_gate(blk[:, 2 * QA + 2 * VA:], wg_ref, bg_ref)
        b = jnp.dot(tri, g, precision=HIGHEST, preferred_element_type=f32)
        bt = jnp.dot(bsum, g, precision=HIGHEST, preferred_element_type=f32)
        qb = q * jnp.exp(b)
        kd = (k * jnp.exp(bt - b)).astype(bf16)
        ki = (k * jnp.exp(jnp.minimum(-b, EXP_CLAMP))).astype(bf16)
        vb = v.astype(bf16)
        qs = jnp.concatenate([jnp.where(lane_q == h, qb, 0.0) for h in range(H_A)], axis=0).astype(bf16)
        att = lax.dot_general(qs, ki, NT, preferred_element_type=f32)
        att = jnp.where(causal4, att, 0.0).astype(bf16)
        res = jnp.dot(att, vb, preferred_element_type=f32)
        o = jnp.where(lane_v == 0, res[0:R], 0.0)
        for h in range(1, H_A):
            o = o + jnp.where(lane_v == h, res[h * R:(h + 1) * R], 0.0)
        qbb = qb.astype(bf16)
        parts = []
        for i in range(R // C):
            sl = slice(i * C, (i + 1) * C)
            parts.append(lax.dot_general(qbb[sl], st.astype(bf16), NT, preferred_element_type=f32))
            kv = lax.dot_general(vb[sl], kd[sl], TN, preferred_element_type=f32)
            st = st * jnp.exp(bt[i * C:i * C + 1, :]) + jnp.where(bd, kv, 0.0)
        o = o + jnp.concatenate(parts, axis=0)
        ms = jnp.dot(o * o, hmean, precision=HIGHEST, preferred_element_type=f32)
        y = o * lax.rsqrt(ms + EPS) * gn_ref[...]
        o_ref[pl.ds(r0, R), :] = (y * _silu(r)).astype(o_ref.dtype)
        return st

    st = lax.fori_loop(0, a_ref.shape[0] // R, block, st_ref[...])
    st_ref[...] = st
    s_ref[...] = st


def _gla_prompt_call(a, wg, bg, gn, nb, tm):
    n = a.shape[0]
    tpb = n // nb // tm
    return pl.pallas_call(
        _gla_prompt_kernel,
        grid=(nb, tpb),
        in_specs=[
            pl.BlockSpec((tm, A_WIDTH), lambda b, j: (b * tpb + j, 0)),
            pl.BlockSpec((128, QA), lambda b, j: (0, 0)),
            pl.BlockSpec((1, QA), lambda b, j: (0, 0)),
            pl.BlockSpec((1, VA), lambda b, j: (0, 0)),
        ],
        out_specs=[
            pl.BlockSpec((tm, VA), lambda b, j: (b * tpb + j, 0)),
            pl.BlockSpec((None, VA, QA), lambda b, j: (b, 0, 0)),
        ],
        out_shape=[jax.ShapeDtypeStruct((n, VA), bf16), jax.ShapeDtypeStruct((nb, VA, QA), f32)],
        scratch_shapes=[pltpu.VMEM((VA, QA), f32)],
        compiler_params=_cparams("arbitrary", "arbitrary"),
        name="gla_prompt",
    )(a, wg, bg, gn)


def _state_from_blockdiag(st):
    b = st.shape[0]
    st5 = st.reshape(b, H_A, DV_A, H_A, DK_A)
    return jnp.stack([jnp.swapaxes(st5[:, h, :, h, :], -1, -2) for h in range(H_A)], axis=1)


def _gla_sample_kernel(a_ref, at_ref, wgt_ref, bgt_ref, gn_ref, s_ref, o_ref, so_ref):
    nb = a_ref.shape[0]
    qt = at_ref[0:QA, :] * (DK_A ** -0.5)
    kt = at_ref[QA:2 * QA, :]
    xt = jnp.dot(wgt_ref[...].astype(bf16), at_ref[2 * QA + 2 * VA:, :].astype(bf16), preferred_element_type=f32) + bgt_ref[...]
    et = jnp.exp((jnp.minimum(xt, 0.0) - jnp.log1p(jnp.exp(-jnp.abs(xt)))) * (1.0 / GATE_TAU))
    gn = gn_ref[...]
    for t in range(nb):
        row = a_ref[t:t + 1, :]
        vt = jnp.concatenate(
            [jnp.broadcast_to(row[:, 2 * QA + h * DV_A:2 * QA + (h + 1) * DV_A], (DK_A, DV_A)) for h in range(H_A)], axis=0)
        s_new = s_ref[t] * et[:, t:t + 1] + kt[:, t:t + 1] * vt
        so_ref[t] = s_new
        o = jnp.sum((qt[:, t:t + 1] * s_new).reshape(H_A, DK_A, DV_A), axis=1)
        y = _rmsnorm(o, gn)
        r = jnp.concatenate([row[:, 2 * QA + VA + h * DV_A:2 * QA + VA + (h + 1) * DV_A] for h in range(H_A)], axis=0)
        o_ref[t] = y * _silu(r)


def _gla_sample_call(a, wg, bg, gn, state):
    nb = a.shape[0]
    return pl.pallas_call(
        _gla_sample_kernel,
        out_shape=[jax.ShapeDtypeStruct((nb, H_A, DV_A), f32), jax.ShapeDtypeStruct((nb, QA, DV_A), f32)],
        compiler_params=pltpu.CompilerParams(vmem_limit_bytes=VMEM_LIMIT),
        name="gla_sample",
    )(a, a.T, wg.T, bg.reshape(QA, 1), gn.reshape(H_A, DV_A), state.reshape(nb, QA, DV_A))


def _rel_bucket_np(dist):
    max_exact = N_BUCKETS // 2
    d = np.maximum(dist, 0)
    df = np.maximum(d, 1).astype(np.float32)
    large = max_exact + (np.log(df / np.float32(max_exact)) / np.float32(math.log(MAX_DISTANCE / max_exact))
                         * np.float32(N_BUCKETS - max_exact)).astype(np.int32)
    large = np.minimum(large, N_BUCKETS - 1)
    return np.where(d < max_exact, d, large).astype(np.int32)


def _bias_lookup(rel_bias, buckets):
    onehot = np.zeros((buckets.size, N_BUCKETS), np.float32)
    onehot[np.arange(buckets.size), buckets.reshape(-1)] = 1.0
    out = jnp.dot(jnp.asarray(onehot), rel_bias.astype(f32), precision=HIGHEST)
    return out.reshape(buckets.shape + (rel_bias.shape[-1],))


def _prompt_bias(rel_bias, dil):
    i = np.arange(BLK)[:, None]
    j = np.arange(2 * BLK)[None, :]
    delta = BLK + i - j
    band = (delta >= 0) & (delta <= BLK)
    tbl = _bias_lookup(rel_bias, _rel_bucket_np(delta * dil))
    tbl = jnp.where(jnp.asarray(band)[:, :, None], tbl, NEG)
    return jnp.transpose(tbl, (2, 0, 1)).reshape(N_PAIRS, 2 * BLK, 2 * BLK)


def _dil_prompt_kernel(q_ref, kp_ref, kc_ref, vp_ref, vc_ref, bias_ref, o_ref, lse_ref):
    lane = lax.broadcasted_iota(jnp.int32, (1, 2 * DH_D), 1)
    lo = lane < DH_D
    col = lax.broadcasted_iota(jnp.int32, (1, 2 * BLK), 1)
    is_first = (pl.program_id(2) == 0).astype(f32)
    first = jnp.where(col < BLK, NEG, 0.0) * is_first
    lse_acc = jnp.zeros((BLK, 128), f32)
    lane_h = lax.broadcasted_iota(jnp.int32, (1, 128), 1)
    zero = jnp.zeros((), bf16)
    for p in range(N_PAIRS):
        cols = slice(p * 2 * DH_D, (p + 1) * 2 * DH_D)
        qp = q_ref[:, cols]
        q2 = jnp.concatenate([jnp.where(lo, qp, zero), jnp.where(lo, zero, qp)], axis=0)
        k2 = jnp.concatenate([kp_ref[:, cols], kc_ref[:, cols]], axis=0)
        v2 = jnp.concatenate([vp_ref[:, cols], vc_ref[:, cols]], axis=0)
        s = lax.dot_general(q2, k2, NT, preferred_element_type=f32) + bias_ref[p] + first
        m = jnp.max(s, axis=-1, keepdims=True)
        e = jnp.exp(s - m)
        den = jnp.sum(e, axis=-1, keepdims=True)
        pv = jnp.dot(e.astype(bf16), v2, preferred_element_type=f32) / den
        o_ref[:, cols] = jnp.where(lo, pv[:BLK], pv[BLK:]).astype(o_ref.dtype)
        lse = m + jnp.log(den)
        lse_acc = jnp.where(lane_h == 2 * p, lse[:BLK], lse_acc)
        lse_acc = jnp.where(lane_h == 2 * p + 1, lse[BLK:], lse_acc)
    lse_ref[...] = lse_acc


def _dil_prompt_call(q, k, v, bias, nb, dil):
    n = q.shape[0]
    s = n // nb
    rows = s // dil
    nblk = rows // BLK
    view = lambda t: t.reshape(nb, rows, dil * t.shape[-1])
    cur = pl.BlockSpec((None, BLK, QD), lambda b, r, i: (b, i, r))
    prev = pl.BlockSpec((None, BLK, QD), lambda b, r, i: (b, jnp.maximum(i - 1, 0), r))
    o, lse = pl.pallas_call(
        _dil_prompt_kernel,
        grid=(nb, dil, nblk),
        in_specs=[cur, prev, cur, prev, cur, _resident((N_PAIRS, 2 * BLK, 2 * BLK), lambda b, r, i: (0, 0, 0))],
        out_specs=[cur, pl.BlockSpec((None, BLK, 128), lambda b, r, i: (b, i, r))],
        out_shape=[jax.ShapeDtypeStruct((nb, rows, dil * QD), bf16), jax.ShapeDtypeStruct((nb, rows, dil * 128), f32)],
        compiler_params=_cparams("arbitrary", "arbitrary", "arbitrary"),
        name=f"dilated_prompt_d{dil}",
    )(view(q), view(k), view(k), view(v), view(v), bias)
    return o.reshape(n, QD), lse.reshape(n, 128)


def _head_expand_np():
    e = np.zeros((128, QD), np.float32)
    for h in range(H_D):
        e[h, h * DH_D:(h + 1) * DH_D] = 1.0
    return e


def _expand(w, e_ref):
    hi = w.astype(bf16)
    lo = (w - hi.astype(f32)).astype(bf16)
    return jnp.dot(hi, e_ref[...], preferred_element_type=f32) + jnp.dot(lo, e_ref[...], preferred_element_type=f32)


def _combine_kernel(o1_ref, o2_ref, o3_ref, l1_ref, l2_ref, l3_ref, e_ref, out_ref):
    l1, l2, l3 = l1_ref[...], l2_ref[...], l3_ref[...]
    m = jnp.maximum(jnp.maximum(l1, l2), l3)
    e1, e2, e3 = jnp.exp(l1 - m), jnp.exp(l2 - m), jnp.exp(l3 - m)
    z = e1 + e2 + e3
    acc = _expand(e1 / z, e_ref) * o1_ref[...].astype(f32)
    acc = acc + _expand(e2 / z, e_ref) * o2_ref[...].astype(f32)
    acc = acc + _expand(e3 / z, e_ref) * o3_ref[...].astype(f32)
    out_ref[...] = acc.astype(out_ref.dtype)


def _combine_call(os, ls, tm):
    n = os[0].shape[0]
    row = lambda i: (i, 0)
    return pl.pallas_call(
        _combine_kernel,
        grid=(n // tm,),
        in_specs=[pl.BlockSpec((tm, QD), row)] * 3 + [pl.BlockSpec((tm, 128), row)] * 3
        + [pl.BlockSpec((128, QD), lambda i: (0, 0))],
        out_specs=pl.BlockSpec((tm, QD), row),
        out_shape=jax.ShapeDtypeStruct((n, QD), bf16),
        compiler_params=_cparams("arbitrary"),
        name="branch_combine",
    )(*os, *ls, jnp.asarray(_head_expand_np(), bf16))


def _sample_bias(rel_bias):
    rows = [_bias_lookup(rel_bias, _rel_bucket_np((BLK - np.arange(BLK)) * dil)) for (_, dil) in DIL_CONFIGS]
    b0 = _bias_lookup(rel_bias, _rel_bucket_np(np.zeros((1,), np.int64)))
    return jnp.stack(rows)[..., None], b0.reshape(H_D, 1)


def _dil_sample_kernel(q_ref, kn_ref, vn_ref, k1_ref, k2_ref, k3_ref, v1_ref, v2_ref, v3_ref,
                       bias_ref, b0_ref, o_ref):
    q = q_ref[...] * (DH_D ** -0.5)
    kn, vn = kn_ref[...], vn_ref[...]
    l0 = jnp.sum(q * kn, axis=-1, keepdims=True) + b0_ref[...]
    outs, lses = [], []
    for c, (k_ref, v_ref) in enumerate(((k1_ref, v1_ref), (k2_ref, v2_ref), (k3_ref, v3_ref))):
        lg = jnp.sum(k_ref[...] * q[None], axis=-1, keepdims=True) + bias_ref[c]
        m = jnp.maximum(jnp.max(lg, axis=0), l0)
        p = jnp.exp(lg - m[None])
        p0 = jnp.exp(l0 - m)
        den = jnp.sum(p, axis=0) + p0
        num = jnp.sum(p * v_ref[...], axis=0) + p0 * vn
        outs.append(num / den)
        lses.append(m + jnp.log(den))
    m = jnp.maximum(jnp.maximum(lses[0], lses[1]), lses[2])
    es = [jnp.exp(l - m) for l in lses]
    z = es[0] + es[1] + es[2]
    o_ref[...] = (es[0] / z) * outs[0] + (es[1] / z) * outs[1] + (es[2] / z) * outs[2]


def _dil_sample_call(q, kn, vn, cache_k, cache_v, layer, bias, b0):
    nb = q.shape[0]
    depth, _, length = cache_k.shape[:3]
    hd = pl.BlockSpec((None, H_D, DH_D), lambda b: (b, 0, 0))
    views, specs = [], []
    for cache in (cache_k, cache_v):
        for (_, dil) in DIL_CONFIGS:
            rows = length // dil
            views.append(cache.reshape(depth, nb, rows, dil, H_D, DH_D))
            specs.append(pl.BlockSpec((None, None, BLK, None, H_D, DH_D), functools.partial(
                lambda b, last: (layer, b, last, 0, 0, 0), last=rows // BLK - 1)))
    const = lambda shape: pl.BlockSpec(shape, lambda b: (0,) * len(shape))
    to_hd = lambda t: t.reshape(nb, H_D, DH_D)
    out = pl.pallas_call(
        _dil_sample_kernel,
        grid=(nb,),
        in_specs=[hd, hd, hd] + specs + [const((3, BLK, H_D, 1)), const((H_D, 1))],
        out_specs=hd,
        out_shape=jax.ShapeDtypeStruct((nb, H_D, DH_D), f32),
        compiler_params=_cparams("arbitrary"),
        name="dilated_sample",
    )(to_hd(q), to_hd(kn), to_hd(vn), *views, bias, b0)
    return out.reshape(nb, QD)


def _route(logits_t, rb):
    per_group = N_EXPERTS // N_EXPERT_GROUPS
    sc = jax.nn.sigmoid(logits_t)
    sel = sc + rb
    sel_r = [sel[e:e + 1, :] for e in range(N_EXPERTS)]
    sc_r = [sc[e:e + 1, :] for e in range(N_EXPERTS)]

    def beats(j, i, vals):
        return (vals[j] >= vals[i]) if j < i else (vals[j] > vals[i])

    gs = []
    for g in range(N_EXPERT_GROUPS):
        a, b, c, d = sel_r[per_group * g:per_group * (g + 1)]
        m1, n1, m2, n2 = jnp.maximum(a, b), jnp.minimum(a, b), jnp.maximum(c, d), jnp.minimum(c, d)
        gs.append(jnp.maximum(m1, m2) + jnp.maximum(jnp.minimum(m1, m2), jnp.maximum(n1, n2)))
    nums = []
    for g in range(N_EXPERT_GROUPS):
        grank = sum(beats(j, g, gs).astype(f32) for j in range(N_EXPERT_GROUPS) if j != g)
        vals = sel_r[per_group * g:per_group * (g + 1)]
        for i in range(per_group):
            rank = sum(beats(j, i, vals).astype(f32) for j in range(per_group) if j != i)
            pick = (grank == 0.0) & (rank < 2.0)
            nums.append(jnp.where(pick, sc_r[per_group * g + i], 0.0))
    den = sum(nums)
    return jnp.concatenate(nums, axis=0) / den


def _wo_kernel(x_ref, oa_ref, md_ref, wo_ref, g1_ref, ng_ref, sc_ref, sh_ref, rwt_ref, rb_ref,
               x1_ref, h_ref, gate_ref):
    y = jnp.dot(oa_ref[...].astype(bf16), wo_ref[0:VA, :], preferred_element_type=f32)
    y = y + jnp.dot(md_ref[...].astype(bf16), wo_ref[VA:, :], preferred_element_type=f32)
    x1 = x_ref[...] + g1_ref[...] * y
    x1_ref[...] = x1
    h = _rmsnorm(x1, ng_ref[...]) * (1.0 + sc_ref[...]) + sh_ref[...]
    h_ref[...] = h.astype(bf16)
    logits_t = lax.dot_general(rwt_ref[...], h, NT, precision=HIGHEST, preferred_element_type=f32)
    gate_ref[...] = _route(logits_t, rb_ref[...])


def _wo_call(x2d, oa, md, wo, g1, ng, sc, sh, rwt, rb, tm, tiles_per_batch):
    n, d = x2d.shape
    row = lambda i: (i, 0)
    const = lambda shape: pl.BlockSpec(shape, lambda i: (0,) * len(shape))
    return pl.pallas_call(
        _wo_kernel,
        grid=(n // tm,),
        in_specs=[
            pl.BlockSpec((tm, d), row),
            pl.BlockSpec((tm, VA), row),
            pl.BlockSpec((tm, QD), row),
            _resident((d, d), lambda i: (0, 0)),
            _mod_spec(g1, tiles_per_batch),
            const((1, d)),
            _mod_spec(sc, tiles_per_batch),
            _mod_spec(sh, tiles_per_batch),
            const((N_EXPERTS, d)),
            const((N_EXPERTS, 1)),
        ],
        out_specs=[pl.BlockSpec((tm, d), row), pl.BlockSpec((tm, d), row), pl.BlockSpec((N_EXPERTS, tm), lambda i: (0, i))],
        out_shape=[jax.ShapeDtypeStruct((n, d), f32), jax.ShapeDtypeStruct((n, d), bf16),
                   jax.ShapeDtypeStruct((N_EXPERTS, n), f32)],
        compiler_params=_cparams("arbitrary"),
        name="wo_router",
    )(x2d, oa, md, wo, g1, ng, sc, sh, rwt, rb)


def _moe_dense_kernel(final, h_ref, gate_ref, wg_ref, wu_ref, wd_ref, x1_ref, g2_ref, fg_ref, out_ref, acc_ref):
    e = pl.program_id(1)

    @pl.when(e == 0)
    def _():
        acc_ref[...] = jnp.zeros_like(acc_ref)

    h = h_ref[...]
    hid = _silu(jnp.dot(h, wg_ref[...], preferred_element_type=f32)) * jnp.dot(h, wu_ref[...], preferred_element_type=f32)
    lane = lax.broadcasted_iota(jnp.int32, (1, N_EXPERTS), 1)
    ge = jnp.sum(jnp.where(lane == e, gate_ref[...], 0.0), axis=-1, keepdims=True)
    acc_ref[...] += jnp.dot((hid * ge).astype(bf16), wd_ref[...], preferred_element_type=f32)

    @pl.when(e == pl.num_programs(1) - 1)
    def _():
        x2 = x1_ref[...] + g2_ref[...] * acc_ref[...]
        out_ref[...] = _rmsnorm(x2, fg_ref[...]) if final else x2


def _moe_dense_call(h, gate, wg, wu, wd, x1, g2, fg, final, tm, tiles_per_batch):
    n, d = x1.shape
    ne, _, ff = wg.shape
    row = lambda i, e: (i, 0)
    return pl.pallas_call(
        functools.partial(_moe_dense_kernel, final),
        grid=(n // tm, ne),
        in_specs=[
            pl.BlockSpec((tm, d), row),
            pl.BlockSpec((tm, N_EXPERTS), row),
            pl.BlockSpec((None, d, ff), lambda i, e: (e, 0, 0)),
            pl.BlockSpec((None, d, ff), lambda i, e: (e, 0, 0)),
            pl.BlockSpec((None, ff, d), lambda i, e: (e, 0, 0)),
            pl.BlockSpec((tm, d), row),
            _mod_spec(g2, tiles_per_batch),
            pl.BlockSpec((1, d), lambda i, e: (0, 0)),
        ],
        out_specs=pl.BlockSpec((tm, d), row),
        out_shape=jax.ShapeDtypeStruct((n, d), f32),
        scratch_shapes=[pltpu.VMEM((tm, d), f32)],
        compiler_params=_cparams("arbitrary", "arbitrary"),
        name="moe_dense",
    )(h, gate, wg, wu, wd, x1, g2, fg)


def _reorder_w_in(w_in):
    o = np.cumsum((0, QA, QA, VA, GATE_RANK, VA, QD, QD, QD))
    pad = jnp.zeros(w_in.shape[:-1] + (128 - GATE_RANK,), w_in.dtype)
    parts = [w_in[..., o[0]:o[3]], w_in[..., o[4]:o[5]], w_in[..., o[3]:o[4]], pad, w_in[..., o[5]:o[8]]]
    return jnp.concatenate(parts, axis=-1).astype(bf16)


def _trunk(x, mods, prompt, gla_state, cache_k, cache_v, p):
    nb, s, d = x.shape
    n = nb * s
    depth = p["w_in"].shape[0]
    sh1, sc1, g1, sh2, sc2, g2 = mods
    if prompt:
        tm = 512
        tpb = s // tm
        keep = min(2048, s)
        keep_tiles = keep // tm
    else:
        tm, tpb, keep_tiles = n, 0, 0
    x2d = x.reshape(n, d)
    new_s, new_k, new_v = [], [], []
    for l in range(depth):
        outs = _inproj_call(x2d, sc1[l], sh1[l], p["norm1_g"][l], p["w_in"][l], tm, tpb, keep_tiles)
        a, q, k, v = outs[:4]
        if prompt:
            oa, st = _gla_prompt_call(a, p["wg"][l], p["bg"][l], p["gn"][l], nb, tm)
            new_s.append(_state_from_blockdiag(st))
            new_k.append(outs[4][:, 1:].reshape(nb, keep, H_D, DH_D))
            new_v.append(outs[5][:, 1:].reshape(nb, keep, H_D, DH_D))
            os, ls = zip(*[_dil_prompt_call(q, k, v, p["bias_p"][c], nb, dil) for c, (_, dil) in enumerate(DIL_CONFIGS)])
            md = _combine_call(os, ls, tm)
        else:
            oa, st = _gla_sample_call(a, p["wg"][l], p["bg"][l], p["gn"][l], gla_state[l])
            oa = oa.reshape(n, VA)
            new_s.append(st.reshape(nb, H_A, DK_A, DV_A))
            new_k.append(k.reshape(nb, s, H_D, DH_D))
            new_v.append(v.reshape(nb, s, H_D, DH_D))
            md = _dil_sample_call(q, k, v, cache_k, cache_v, l, *p["bias_s"])
        x1, h2, gate_t = _wo_call(x2d, oa, md, p["w_o"][l], g1[l], p["norm2_g"][l], sc2[l], sh2[l],
                                  p["rwt"], p["rb"], tm, tpb)
        x2d = _moe_dense_call(h2, gate_t.T, p["moe_wg"][l], p["moe_wu"][l], p["moe_wd"][l], x1, g2[l],
                              p["final_g"], l == depth - 1, tm, tpb)
    return x2d.reshape(nb, s, d), jnp.stack(new_s), jnp.stack(new_k), jnp.stack(new_v)


def kernel(x_prompt, x_sample, state_gla, cache_win_k, cache_win_v, c_prompt, c_sample, w_ada, b_ada, norm1_g, norm2_g, w_in, w_gate_up, b_gate, gla_norm_g, w_o, rel_bias, router_w, router_b, moe_w_gate, moe_w_up, moe_w_down, final_norm_g):
    nbp, seq, d = x_prompt.shape
    nbs, dec_seq, _ = x_sample.shape
    depth = w_in.shape[0]
    assert dec_seq == 1 and cache_win_k.shape[2] == DIL_CONFIGS[-1][0], "sample path: one new token over a full window"
    assert seq % DIL_CONFIGS[-1][0] == 0, "prompt length must be a whole number of the widest window"

    c_all = jnp.concatenate([c_prompt, c_sample], axis=0)
    c_all = jnp.pad(c_all, ((0, -c_all.shape[0] % 8), (0, 0)))
    mod = _ada_call(c_all, w_ada, b_ada)
    mods_p = [m[:, :nbp, None, :] for m in jnp.split(mod, 6, axis=-1)]
    mods_s = [m[:, None, nbp:nbp + nbs, :] for m in jnp.split(mod, 6, axis=-1)]

    p = {
        "w_in": _reorder_w_in(w_in),
        "norm1_g": norm1_g.reshape(depth, 1, d),
        "norm2_g": norm2_g.reshape(depth, 1, d),
        "wg": jnp.pad(w_gate_up, ((0, 0), (0, 128 - GATE_RANK), (0, 0))),
        "bg": b_gate.reshape(depth, 1, QA),
        "gn": gla_norm_g.reshape(depth, 1, VA),
        "w_o": w_o.astype(bf16),
        "rwt": router_w.T,
        "rb": router_b.reshape(N_EXPERTS, 1),
        "moe_wg": moe_w_gate.astype(bf16),
        "moe_wu": moe_w_up.astype(bf16),
        "moe_wd": moe_w_down.astype(bf16),
        "final_g": final_norm_g.reshape(1, d),
        "bias_p": [_prompt_bias(rel_bias, dil) for (_, dil) in DIL_CONFIGS],
        "bias_s": _sample_bias(rel_bias),
    }
    y_p, s_p, k_p, v_p = _trunk(x_prompt, mods_p, True, None, None, None, p)
    y_s, s_s, k_s, v_s = _trunk(x_sample, mods_s, False, state_gla, cache_win_k, cache_win_v, p)
    return (y_p, y_s, s_p, k_p, v_p, s_s, k_s, v_s)
```

```python
import functools
import math

import jax
import jax.numpy as jnp
import numpy as np
from jax import lax
from jax.experimental import pallas as pl
from jax.experimental.pallas import tpu as pltpu

f32 = jnp.float32
bf16 = jnp.bfloat16

H_A, DK_A, DV_A = 4, 32, 64
GATE_RANK = 16
GATE_TAU = 16.0
H_D, DH_D = 12, 64
DIL_CONFIGS = ((128, 1), (512, 4), (2048, 16))
N_BUCKETS = 32
MAX_DISTANCE = 2048
N_EXPERTS = 16
N_EXPERT_GROUPS = 4
EPS = 1e-6

QA = H_A * DK_A
VA = H_A * DV_A
QD = H_D * DH_D
A_WIDTH = 2 * QA + 2 * VA + 128
W_IN_COLS = A_WIDTH + 3 * QD
BLK = 128
N_PAIRS = H_D // 2
NEG = -1e30
GLA_SUB = 16
GLA_ROWS = 128
EXP_CLAMP = 80.0
VMEM_LIMIT = 48 * 1024 * 1024

HIGHEST = lax.Precision.HIGHEST
NT = (((1,), (1,)), ((), ()))
TN = (((0,), (0,)), ((), ()))


def _cparams(*sem):
    return pltpu.CompilerParams(dimension_semantics=sem, vmem_limit_bytes=VMEM_LIMIT)


def _resident(shape, index_map):
    return pl.BlockSpec(shape, index_map, pipeline_mode=pl.Buffered(1))


def _silu(x):
    return x * jax.nn.sigmoid(x)


def _split_bf16(x, terms):
    parts = []
    for _ in range(terms):
        p = x.astype(bf16)
        parts.append(p)
        x = x - p.astype(f32)
    return parts


def _rmsnorm(x, g):
    return x * lax.rsqrt(jnp.mean(x * x, axis=-1, keepdims=True) + EPS) * g


def _ada_kernel(c_ref, w_ref, b_ref, o_ref):
    cs = _silu(c_ref[...]).astype(bf16)
    o_ref[...] = jnp.dot(cs, w_ref[...].astype(bf16), preferred_element_type=f32) + b_ref[...]


def _ada_call(c, w_ada, b_ada):
    depth, d, n6 = w_ada.shape
    m = c.shape[0]
    tn = 1536
    return pl.pallas_call(
        _ada_kernel,
        grid=(depth, n6 // tn),
        in_specs=[
            pl.BlockSpec((m, d), lambda l, j: (0, 0)),
            pl.BlockSpec((None, d, tn), lambda l, j: (l, 0, j)),
            pl.BlockSpec((None, 1, tn), lambda l, j: (l, 0, j)),
        ],
        out_specs=pl.BlockSpec((None, m, tn), lambda l, j: (l, 0, j)),
        out_shape=jax.ShapeDtypeStruct((depth, m, n6), f32),
        compiler_params=_cparams("arbitrary", "arbitrary"),
        name="adaln",
    )(c, w_ada, b_ada.reshape(depth, 1, n6))


def _inproj_kernel(prompt, x_ref, sc_ref, sh_ref, g_ref, w_ref, *outs):
    h = _rmsnorm(x_ref[...], g_ref[...])
    h = (h * (1.0 + sc_ref[...]) + sh_ref[...]).astype(bf16)

    def proj(lo, hi):
        return jnp.dot(h, w_ref[:, lo:hi], preferred_element_type=f32)

    a_ref, q_ref, k_ref, v_ref = outs[:4]
    a_ref[...] = proj(0, A_WIDTH)
    q = proj(A_WIDTH, A_WIDTH + QD)
    k = proj(A_WIDTH + QD, A_WIDTH + 2 * QD)
    v = proj(A_WIDTH + 2 * QD, A_WIDTH + 3 * QD)
    if prompt:
        n_cls = len(STRIDED_DILS)
        kw_ref, vw_ref = outs[4:6]
        cls_refs = outs[6:6 + 3 * n_cls]
        scratch = outs[6 + 3 * n_cls:]
        q = q * (DH_D ** -0.5)
        kw_ref[...] = k
        vw_ref[...] = v
        for j, (val, nat_ref) in enumerate(((q, q_ref), (k, k_ref), (v, v_ref))):
            nat_ref[...] = val.astype(bf16)
            _to_class_major(val, scratch[j], [cls_refs[3 * c + j] for c in range(n_cls)])
    else:
        q_ref[...] = q
        k_ref[...] = k
        v_ref[...] = v


STRIDED_DILS = tuple(dil for (_, dil) in DIL_CONFIGS if dil > 1)
LANE_TILES = QD // 128


def _to_class_major(val, planes, dst_refs):
    rows = val.shape[0]
    for c in range(LANE_TILES):
        planes[c] = val[:, c * 128:(c + 1) * 128]
    for dil, dst in zip(STRIDED_DILS, dst_refs):
        for c in range(LANE_TILES):
            for r in range(dil):
                lo = r * QD + c * 128
                dst[:, lo:lo + 128] = planes[c, pl.ds(r, rows // dil, stride=dil), :].astype(dst.dtype)


def _from_class_major(src_ref, planes, dil, width):
    rows = planes.shape[1]
    for c in range(width // 128):
        for r in range(dil):
            lo = r * width + c * 128
            planes[c, pl.ds(r, rows // dil, stride=dil), :] = src_ref[:, lo:lo + 128].astype(f32)


def _mod_spec(arr, tiles_per_batch):
    rows, d = arr.shape[1], arr.shape[2]
    if tiles_per_batch:
        return pl.BlockSpec((None, rows, d), lambda i, *_: (i // tiles_per_batch, 0, 0))
    return pl.BlockSpec((None, rows, d), lambda i, *_: (0, 0, 0))


def _inproj_call(x2d, sc, sh, g, w, tm, tiles_per_batch, keep_tiles):
    n, d = x2d.shape
    prompt = bool(tiles_per_batch)
    qkv_dt = bf16 if prompt else f32
    row = lambda i: (i, 0)
    out_specs = [pl.BlockSpec((tm, A_WIDTH), row)] + [pl.BlockSpec((tm, QD), row)] * 3
    out_shape = [jax.ShapeDtypeStruct((n, A_WIDTH), f32)] + [jax.ShapeDtypeStruct((n, QD), qkv_dt)] * 3
    scratch = []
    if prompt:
        nb = n // (tm * tiles_per_batch)
        first_keep = tiles_per_batch - keep_tiles

        def win(i):
            return (i // tiles_per_batch, jnp.maximum(i % tiles_per_batch - first_keep + 1, 0), 0, 0)

        out_specs += [pl.BlockSpec((None, None, tm, QD), win)] * 2
        out_shape += [jax.ShapeDtypeStruct((nb, keep_tiles + 1, tm, QD), f32)] * 2
        for dil in STRIDED_DILS:
            out_specs += [pl.BlockSpec((tm // dil, dil * QD), row)] * 3
            out_shape += [jax.ShapeDtypeStruct((n // dil, dil * QD), bf16)] * 3
        scratch = [pltpu.VMEM((LANE_TILES, tm, 128), f32)] * 3
    return pl.pallas_call(
        functools.partial(_inproj_kernel, prompt),
        grid=(n // tm,),
        in_specs=[
            pl.BlockSpec((tm, d), row),
            _mod_spec(sc, tiles_per_batch),
            _mod_spec(sh, tiles_per_batch),
            pl.BlockSpec((1, d), lambda i: (0, 0)),
            _resident((d, W_IN_COLS), lambda i: (0, 0)),
        ],
        out_specs=out_specs,
        out_shape=out_shape,
        scratch_shapes=scratch,
        compiler_params=_cparams("arbitrary"),
        name="inproj_prompt" if prompt else "inproj_sample",
    )(x2d, sc, sh, g, w)


def _log_gate(glr, wg_ref, bg_ref):
    x = jnp.dot(glr.astype(bf16), wg_ref[...].astype(bf16), preferred_element_type=f32) + bg_ref[...]
    return (jnp.minimum(x, 0.0) - jnp.log1p(jnp.exp(-jnp.abs(x)))) * (1.0 / GATE_TAU)


def _gla_prompt_kernel(a_ref, wg_ref, bg_ref, gn_ref, o_ref, s_ref, st_ref):
    R, C = GLA_ROWS, GLA_SUB

    @pl.when(pl.program_id(1) == 0)
    def _():
        st_ref[...] = jnp.zeros_like(st_ref)

    def iota(shape, dim, shift=0):
        return lax.shift_right_logical(lax.broadcasted_iota(jnp.int32, shape, dim), shift)

    sub, lk, lv = int(math.log2(C)), int(math.log2(DK_A)), int(math.log2(DV_A))
    ri = lax.bitwise_and(iota((H_A * R, R), 0), R - 1)
    ci = iota((H_A * R, R), 1)
    same4 = lax.shift_right_logical(ri, sub) == lax.shift_right_logical(ci, sub)
    causal4 = same4 & (ci <= ri)
    sums = jnp.concatenate([jnp.where(causal4, 1.0, 0.0)[0:R], jnp.where(same4, 1.0, 0.0)[0:R]], axis=0).astype(bf16)
    lane_q = iota((1, QA), 1, lk)
    lane_v = iota((1, VA), 1, lv)
    bd = iota((VA, QA), 0, lv) == iota((VA, QA), 1, lk)
    hmean = jnp.where(iota((VA, VA), 0, lv) == iota((VA, VA), 1, lv), 1.0 / DV_A, 0.0).astype(bf16)

    def block(c, st):
        rows = slice(c * R, (c + 1) * R)
        blk = a_ref[rows, :]
        q = blk[:, 0:QA] * (DK_A ** -0.5)
        k = blk[:, QA:2 * QA]
        v = blk[:, 2 * QA:2 * QA + VA]
        r = blk[:, 2 * QA + VA:2 * QA + 2 * VA]
        g = _log_gate(blk[:, 2 * QA + 2 * VA:], wg_ref, bg_ref)
        cums = sum(jnp.dot(sums, part, preferred_element_type=f32) for part in _split_bf16(g, 3))
        b, bt = cums[0:R], cums[R:2 * R]
        qb = q * jnp.exp(b)
        kd = (k * jnp.exp(bt - b)).astype(bf16)
        ki = (k * jnp.exp(jnp.minimum(-b, EXP_CLAMP))).astype(bf16)
        vb = v.astype(bf16)
        qs = jnp.concatenate([jnp.where(lane_q == h, qb, 0.0) for h in range(H_A)], axis=0).astype(bf16)
        att = lax.dot_general(qs, ki, NT, preferred_element_type=f32)
        att = jnp.where(causal4, att, 0.0).astype(bf16)
        res = jnp.dot(att, vb, preferred_element_type=f32)
        o = jnp.where(lane_v == 0, res[0:R], 0.0)
        for h in range(1, H_A):
            o = o + jnp.where(lane_v == h, res[h * R:(h + 1) * R], 0.0)
        qbb = qb.astype(bf16)
        parts = []
        for i in range(R // C):
            sl = slice(i * C, (i + 1) * C)
            parts.append(lax.dot_general(qbb[sl], st.astype(bf16), NT, preferred_element_type=f32))
            kv = lax.dot_general(vb[sl], kd[sl], TN, preferred_element_type=f32)
            st = st * jnp.exp(bt[i * C:i * C + 1, :]) + jnp.where(bd, kv, 0.0)
        o = o + jnp.concatenate(parts, axis=0)
        ms = sum(jnp.dot(part, hmean, preferred_element_type=f32) for part in _split_bf16(o * o, 2))
        y = o * lax.rsqrt(ms + EPS) * gn_ref[...]
        o_ref[rows, :] = (y * _silu(r)).astype(o_ref.dtype)
        return st

    st = st_ref[...]
    for c in range(a_ref.shape[0] // R):
        st = block(c, st)
    st_ref[...] = st
    s_ref[...] = st


def _gla_prompt_call(a, wg, bg, gn, nb, tm):
    n = a.shape[0]
    tpb = n // nb // tm
    return pl.pallas_call(
        _gla_prompt_kernel,
        grid=(nb, tpb),
        in_specs=[
            pl.BlockSpec((tm, A_WIDTH), lambda b, j: (b * tpb + j, 0)),
            pl.BlockSpec((128, QA), lambda b, j: (0, 0)),
            pl.BlockSpec((1, QA), lambda b, j: (0, 0)),
            pl.BlockSpec((1, VA), lambda b, j: (0, 0)),
        ],
        out_specs=[
            pl.BlockSpec((tm, VA), lambda b, j: (b * tpb + j, 0)),
            pl.BlockSpec((None, VA, QA), lambda b, j: (b, 0, 0)),
        ],
        out_shape=[jax.ShapeDtypeStruct((n, VA), bf16), jax.ShapeDtypeStruct((nb, VA, QA), f32)],
        scratch_shapes=[pltpu.VMEM((VA, QA), f32)],
        compiler_params=_cparams("arbitrary", "arbitrary"),
        name="gla_prompt",
    )(a, wg, bg, gn)


def _state_from_blockdiag(st):
    b = st.shape[0]
    st5 = st.reshape(b, H_A, DV_A, H_A, DK_A)
    return jnp.stack([jnp.swapaxes(st5[:, h, :, h, :], -1, -2) for h in range(H_A)], axis=1)


def _gla_sample_kernel(a_ref, at_ref, wgt_ref, bgt_ref, gn_ref, s_ref, o_ref, so_ref):
    nb = a_ref.shape[0]
    qt = at_ref[0:QA, :] * (DK_A ** -0.5)
    kt = at_ref[QA:2 * QA, :]
    xt = jnp.dot(wgt_ref[...].astype(bf16), at_ref[2 * QA + 2 * VA:, :].astype(bf16), preferred_element_type=f32) + bgt_ref[...]
    et = jnp.exp((jnp.minimum(xt, 0.0) - jnp.log1p(jnp.exp(-jnp.abs(xt)))) * (1.0 / GATE_TAU))
    gn = gn_ref[...]
    for t in range(nb):
        row = a_ref[t:t + 1, :]
        vt = jnp.concatenate(
            [jnp.broadcast_to(row[:, 2 * QA + h * DV_A:2 * QA + (h + 1) * DV_A], (DK_A, DV_A)) for h in range(H_A)], axis=0)
        s_new = s_ref[t] * et[:, t:t + 1] + kt[:, t:t + 1] * vt
        so_ref[t] = s_new
        o = jnp.sum((qt[:, t:t + 1] * s_new).reshape(H_A, DK_A, DV_A), axis=1)
        y = _rmsnorm(o, gn)
        r = jnp.concatenate([row[:, 2 * QA + VA + h * DV_A:2 * QA + VA + (h + 1) * DV_A] for h in range(H_A)], axis=0)
        o_ref[t] = y * _silu(r)


def _gla_sample_call(a, wg, bg, gn, state):
    nb = a.shape[0]
    return pl.pallas_call(
        _gla_sample_kernel,
        out_shape=[jax.ShapeDtypeStruct((nb, H_A, DV_A), f32), jax.ShapeDtypeStruct((nb, QA, DV_A), f32)],
        compiler_params=pltpu.CompilerParams(vmem_limit_bytes=VMEM_LIMIT),
        name="gla_sample",
    )(a, a.T, wg.T, bg.reshape(QA, 1), gn.reshape(H_A, DV_A), state.reshape(nb, QA, DV_A))


def _rel_bucket_np(dist):
    max_exact = N_BUCKETS // 2
    d = np.maximum(dist, 0)
    df = np.maximum(d, 1).astype(np.float32)
    large = max_exact + (np.log(df / np.float32(max_exact)) / np.float32(math.log(MAX_DISTANCE / max_exact))
                         * np.float32(N_BUCKETS - max_exact)).astype(np.int32)
    large = np.minimum(large, N_BUCKETS - 1)
    return np.where(d < max_exact, d, large).astype(np.int32)


def _bias_lookup(rel_bias, buckets):
    onehot = np.zeros((buckets.size, N_BUCKETS), np.float32)
    onehot[np.arange(buckets.size), buckets.reshape(-1)] = 1.0
    out = jnp.dot(jnp.asarray(onehot), rel_bias.astype(f32), precision=HIGHEST)
    return out.reshape(buckets.shape + (rel_bias.shape[-1],))


def _prompt_bias(rel_bias, dil):
    i = np.arange(BLK)[:, None]
    j = np.arange(2 * BLK)[None, :]
    delta = BLK + i - j
    band = (delta >= 0) & (delta <= BLK)
    tbl = _bias_lookup(rel_bias, _rel_bucket_np(delta * dil))
    tbl = jnp.where(jnp.asarray(band)[:, :, None], tbl, NEG)
    return jnp.transpose(tbl, (2, 0, 1)).reshape(N_PAIRS, 2 * BLK, 2 * BLK)


def _dil_prompt_kernel(q_ref, kp_ref, kc_ref, vp_ref, vc_ref, bias_ref, o_ref, lse_ref):
    lane = lax.broadcasted_iota(jnp.int32, (1, 2 * DH_D), 1)
    lo = lane < DH_D
    col = lax.broadcasted_iota(jnp.int32, (1, 2 * BLK), 1)
    is_first = (pl.program_id(2) == 0).astype(f32)
    first = jnp.where(col < BLK, NEG, 0.0) * is_first
    lse_acc = jnp.zeros((BLK, 128), f32)
    lane_h = lax.broadcasted_iota(jnp.int32, (1, 128), 1)
    zero = jnp.zeros((), bf16)
    for p in range(N_PAIRS):
        cols = slice(p * 2 * DH_D, (p + 1) * 2 * DH_D)
        qp = q_ref[:, cols]
        q2 = jnp.concatenate([jnp.where(lo, qp, zero), jnp.where(lo, zero, qp)], axis=0)
        k2 = jnp.concatenate([kp_ref[:, cols], kc_ref[:, cols]], axis=0)
        v2 = jnp.concatenate([vp_ref[:, cols], vc_ref[:, cols]], axis=0)
        s = lax.dot_general(q2, k2, NT, preferred_element_type=f32) + bias_ref[p] + first
        m = jnp.max(s, axis=-1, keepdims=True)
        e = jnp.exp(s - m)
        den = jnp.sum(e, axis=-1, keepdims=True)
        pv = jnp.dot(e.astype(bf16), v2, preferred_element_type=f32) / den
        o_ref[:, cols] = jnp.where(lo, pv[:BLK], pv[BLK:]).astype(o_ref.dtype)
        lse = m + jnp.log(den)
        lse_acc = jnp.where(lane_h == 2 * p, lse[:BLK], lse_acc)
        lse_acc = jnp.where(lane_h == 2 * p + 1, lse[BLK:], lse_acc)
    lse_ref[...] = lse_acc


def _dil_prompt_call(q, k, v, bias, nb, dil):
    rows = q.shape[0] // nb
    nblk = rows // BLK
    view = lambda t: t.reshape(nb, rows, t.shape[-1])
    cur = pl.BlockSpec((None, BLK, QD), lambda b, r, i: (b, i, r))
    prev = pl.BlockSpec((None, BLK, QD), lambda b, r, i: (b, jnp.maximum(i - 1, 0), r))
    o, lse = pl.pallas_call(
        _dil_prompt_kernel,
        grid=(nb, dil, nblk),
        in_specs=[cur, prev, cur, prev, cur, _resident((N_PAIRS, 2 * BLK, 2 * BLK), lambda b, r, i: (0, 0, 0))],
        out_specs=[cur, pl.BlockSpec((None, BLK, 128), lambda b, r, i: (b, i, r))],
        out_shape=[jax.ShapeDtypeStruct((nb, rows, dil * QD), bf16), jax.ShapeDtypeStruct((nb, rows, dil * 128), f32)],
        compiler_params=_cparams("arbitrary", "arbitrary", "arbitrary"),
        name=f"dilated_prompt_d{dil}",
    )(view(q), view(k), view(k), view(v), view(v), bias)
    return o.reshape(nb * rows, dil * QD), lse.reshape(nb * rows, dil * 128)


def _head_expand_np():
    e = np.zeros((128, QD), np.float32)
    for h in range(H_D):
        e[h, h * DH_D:(h + 1) * DH_D] = 1.0
    return e


def _expand(w, e_ref):
    hi = w.astype(bf16)
    lo = (w - hi.astype(f32)).astype(bf16)
    return jnp.dot(hi, e_ref[...], preferred_element_type=f32) + jnp.dot(lo, e_ref[...], preferred_element_type=f32)


def _combine_kernel(*refs):
    nbr = len(DIL_CONFIGS)
    o_refs, l_refs = refs[:nbr], refs[nbr:2 * nbr]
    e_ref, out_ref = refs[2 * nbr], refs[2 * nbr + 1]
    scratch = refs[2 * nbr + 2:]
    o_planes, lses, used = [], [], 0
    for c, (_, dil) in enumerate(DIL_CONFIGS):
        if dil == 1:
            o_planes.append(None)
            lses.append(l_refs[c][...])
        else:
            op, lp = scratch[used], scratch[used + 1]
            used += 2
            _from_class_major(o_refs[c], op, dil, QD)
            _from_class_major(l_refs[c], lp, dil, 128)
            o_planes.append(op)
            lses.append(lp[0])
    m = functools.reduce(jnp.maximum, lses)
    es = [jnp.exp(l - m) for l in lses]
    z = functools.reduce(lambda a, b: a + b, es)
    ws = [_expand(e / z, e_ref) for e in es]
    for t in range(LANE_TILES):
        cols = slice(t * 128, (t + 1) * 128)
        acc = None
        for c in range(nbr):
            o = o_refs[c][:, cols].astype(f32) if o_planes[c] is None else o_planes[c][t]
            acc = ws[c][:, cols] * o if acc is None else acc + ws[c][:, cols] * o
        out_ref[:, cols] = acc.astype(out_ref.dtype)


def _combine_call(os, ls, tm):
    n = os[0].shape[0] * DIL_CONFIGS[0][1]
    row = lambda i: (i, 0)
    o_specs = [pl.BlockSpec((tm // dil, dil * QD), row) for (_, dil) in DIL_CONFIGS]
    l_specs = [pl.BlockSpec((tm // dil, dil * 128), row) for (_, dil) in DIL_CONFIGS]
    scratch = []
    for (_, dil) in DIL_CONFIGS:
        if dil > 1:
            scratch += [pltpu.VMEM((LANE_TILES, tm, 128), f32), pltpu.VMEM((1, tm, 128), f32)]
    return pl.pallas_call(
        _combine_kernel,
        grid=(n // tm,),
        in_specs=o_specs + l_specs + [pl.BlockSpec((128, QD), lambda i: (0, 0))],
        out_specs=pl.BlockSpec((tm, QD), row),
        out_shape=jax.ShapeDtypeStruct((n, QD), bf16),
        scratch_shapes=scratch,
        compiler_params=_cparams("arbitrary"),
        name="branch_combine",
    )(*os, *ls, jnp.asarray(_head_expand_np(), bf16))


SAMPLE_HEADS = 4


def _sample_bias(rel_bias, length):
    dist = length - np.arange(length)
    tbl = _bias_lookup(rel_bias, _rel_bucket_np(dist)).T
    rows = []
    for (_, dil) in DIL_CONFIGS:
        valid = (dist % dil == 0) & (dist <= BLK * dil)
        rows.append(jnp.where(jnp.asarray(valid)[None, :], tbl, NEG))
    b0 = _bias_lookup(rel_bias, _rel_bucket_np(np.zeros((1,), np.int64)))
    groups = H_D // SAMPLE_HEADS
    return (jnp.stack(rows).reshape(len(DIL_CONFIGS), groups, SAMPLE_HEADS, length),
            jnp.broadcast_to(b0.reshape(groups, SAMPLE_HEADS, 1), (groups, SAMPLE_HEADS, 128)))


def _dil_sample_kernel(q_ref, kn_ref, vn_ref, kt_ref, vt_ref, bias_ref, b0_ref, o_ref):
    def row8(x):
        return jnp.broadcast_to(x, (8, x.shape[1])).astype(bf16)

    for h in range(SAMPLE_HEADS):
        q = q_ref[h:h + 1, :] * (DH_D ** -0.5)
        kn, vn = kn_ref[h:h + 1, :], vn_ref[h:h + 1, :]
        lg = jnp.dot(row8(q), kt_ref[h].astype(bf16), preferred_element_type=f32)[0:1]
        l0 = jnp.sum(q * kn, axis=-1, keepdims=True) + b0_ref[h:h + 1, 0:1]
        ps, p0s, dens, lses = [], [], [], []
        for c in range(len(DIL_CONFIGS)):
            s = lg + bias_ref[c, h:h + 1, :]
            m = jnp.maximum(jnp.max(s, axis=-1, keepdims=True), l0)
            p = jnp.exp(s - m)
            p0 = jnp.exp(l0 - m)
            den = jnp.sum(p, axis=-1, keepdims=True) + p0
            ps.append(p), p0s.append(p0), dens.append(den), lses.append(m + jnp.log(den))
        m = jnp.maximum(jnp.maximum(lses[0], lses[1]), lses[2])
        es = [jnp.exp(l - m) for l in lses]
        z = es[0] + es[1] + es[2]
        coef = [e / z / den for e, den in zip(es, dens)]
        pmix = coef[0] * ps[0] + coef[1] * ps[1] + coef[2] * ps[2]
        p0mix = coef[0] * p0s[0] + coef[1] * p0s[1] + coef[2] * p0s[2]
        o = lax.dot_general(row8(pmix), vt_ref[h].astype(bf16), NT, preferred_element_type=f32)[0:1]
        o_ref[h:h + 1, :] = o + p0mix * vn


def _dil_sample_call(q, kn, vn, cache_k, cache_v, layer, bias, b0):
    nb = q.shape[0]
    depth, _, length = cache_k.shape[:3]
    groups = H_D // SAMPLE_HEADS
    hd = pl.BlockSpec((None, None, SAMPLE_HEADS, DH_D), lambda b, g: (b, g, 0, 0))
    cache_spec = pl.BlockSpec((None, None, None, SAMPLE_HEADS, DH_D, length), lambda b, g: (layer, b, g, 0, 0, 0))
    view = lambda c: jnp.transpose(c, (0, 1, 3, 4, 2)).reshape(depth, nb, groups, SAMPLE_HEADS, DH_D, length)
    to_hd = lambda t: t.reshape(nb, groups, SAMPLE_HEADS, DH_D)
    out = pl.pallas_call(
        _dil_sample_kernel,
        grid=(nb, groups),
        in_specs=[hd, hd, hd, cache_spec, cache_spec,
                  pl.BlockSpec((len(DIL_CONFIGS), None, SAMPLE_HEADS, length), lambda b, g: (0, g, 0, 0)),
                  pl.BlockSpec((None, SAMPLE_HEADS, 128), lambda b, g: (g, 0, 0))],
        out_specs=hd,
        out_shape=jax.ShapeDtypeStruct((nb, groups, SAMPLE_HEADS, DH_D), f32),
        compiler_params=_cparams("arbitrary", "arbitrary"),
        name="dilated_sample",
    )(to_hd(q), to_hd(kn), to_hd(vn), view(cache_k), view(cache_v), bias, b0)
    return out.reshape(nb, QD)


def _route(logits_t, rb):
    per_group = N_EXPERTS // N_EXPERT_GROUPS
    sc = jax.nn.sigmoid(logits_t)
    sel = sc + rb
    sel_r = [sel[e:e + 1, :] for e in range(N_EXPERTS)]
    sc_r = [sc[e:e + 1, :] for e in range(N_EXPERTS)]

    def beats(j, i, vals):
        return (vals[j] >= vals[i]) if j < i else (vals[j] > vals[i])

    gs = []
    for g in range(N_EXPERT_GROUPS):
        a, b, c, d = sel_r[per_group * g:per_group * (g + 1)]
        m1, n1, m2, n2 = jnp.maximum(a, b), jnp.minimum(a, b), jnp.maximum(c, d), jnp.minimum(c, d)
        gs.append(jnp.maximum(m1, m2) + jnp.maximum(jnp.minimum(m1, m2), jnp.maximum(n1, n2)))
    nums = []
    for g in range(N_EXPERT_GROUPS):
        grank = sum(beats(j, g, gs).astype(f32) for j in range(N_EXPERT_GROUPS) if j != g)
        vals = sel_r[per_group * g:per_group * (g + 1)]
        for i in range(per_group):
            rank = sum(beats(j, i, vals).astype(f32) for j in range(per_group) if j != i)
            pick = (grank == 0.0) & (rank < 2.0)
            nums.append(jnp.where(pick, sc_r[per_group * g + i], 0.0))
    den = sum(nums)
    return jnp.concatenate(nums, axis=0) / den


def _wo_kernel(x_ref, oa_ref, md_ref, wo_ref, g1_ref, ng_ref, sc_ref, sh_ref, rwt_ref, rb_ref,
               x1_ref, h_ref, gate_ref):
    y = jnp.dot(oa_ref[...].astype(bf16), wo_ref[0:VA, :], preferred_element_type=f32)
    y = y + jnp.dot(md_ref[...].astype(bf16), wo_ref[VA:, :], preferred_element_type=f32)
    x1 = x_ref[...] + g1_ref[...] * y
    x1_ref[...] = x1
    h = _rmsnorm(x1, ng_ref[...]) * (1.0 + sc_ref[...]) + sh_ref[...]
    h_ref[...] = h.astype(bf16)
    logits_t = lax.dot_general(rwt_ref[...], h, NT, precision=HIGHEST, preferred_element_type=f32)
    gate_ref[...] = _route(logits_t, rb_ref[...])


def _wo_call(x2d, oa, md, wo, g1, ng, sc, sh, rwt, rb, tm, tiles_per_batch):
    n, d = x2d.shape
    row = lambda i: (i, 0)
    const = lambda shape: pl.BlockSpec(shape, lambda i: (0,) * len(shape))
    return pl.pallas_call(
        _wo_kernel,
        grid=(n // tm,),
        in_specs=[
            pl.BlockSpec((tm, d), row),
            pl.BlockSpec((tm, VA), row),
            pl.BlockSpec((tm, QD), row),
            _resident((d, d), lambda i: (0, 0)),
            _mod_spec(g1, tiles_per_batch),
            const((1, d)),
            _mod_spec(sc, tiles_per_batch),
            _mod_spec(sh, tiles_per_batch),
            const((N_EXPERTS, d)),
            const((N_EXPERTS, 1)),
        ],
        out_specs=[pl.BlockSpec((tm, d), row), pl.BlockSpec((tm, d), row), pl.BlockSpec((N_EXPERTS, tm), lambda i: (0, i))],
        out_shape=[jax.ShapeDtypeStruct((n, d), f32), jax.ShapeDtypeStruct((n, d), bf16),
                   jax.ShapeDtypeStruct((N_EXPERTS, n), f32)],
        compiler_params=_cparams("arbitrary"),
        name="wo_router",
    )(x2d, oa, md, wo, g1, ng, sc, sh, rwt, rb)


def _moe_dense_kernel(final, h_ref, gate_ref, wg_ref, wu_ref, wd_ref, x1_ref, g2_ref, fg_ref, out_ref, acc_ref):
    e = pl.program_id(1)

    @pl.when(e == 0)
    def _():
        acc_ref[...] = jnp.zeros_like(acc_ref)

    h = h_ref[...]
    hid = _silu(jnp.dot(h, wg_ref[...], preferred_element_type=f32)) * jnp.dot(h, wu_ref[...], preferred_element_type=f32)
    lane = lax.broadcasted_iota(jnp.int32, (1, N_EXPERTS), 1)
    ge = jnp.sum(jnp.where(lane == e, gate_ref[...], 0.0), axis=-1, keepdims=True)
    acc_ref[...] += jnp.dot((hid * ge).astype(bf16), wd_ref[...], preferred_element_type=f32)

    @pl.when(e == pl.num_programs(1) - 1)
    def _():
        x2 = x1_ref[...] + g2_ref[...] * acc_ref[...]
        out_ref[...] = _rmsnorm(x2, fg_ref[...]) if final else x2


def _moe_dense_call(h, gate, wg, wu, wd, x1, g2, fg, final, tm, tiles_per_batch):
    n, d = x1.shape
    ne, _, ff = wg.shape
    row = lambda i, e: (i, 0)
    return pl.pallas_call(
        functools.partial(_moe_dense_kernel, final),
        grid=(n // tm, ne),
        in_specs=[
            pl.BlockSpec((tm, d), row),
            pl.BlockSpec((tm, N_EXPERTS), row),
            pl.BlockSpec((None, d, ff), lambda i, e: (e, 0, 0)),
            pl.BlockSpec((None, d, ff), lambda i, e: (e, 0, 0)),
            pl.BlockSpec((None, ff, d), lambda i, e: (e, 0, 0)),
            pl.BlockSpec((tm, d), row),
            _mod_spec(g2, tiles_per_batch),
            pl.BlockSpec((1, d), lambda i, e: (0, 0)),
        ],
        out_specs=pl.BlockSpec((tm, d), row),
        out_shape=jax.ShapeDtypeStruct((n, d), f32),
        scratch_shapes=[pltpu.VMEM((tm, d), f32)],
        compiler_params=_cparams("arbitrary", "arbitrary"),
        name="moe_dense",
    )(h, gate, wg, wu, wd, x1, g2, fg)


MOE_TILE = 1024
MOE_ALIGN = 16
MOE_WINDOW = 144
MOE_ROWS = 2560
MOE_SELECT_ROWS = 512
MOE_COMBINE_ROWS = 256


def _plan_kernel(gate_ref, slot_ref, off_ref, cnt_ref):
    t = gate_ref.shape[1]
    gate = gate_ref[...]
    sel = jnp.where(gate > 0.0, 1.0, 0.0)
    before = lax.broadcasted_iota(jnp.int32, (t, t), 0) < lax.broadcasted_iota(jnp.int32, (t, t), 1)
    rank = jnp.dot(sel.astype(bf16), jnp.where(before, 1.0, 0.0).astype(bf16), preferred_element_type=f32)
    cnt = jnp.sum(sel, axis=1, keepdims=True)
    cpad = jnp.floor((cnt + (MOE_ALIGN - 1)) * (1.0 / MOE_ALIGN)) * MOE_ALIGN
    offs, run = [], jnp.zeros((1, 1), f32)
    for e in range(N_EXPERTS):
        offs.append(run)
        run = run + cpad[e:e + 1]
    off = jnp.concatenate(offs, axis=0)
    dest = off + rank
    seen = jnp.zeros((1, t), f32)
    dest_a = dest_b = w_a = w_b = jnp.zeros((1, t), f32)
    n_a = n_b = jnp.zeros((1, t), f32)
    for e in range(N_EXPERTS):
        s_e = sel[e:e + 1]
        is_a = s_e * jnp.where(seen == 0.0, 1.0, 0.0)
        is_b = s_e * jnp.where(seen == 1.0, 1.0, 0.0)
        dest_a, w_a, n_a = dest_a + is_a * dest[e:e + 1], w_a + is_a * gate[e:e + 1], n_a + is_a
        dest_b, w_b, n_b = dest_b + is_b * dest[e:e + 1], w_b + is_b * gate[e:e + 1], n_b + is_b
        seen = seen + s_e
    dest_a = jnp.where(n_a > 0.0, dest_a, -1.0)
    dest_b = jnp.where(n_b > 0.0, dest_b, -1.0)
    slot_ref[...] = jnp.concatenate([dest_a, dest_b, w_a, w_b, jnp.zeros((4, t), f32)], axis=0)
    off_ref[...] = jnp.broadcast_to(off, (N_EXPERTS, 128))
    cnt_ref[...] = jnp.broadcast_to(cnt, (N_EXPERTS, 128))


def _plan_call(gate_t):
    n = gate_t.shape[1]
    nt = n // MOE_TILE
    slots, off, cnt = pl.pallas_call(
        _plan_kernel,
        grid=(nt,),
        in_specs=[pl.BlockSpec((N_EXPERTS, MOE_TILE), lambda i: (0, i))],
        out_specs=[pl.BlockSpec((8, MOE_TILE), lambda i: (0, i)),
                   pl.BlockSpec((None, N_EXPERTS, 128), lambda i: (i, 0, 0)),
                   pl.BlockSpec((None, N_EXPERTS, 128), lambda i: (i, 0, 0))],
        out_shape=[jax.ShapeDtypeStruct((8, n), f32), jax.ShapeDtypeStruct((nt, N_EXPERTS, 128), f32),
                   jax.ShapeDtypeStruct((nt, N_EXPERTS, 128), f32)],
        compiler_params=_cparams("arbitrary"),
        name="moe_plan",
    )(gate_t)
    to_smem = lambda a: a[:, :, 0].astype(jnp.int32).reshape(nt * N_EXPERTS)
    return slots, to_smem(off), to_smem(cnt)


def _moe_sparse_kernel(final, off_ref, cnt_ref, h_ref, srow_ref, scol_ref, wg_ref, wu_ref, wd_ref,
                       x1_ref, g2_ref, fg_ref, out_ref, xs_ref, ys_ref):
    i, e = pl.program_id(0), pl.program_id(1)
    t = h_ref.shape[0]

    @pl.when(e == 0)
    def _():
        dest_a, dest_b = srow_ref[0:1, :], srow_ref[1:2, :]
        for c in range(MOE_ROWS // MOE_SELECT_ROWS):
            r = (lax.broadcasted_iota(jnp.int32, (MOE_SELECT_ROWS, t), 0) + c * MOE_SELECT_ROWS).astype(f32)
            sel = (jnp.where(r == dest_a, 1.0, 0.0) + jnp.where(r == dest_b, 1.0, 0.0)).astype(bf16)
            rows = slice(c * MOE_SELECT_ROWS, (c + 1) * MOE_SELECT_ROWS)
            xs_ref[rows, :] = jnp.dot(sel, h_ref[...], preferred_element_type=f32).astype(bf16)
        ys_ref[...] = jnp.zeros_like(ys_ref)

    off = off_ref[i * N_EXPERTS + e]
    cnt = cnt_ref[i * N_EXPERTS + e]

    def window(j, carry):
        start = pl.multiple_of(off + j * MOE_WINDOW, MOE_ALIGN)
        xw = xs_ref[pl.ds(start, MOE_WINDOW), :]
        hid = _silu(jnp.dot(xw, wg_ref[...], preferred_element_type=f32)) * jnp.dot(xw, wu_ref[...], preferred_element_type=f32)
        ys_ref[pl.ds(start, MOE_WINDOW), :] = jnp.dot(hid.astype(bf16), wd_ref[...], preferred_element_type=f32).astype(bf16)
        return carry

    lax.fori_loop(0, (cnt + MOE_WINDOW - 1) // MOE_WINDOW, window, 0)

    @pl.when(e == pl.num_programs(1) - 1)
    def _():
        for c in range(t // MOE_COMBINE_ROWS):
            rows = slice(c * MOE_COMBINE_ROWS, (c + 1) * MOE_COMBINE_ROWS)
            sc = scol_ref[rows, :]
            r = lax.broadcasted_iota(jnp.int32, (MOE_COMBINE_ROWS, MOE_ROWS), 1).astype(f32)
            wsel = jnp.where(r == sc[:, 0:1], sc[:, 2:3], 0.0) + jnp.where(r == sc[:, 1:2], sc[:, 3:4], 0.0)
            y = jnp.dot(wsel.astype(bf16), ys_ref[...], preferred_element_type=f32)
            x2 = x1_ref[rows, :] + g2_ref[...] * y
            out_ref[rows, :] = _rmsnorm(x2, fg_ref[...]) if final else x2


def _moe_sparse_call(h, gate_t, wg, wu, wd, x1, g2, fg, final, tiles_per_batch):
    n, d = x1.shape
    ne, _, ff = wg.shape
    t = MOE_TILE
    assert MOE_ROWS >= 2 * t + (N_EXPERTS - 1) * (MOE_ALIGN - 1) + MOE_WINDOW and MOE_ROWS % MOE_SELECT_ROWS == 0
    slots, off, cnt = _plan_call(gate_t)
    row = lambda i, e, *_: (i, 0)
    grid_spec = pltpu.PrefetchScalarGridSpec(
        num_scalar_prefetch=2,
        grid=(n // t, ne),
        in_specs=[
            pl.BlockSpec((t, d), row),
            pl.BlockSpec((8, t), lambda i, e, *_: (0, i)),
            pl.BlockSpec((t, 8), row),
            pl.BlockSpec((None, d, ff), lambda i, e, *_: (e, 0, 0)),
            pl.BlockSpec((None, d, ff), lambda i, e, *_: (e, 0, 0)),
            pl.BlockSpec((None, ff, d), lambda i, e, *_: (e, 0, 0)),
            pl.BlockSpec((t, d), row),
            _mod_spec(g2, tiles_per_batch),
            pl.BlockSpec((1, d), lambda i, e, *_: (0, 0)),
        ],
        out_specs=pl.BlockSpec((t, d), row),
        scratch_shapes=[pltpu.VMEM((MOE_ROWS, d), bf16), pltpu.VMEM((MOE_ROWS, d), bf16)],
    )
    return pl.pallas_call(
        functools.partial(_moe_sparse_kernel, final),
        grid_spec=grid_spec,
        out_shape=jax.ShapeDtypeStruct((n, d), f32),
        compiler_params=_cparams("arbitrary", "arbitrary"),
        name="moe_sparse",
    )(off, cnt, h, slots, slots.T, wg, wu, wd, x1, g2, fg)


def _reorder_w_in(w_in):
    o = np.cumsum((0, QA, QA, VA, GATE_RANK, VA, QD, QD, QD))
    pad = jnp.zeros(w_in.shape[:-1] + (128 - GATE_RANK,), w_in.dtype)
    parts = [w_in[..., o[0]:o[3]], w_in[..., o[4]:o[5]], w_in[..., o[3]:o[4]], pad, w_in[..., o[5]:o[8]]]
    return jnp.concatenate(parts, axis=-1).astype(bf16)


def _trunk(x, mods, prompt, gla_state, cache_k, cache_v, p):
    nb, s, d = x.shape
    n = nb * s
    depth = p["w_in"].shape[0]
    sh1, sc1, g1, sh2, sc2, g2 = mods
    if prompt:
        tm = 512
        tpb = s // tm
        keep = min(2048, s)
        keep_tiles = keep // tm
    else:
        tm, tpb, keep_tiles = n, 0, 0
    x2d = x.reshape(n, d)
    new_s, new_k, new_v = [], [], []
    for l in range(depth):
        outs = _inproj_call(x2d, sc1[l], sh1[l], p["norm1_g"][l], p["w_in"][l], tm, tpb, keep_tiles)
        a, q, k, v = outs[:4]
        if prompt:
            oa, st = _gla_prompt_call(a, p["wg"][l], p["bg"][l], p["gn"][l], nb, tm)
            new_s.append(_state_from_blockdiag(st))
            new_k.append(outs[4][:, 1:].reshape(nb, keep, H_D, DH_D))
            new_v.append(outs[5][:, 1:].reshape(nb, keep, H_D, DH_D))
            qkv = {1: (q, k, v)}
            for j, dil in enumerate(STRIDED_DILS):
                qkv[dil] = outs[6 + 3 * j:9 + 3 * j]
            os, ls = zip(*[_dil_prompt_call(*qkv[dil], p["bias_p"][c], nb, dil) for c, (_, dil) in enumerate(DIL_CONFIGS)])
            md = _combine_call(os, ls, tm)
        else:
            oa, st = _gla_sample_call(a, p["wg"][l], p["bg"][l], p["gn"][l], gla_state[l])
            oa = oa.reshape(n, VA)
            new_s.append(st.reshape(nb, H_A, DK_A, DV_A))
            new_k.append(k.reshape(nb, s, H_D, DH_D))
            new_v.append(v.reshape(nb, s, H_D, DH_D))
            md = _dil_sample_call(q, k, v, cache_k, cache_v, l, *p["bias_s"])
        x1, h2, gate_t = _wo_call(x2d, oa, md, p["w_o"][l], g1[l], p["norm2_g"][l], sc2[l], sh2[l],
                                  p["rwt"], p["rb"], tm, tpb)
        experts = (p["moe_wg"][l], p["moe_wu"][l], p["moe_wd"][l])
        if prompt:
            x2d = _moe_sparse_call(h2, gate_t, *experts, x1, g2[l], p["final_g"], l == depth - 1, s // MOE_TILE)
        else:
            x2d = _moe_dense_call(h2, gate_t.T, *experts, x1, g2[l], p["final_g"], l == depth - 1, tm, tpb)
    return x2d.reshape(nb, s, d), jnp.stack(new_s), jnp.stack(new_k), jnp.stack(new_v)


def kernel(x_prompt, x_sample, state_gla, cache_win_k, cache_win_v, c_prompt, c_sample, w_ada, b_ada, norm1_g, norm2_g, w_in, w_gate_up, b_gate, gla_norm_g, w_o, rel_bias, router_w, router_b, moe_w_gate, moe_w_up, moe_w_down, final_norm_g):
    nbp, seq, d = x_prompt.shape
    nbs, dec_seq, _ = x_sample.shape
    depth = w_in.shape[0]
    assert dec_seq == 1 and cache_win_k.shape[2] == DIL_CONFIGS[-1][0], "sample path: one new token over a full window"
    assert seq % DIL_CONFIGS[-1][0] == 0, "prompt length must be a whole number of the widest window"

    c_all = jnp.concatenate([c_prompt, c_sample], axis=0)
    c_all = jnp.pad(c_all, ((0, -c_all.shape[0] % 8), (0, 0)))
    mod = _ada_call(c_all, w_ada, b_ada)
    mods_p = [m[:, :nbp, None, :] for m in jnp.split(mod, 6, axis=-1)]
    mods_s = [m[:, None, nbp:nbp + nbs, :] for m in jnp.split(mod, 6, axis=-1)]

    p = {
        "w_in": _reorder_w_in(w_in),
        "norm1_g": norm1_g.reshape(depth, 1, d),
        "norm2_g": norm2_g.reshape(depth, 1, d),
        "wg": jnp.pad(w_gate_up, ((0, 0), (0, 128 - GATE_RANK), (0, 0))),
        "bg": b_gate.reshape(depth, 1, QA),
        "gn": gla_norm_g.reshape(depth, 1, VA),
        "w_o": w_o.astype(bf16),
        "rwt": router_w.T,
        "rb": router_b.reshape(N_EXPERTS, 1),
        "moe_wg": moe_w_gate.astype(bf16),
        "moe_wu": moe_w_up.astype(bf16),
        "moe_wd": moe_w_down.astype(bf16),
        "final_g": final_norm_g.reshape(1, d),
        "bias_p": [_prompt_bias(rel_bias, dil) for (_, dil) in DIL_CONFIGS],
        "bias_s": _sample_bias(rel_bias, cache_win_k.shape[2]),
    }
    y_p, s_p, k_p, v_p = _trunk(x_prompt, mods_p, True, None, None, None, p)
    y_s, s_s, k_s, v_s = _trunk(x_sample, mods_s, False, state_gla, cache_win_k, cache_win_v, p)
    return (y_p, y_s, s_p, k_p, v_p, s_s, k_s, v_s)
```

```python
import functools
import math

import jax
import jax.numpy as jnp
import numpy as np
from jax import lax
from jax.experimental import pallas as pl
from jax.experimental.pallas import tpu as pltpu

f32 = jnp.float32
bf16 = jnp.bfloat16

H_A, DK_A, DV_A = 4, 32, 64
GATE_RANK = 16
GATE_TAU = 16.0
H_D, DH_D = 12, 64
DIL_CONFIGS = ((128, 1), (512, 4), (2048, 16))
N_BUCKETS = 32
MAX_DISTANCE = 2048
N_EXPERTS = 16
N_EXPERT_GROUPS = 4
EPS = 1e-6

QA = H_A * DK_A
VA = H_A * DV_A
QD = H_D * DH_D
A_WIDTH = 2 * QA + 2 * VA + 128
W_IN_COLS = A_WIDTH + 3 * QD
BLK = 128
N_PAIRS = H_D // 2
NEG = -1e30
GLA_SUB = 16
GLA_ROWS = 128
EXP_CLAMP = 80.0
VMEM_LIMIT = 48 * 1024 * 1024

HIGHEST = lax.Precision.HIGHEST
NT = (((1,), (1,)), ((), ()))
TN = (((0,), (0,)), ((), ()))


def _cparams(*sem):
    return pltpu.CompilerParams(dimension_semantics=sem, vmem_limit_bytes=VMEM_LIMIT)


def _resident(shape, index_map):
    return pl.BlockSpec(shape, index_map, pipeline_mode=pl.Buffered(1))


def _silu(x):
    return x * jax.nn.sigmoid(x)


def _split_bf16(x, terms):
    parts = []
    for _ in range(terms):
        p = x.astype(bf16)
        parts.append(p)
        x = x - p.astype(f32)
    return parts


def _rmsnorm(x, g):
    return x * lax.rsqrt(jnp.mean(x * x, axis=-1, keepdims=True) + EPS) * g


def _ada_kernel(c_ref, w_ref, b_ref, o_ref):
    cs = _silu(c_ref[...]).astype(bf16)
    o_ref[...] = jnp.dot(cs, w_ref[...].astype(bf16), preferred_element_type=f32) + b_ref[...]


def _ada_call(c, w_ada, b_ada):
    depth, d, n6 = w_ada.shape
    m = c.shape[0]
    tn = 1536
    return pl.pallas_call(
        _ada_kernel,
        grid=(depth, n6 // tn),
        in_specs=[
            pl.BlockSpec((m, d), lambda l, j: (0, 0)),
            pl.BlockSpec((None, d, tn), lambda l, j: (l, 0, j)),
            pl.BlockSpec((None, 1, tn), lambda l, j: (l, 0, j)),
        ],
        out_specs=pl.BlockSpec((None, m, tn), lambda l, j: (l, 0, j)),
        out_shape=jax.ShapeDtypeStruct((depth, m, n6), f32),
        compiler_params=_cparams("arbitrary", "arbitrary"),
        name="adaln",
    )(c, w_ada, b_ada.reshape(depth, 1, n6))


def _inproj_kernel(prompt, resid, x_ref, *refs):
    x = x_ref[...]
    if resid:
        y_ref, g2_ref, x2_ref = refs[0], refs[1], refs[6]
        x = x + g2_ref[...] * y_ref[...].astype(f32)
        x2_ref[...] = x
        refs = refs[2:6] + refs[7:]
    sc_ref, sh_ref, g_ref, w_ref = refs[:4]
    outs = refs[4:]
    h = _rmsnorm(x, g_ref[...])
    h = (h * (1.0 + sc_ref[...]) + sh_ref[...]).astype(bf16)

    def proj(lo, hi):
        return jnp.dot(h, w_ref[:, lo:hi], preferred_element_type=f32)

    a_ref, q_ref, k_ref, v_ref = outs[:4]
    a_ref[...] = proj(0, A_WIDTH)
    q = proj(A_WIDTH, A_WIDTH + QD)
    k = proj(A_WIDTH + QD, A_WIDTH + 2 * QD)
    v = proj(A_WIDTH + 2 * QD, A_WIDTH + 3 * QD)
    if prompt:
        n_cls = len(STRIDED_DILS)
        kw_ref, vw_ref = outs[4:6]
        cls_refs = outs[6:6 + 3 * n_cls]
        scratch = outs[6 + 3 * n_cls:]
        q = q * (DH_D ** -0.5)
        kw_ref[...] = k
        vw_ref[...] = v
        for j, (val, nat_ref) in enumerate(((q, q_ref), (k, k_ref), (v, v_ref))):
            nat_ref[...] = val.astype(bf16)
            _to_class_major(val, scratch[j], [cls_refs[3 * c + j] for c in range(n_cls)])
    else:
        q_ref[...] = q
        k_ref[...] = k
        v_ref[...] = v


STRIDED_DILS = tuple(dil for (_, dil) in DIL_CONFIGS if dil > 1)
LANE_TILES = QD // 128


def _to_class_major(val, planes, dst_refs):
    rows = val.shape[0]
    for c in range(LANE_TILES):
        planes[c] = val[:, c * 128:(c + 1) * 128]
    for dil, dst in zip(STRIDED_DILS, dst_refs):
        for c in range(LANE_TILES):
            for r in range(dil):
                lo = r * QD + c * 128
                dst[:, lo:lo + 128] = planes[c, pl.ds(r, rows // dil, stride=dil), :].astype(dst.dtype)


def _from_class_major(src_ref, planes, dil, width):
    rows = planes.shape[1]
    for c in range(width // 128):
        for r in range(dil):
            lo = r * width + c * 128
            planes[c, pl.ds(r, rows // dil, stride=dil), :] = src_ref[:, lo:lo + 128].astype(f32)


def _mod_spec(arr, tiles_per_batch):
    rows, d = arr.shape[1], arr.shape[2]
    if tiles_per_batch:
        return pl.BlockSpec((None, rows, d), lambda i, *_: (i // tiles_per_batch, 0, 0))
    return pl.BlockSpec((None, rows, d), lambda i, *_: (0, 0, 0))


def _inproj_call(x2d, sc, sh, g, w, tm, tiles_per_batch, keep_tiles, resid=None):
    n, d = x2d.shape
    prompt = bool(tiles_per_batch)
    qkv_dt = bf16 if prompt else f32
    row = lambda i: (i, 0)
    out_specs = [pl.BlockSpec((tm, A_WIDTH), row)] + [pl.BlockSpec((tm, QD), row)] * 3
    out_shape = [jax.ShapeDtypeStruct((n, A_WIDTH), f32)] + [jax.ShapeDtypeStruct((n, QD), qkv_dt)] * 3
    scratch = []
    if prompt:
        nb = n // (tm * tiles_per_batch)
        first_keep = tiles_per_batch - keep_tiles

        def win(i):
            return (i // tiles_per_batch, jnp.maximum(i % tiles_per_batch - first_keep + 1, 0), 0, 0)

        out_specs += [pl.BlockSpec((None, None, tm, QD), win)] * 2
        out_shape += [jax.ShapeDtypeStruct((nb, keep_tiles + 1, tm, QD), f32)] * 2
        for dil in STRIDED_DILS:
            out_specs += [pl.BlockSpec((tm // dil, dil * QD), row)] * 3
            out_shape += [jax.ShapeDtypeStruct((n // dil, dil * QD), bf16)] * 3
        scratch = [pltpu.VMEM((LANE_TILES, tm, 128), f32)] * 3
    in_specs = [
        _mod_spec(sc, tiles_per_batch),
        _mod_spec(sh, tiles_per_batch),
        pl.BlockSpec((1, d), lambda i: (0, 0)),
        _resident((d, W_IN_COLS), lambda i: (0, 0)),
    ]
    args = (sc, sh, g, w)
    if resid is not None:
        in_specs = [pl.BlockSpec((tm, d), row), _mod_spec(resid[1], tiles_per_batch)] + in_specs
        args = resid + args
        out_specs = [pl.BlockSpec((tm, d), row)] + out_specs
        out_shape = [jax.ShapeDtypeStruct((n, d), f32)] + out_shape
    return pl.pallas_call(
        functools.partial(_inproj_kernel, prompt, resid is not None),
        grid=(n // tm,),
        in_specs=[pl.BlockSpec((tm, d), row)] + in_specs,
        out_specs=out_specs,
        out_shape=out_shape,
        scratch_shapes=scratch,
        compiler_params=_cparams("arbitrary"),
        name="inproj_prompt" if prompt else "inproj_sample",
    )(x2d, *args)


def _log_gate(glr, wg_ref, bg_ref):
    x = jnp.dot(glr.astype(bf16), wg_ref[...].astype(bf16), preferred_element_type=f32) + bg_ref[...]
    return (jnp.minimum(x, 0.0) - jnp.log1p(jnp.exp(-jnp.abs(x)))) * (1.0 / GATE_TAU)


def _gla_prompt_kernel(a_ref, wg_ref, bg_ref, gn_ref, o_ref, s_ref, st_ref):
    R, C = GLA_ROWS, GLA_SUB

    @pl.when(pl.program_id(1) == 0)
    def _():
        st_ref[...] = jnp.zeros_like(st_ref)

    def iota(shape, dim, shift=0):
        return lax.shift_right_logical(lax.broadcasted_iota(jnp.int32, shape, dim), shift)

    sub, lk, lv = int(math.log2(C)), int(math.log2(DK_A)), int(math.log2(DV_A))
    ri = lax.bitwise_and(iota((H_A * R, R), 0), R - 1)
    ci = iota((H_A * R, R), 1)
    same4 = lax.shift_right_logical(ri, sub) == lax.shift_right_logical(ci, sub)
    causal4 = same4 & (ci <= ri)
    sums = jnp.concatenate([jnp.where(causal4, 1.0, 0.0)[0:R], jnp.where(same4, 1.0, 0.0)[0:R]], axis=0).astype(bf16)
    lane_q = iota((1, QA), 1, lk)
    lane_v = iota((1, VA), 1, lv)
    bd = iota((VA, QA), 0, lv) == iota((VA, QA), 1, lk)
    hmean = jnp.where(iota((VA, VA), 0, lv) == iota((VA, VA), 1, lv), 1.0 / DV_A, 0.0).astype(bf16)

    def block(c, st):
        rows = slice(c * R, (c + 1) * R)
        blk = a_ref[rows, :]
        q = blk[:, 0:QA] * (DK_A ** -0.5)
        k = blk[:, QA:2 * QA]
        v = blk[:, 2 * QA:2 * QA + VA]
        r = blk[:, 2 * QA + VA:2 * QA + 2 * VA]
        g = _log_gate(blk[:, 2 * QA + 2 * VA:], wg_ref, bg_ref)
        cums = sum(jnp.dot(sums, part, preferred_element_type=f32) for part in _split_bf16(g, 3))
        b, bt = cums[0:R], cums[R:2 * R]
        qb = q * jnp.exp(b)
        kd = (k * jnp.exp(bt - b)).astype(bf16)
        ki = (k * jnp.exp(jnp.minimum(-b, EXP_CLAMP))).astype(bf16)
        vb = v.astype(bf16)
        qs = jnp.concatenate([jnp.where(lane_q == h, qb, 0.0) for h in range(H_A)], axis=0).astype(bf16)
        att = lax.dot_general(qs, ki, NT, preferred_element_type=f32)
        att = jnp.where(causal4, att, 0.0).astype(bf16)
        res = jnp.dot(att, vb, preferred_element_type=f32)
        o = jnp.where(lane_v == 0, res[0:R], 0.0)
        for h in range(1, H_A):
            o = o + jnp.where(lane_v == h, res[h * R:(h + 1) * R], 0.0)
        qbb = qb.astype(bf16)
        parts = []
        for i in range(R // C):
            sl = slice(i * C, (i + 1) * C)
            parts.append(lax.dot_general(qbb[sl], st.astype(bf16), NT, preferred_element_type=f32))
            kv = lax.dot_general(vb[sl], kd[sl], TN, preferred_element_type=f32)
            st = st * jnp.exp(bt[i * C:i * C + 1, :]) + jnp.where(bd, kv, 0.0)
        o = o + jnp.concatenate(parts, axis=0)
        ms = sum(jnp.dot(part, hmean, preferred_element_type=f32) for part in _split_bf16(o * o, 2))
        y = o * lax.rsqrt(ms + EPS) * gn_ref[...]
        o_ref[rows, :] = (y * _silu(r)).astype(o_ref.dtype)
        return st

    st = st_ref[...]
    for c in range(a_ref.shape[0] // R):
        st = block(c, st)
    st_ref[...] = st
    s_ref[...] = st


def _gla_prompt_call(a, wg, bg, gn, nb, tm):
    n = a.shape[0]
    tpb = n // nb // tm
    return pl.pallas_call(
        _gla_prompt_kernel,
        grid=(nb, tpb),
        in_specs=[
            pl.BlockSpec((tm, A_WIDTH), lambda b, j: (b * tpb + j, 0)),
            pl.BlockSpec((128, QA), lambda b, j: (0, 0)),
            pl.BlockSpec((1, QA), lambda b, j: (0, 0)),
            pl.BlockSpec((1, VA), lambda b, j: (0, 0)),
        ],
        out_specs=[
            pl.BlockSpec((tm, VA), lambda b, j: (b * tpb + j, 0)),
            pl.BlockSpec((None, VA, QA), lambda b, j: (b, 0, 0)),
        ],
        out_shape=[jax.ShapeDtypeStruct((n, VA), bf16), jax.ShapeDtypeStruct((nb, VA, QA), f32)],
        scratch_shapes=[pltpu.VMEM((VA, QA), f32)],
        compiler_params=_cparams("arbitrary", "arbitrary"),
        name="gla_prompt",
    )(a, wg, bg, gn)


def _state_from_blockdiag(st):
    b = st.shape[0]
    st5 = st.reshape(b, H_A, DV_A, H_A, DK_A)
    return jnp.stack([jnp.swapaxes(st5[:, h, :, h, :], -1, -2) for h in range(H_A)], axis=1)


def _gla_sample_kernel(a_ref, at_ref, wgt_ref, bgt_ref, gn_ref, s_ref, o_ref, so_ref):
    nb = a_ref.shape[0]
    qt = at_ref[0:QA, :] * (DK_A ** -0.5)
    kt = at_ref[QA:2 * QA, :]
    xt = jnp.dot(wgt_ref[...].astype(bf16), at_ref[2 * QA + 2 * VA:, :].astype(bf16), preferred_element_type=f32) + bgt_ref[...]
    et = jnp.exp((jnp.minimum(xt, 0.0) - jnp.log1p(jnp.exp(-jnp.abs(xt)))) * (1.0 / GATE_TAU))
    gn = gn_ref[...]
    for t in range(nb):
        row = a_ref[t:t + 1, :]
        vt = jnp.concatenate(
            [jnp.broadcast_to(row[:, 2 * QA + h * DV_A:2 * QA + (h + 1) * DV_A], (DK_A, DV_A)) for h in range(H_A)], axis=0)
        s_new = s_ref[t] * et[:, t:t + 1] + kt[:, t:t + 1] * vt
        so_ref[t] = s_new
        o = jnp.sum((qt[:, t:t + 1] * s_new).reshape(H_A, DK_A, DV_A), axis=1)
        y = _rmsnorm(o, gn)
        r = jnp.concatenate([row[:, 2 * QA + VA + h * DV_A:2 * QA + VA + (h + 1) * DV_A] for h in range(H_A)], axis=0)
        o_ref[t] = y * _silu(r)


def _gla_sample_call(a, wg, bg, gn, state):
    nb = a.shape[0]
    return pl.pallas_call(
        _gla_sample_kernel,
        out_shape=[jax.ShapeDtypeStruct((nb, H_A, DV_A), f32), jax.ShapeDtypeStruct((nb, QA, DV_A), f32)],
        compiler_params=pltpu.CompilerParams(vmem_limit_bytes=VMEM_LIMIT),
        name="gla_sample",
    )(a, a.T, wg.T, bg.reshape(QA, 1), gn.reshape(H_A, DV_A), state.reshape(nb, QA, DV_A))


def _rel_bucket_np(dist):
    max_exact = N_BUCKETS // 2
    d = np.maximum(dist, 0)
    df = np.maximum(d, 1).astype(np.float32)
    large = max_exact + (np.log(df / np.float32(max_exact)) / np.float32(math.log(MAX_DISTANCE / max_exact))
                         * np.float32(N_BUCKETS - max_exact)).astype(np.int32)
    large = np.minimum(large, N_BUCKETS - 1)
    return np.where(d < max_exact, d, large).astype(np.int32)


def _bias_lookup(rel_bias, buckets):
    onehot = np.zeros((buckets.size, N_BUCKETS), np.float32)
    onehot[np.arange(buckets.size), buckets.reshape(-1)] = 1.0
    out = jnp.dot(jnp.asarray(onehot), rel_bias.astype(f32), precision=HIGHEST)
    return out.reshape(buckets.shape + (rel_bias.shape[-1],))


def _prompt_bias(rel_bias, dil):
    i = np.arange(BLK)[:, None]
    j = np.arange(2 * BLK)[None, :]
    delta = BLK + i - j
    band = (delta >= 0) & (delta <= BLK)
    tbl = _bias_lookup(rel_bias, _rel_bucket_np(delta * dil))
    tbl = jnp.where(jnp.asarray(band)[:, :, None], tbl, NEG)
    first = jnp.where(jnp.asarray(j >= BLK)[:, :, None], tbl, NEG)
    pairs = lambda t: jnp.transpose(t, (2, 0, 1)).reshape(N_PAIRS, 2 * BLK, 2 * BLK)
    return jnp.concatenate([pairs(tbl), pairs(first)], axis=0)


def _dil_prompt_kernel(q_ref, kp_ref, kc_ref, vp_ref, vc_ref, bias_ref, o_ref, lse_ref):
    lane = lax.broadcasted_iota(jnp.int32, (1, 2 * DH_D), 1)
    lo = lane < DH_D
    variant = jnp.where(pl.program_id(2) == 0, N_PAIRS, 0)
    lse_acc = jnp.zeros((BLK, 128), f32)
    lane_h = lax.broadcasted_iota(jnp.int32, (1, 128), 1)
    zero = jnp.zeros((), bf16)
    for p in range(N_PAIRS):
        cols = slice(p * 2 * DH_D, (p + 1) * 2 * DH_D)
        qp = q_ref[:, cols]
        k2 = jnp.concatenate([kp_ref[:, cols], kc_ref[:, cols]], axis=0)
        v2 = jnp.concatenate([vp_ref[:, cols], vc_ref[:, cols]], axis=0)
        halves = []
        for hh, qh in enumerate((jnp.where(lo, qp, zero), jnp.where(lo, zero, qp))):
            s = lax.dot_general(qh, k2, NT, preferred_element_type=f32) + bias_ref[variant + p, hh * BLK:(hh + 1) * BLK, :]
            m = jnp.max(s, axis=-1, keepdims=True)
            e = jnp.exp(s - m)
            den = jnp.sum(e, axis=-1, keepdims=True)
            halves.append(jnp.dot(e.astype(bf16), v2, preferred_element_type=f32) / den)
            lse_acc = jnp.where(lane_h == 2 * p + hh, m + jnp.log(den), lse_acc)
        o_ref[:, cols] = jnp.where(lo, halves[0], halves[1]).astype(o_ref.dtype)
    lse_ref[...] = lse_acc


def _dil_prompt_call(q, k, v, bias, nb, dil):
    rows = q.shape[0] // nb
    nblk = rows // BLK
    view = lambda t: t.reshape(nb, rows, t.shape[-1])
    cur = pl.BlockSpec((None, BLK, QD), lambda b, r, i: (b, i, r))
    prev = pl.BlockSpec((None, BLK, QD), lambda b, r, i: (b, jnp.maximum(i - 1, 0), r))
    o, lse = pl.pallas_call(
        _dil_prompt_kernel,
        grid=(nb, dil, nblk),
        in_specs=[cur, prev, cur, prev, cur, _resident((2 * N_PAIRS, 2 * BLK, 2 * BLK), lambda b, r, i: (0, 0, 0))],
        out_specs=[cur, pl.BlockSpec((None, BLK, 128), lambda b, r, i: (b, i, r))],
        out_shape=[jax.ShapeDtypeStruct((nb, rows, dil * QD), bf16), jax.ShapeDtypeStruct((nb, rows, dil * 128), f32)],
        compiler_params=_cparams("arbitrary", "arbitrary", "arbitrary"),
        name=f"dilated_prompt_d{dil}",
    )(view(q), view(k), view(k), view(v), view(v), bias)
    return o.reshape(nb * rows, dil * QD), lse.reshape(nb * rows, dil * 128)


def _head_expand_np():
    e = np.zeros((128, QD), np.float32)
    for h in range(H_D):
        e[h, h * DH_D:(h + 1) * DH_D] = 1.0
    return e


def _expand(w, e_ref):
    hi = w.astype(bf16)
    lo = (w - hi.astype(f32)).astype(bf16)
    return jnp.dot(hi, e_ref[...], preferred_element_type=f32) + jnp.dot(lo, e_ref[...], preferred_element_type=f32)


def _combine_kernel(*refs):
    nbr = len(DIL_CONFIGS)
    o_refs, l_refs = refs[:nbr], refs[nbr:2 * nbr]
    e_ref, out_ref = refs[2 * nbr], refs[2 * nbr + 1]
    scratch = refs[2 * nbr + 2:]
    o_planes, lses, used = [], [], 0
    for c, (_, dil) in enumerate(DIL_CONFIGS):
        if dil == 1:
            o_planes.append(None)
            lses.append(l_refs[c][...])
        else:
            op, lp = scratch[used], scratch[used + 1]
            used += 2
            _from_class_major(o_refs[c], op, dil, QD)
            _from_class_major(l_refs[c], lp, dil, 128)
            o_planes.append(op)
            lses.append(lp[0])
    m = functools.reduce(jnp.maximum, lses)
    es = [jnp.exp(l - m) for l in lses]
    z = functools.reduce(lambda a, b: a + b, es)
    ws = [_expand(e / z, e_ref) for e in es]
    for t in range(LANE_TILES):
        cols = slice(t * 128, (t + 1) * 128)
        acc = None
        for c in range(nbr):
            o = o_refs[c][:, cols].astype(f32) if o_planes[c] is None else o_planes[c][t]
            acc = ws[c][:, cols] * o if acc is None else acc + ws[c][:, cols] * o
        out_ref[:, cols] = acc.astype(out_ref.dtype)


def _combine_call(os, ls, tm):
    n = os[0].shape[0] * DIL_CONFIGS[0][1]
    row = lambda i: (i, 0)
    o_specs = [pl.BlockSpec((tm // dil, dil * QD), row) for (_, dil) in DIL_CONFIGS]
    l_specs = [pl.BlockSpec((tm // dil, dil * 128), row) for (_, dil) in DIL_CONFIGS]
    scratch = []
    for (_, dil) in DIL_CONFIGS:
        if dil > 1:
            scratch += [pltpu.VMEM((LANE_TILES, tm, 128), f32), pltpu.VMEM((1, tm, 128), f32)]
    return pl.pallas_call(
        _combine_kernel,
        grid=(n // tm,),
        in_specs=o_specs + l_specs + [pl.BlockSpec((128, QD), lambda i: (0, 0))],
        out_specs=pl.BlockSpec((tm, QD), row),
        out_shape=jax.ShapeDtypeStruct((n, QD), bf16),
        scratch_shapes=scratch,
        compiler_params=_cparams("arbitrary"),
        name="branch_combine",
    )(*os, *ls, jnp.asarray(_head_expand_np(), bf16))


SAMPLE_HEADS = 4


def _sample_bias(rel_bias, length):
    dist = length - np.arange(length)
    tbl = _bias_lookup(rel_bias, _rel_bucket_np(dist)).T
    rows = []
    for (_, dil) in DIL_CONFIGS:
        valid = (dist % dil == 0) & (dist <= BLK * dil)
        rows.append(jnp.where(jnp.asarray(valid)[None, :], tbl, NEG))
    b0 = _bias_lookup(rel_bias, _rel_bucket_np(np.zeros((1,), np.int64)))
    groups = H_D // SAMPLE_HEADS
    return (jnp.stack(rows).reshape(len(DIL_CONFIGS), groups, SAMPLE_HEADS, length),
            jnp.broadcast_to(b0.reshape(groups, SAMPLE_HEADS, 1), (groups, SAMPLE_HEADS, 128)))


def _dil_sample_kernel(qkv_ref, kt_ref, vt_ref, bias_ref, b0_ref, o_ref):
    nh = SAMPLE_HEADS
    eye = jnp.where(lax.broadcasted_iota(jnp.int32, (DH_D, DH_D), 0) == lax.broadcasted_iota(jnp.int32, (DH_D, DH_D), 1),
                    1.0, 0.0).astype(f32)
    cols = lax.dot_general(eye, qkv_ref[...], NT, precision=HIGHEST, preferred_element_type=f32)
    qc = cols[:, 0:nh] * (DH_D ** -0.5)
    lg = jnp.concatenate([jnp.sum(kt_ref[h] * qc[:, h:h + 1], axis=0, keepdims=True) for h in range(nh)], axis=0)
    l0 = jnp.concatenate([jnp.sum(qc[:, h:h + 1] * cols[:, nh + h:nh + h + 1], axis=0, keepdims=True) for h in range(nh)],
                         axis=0) + b0_ref[:, 0:1]
    ps, p0s, dens, lses = [], [], [], []
    for c in range(len(DIL_CONFIGS)):
        s = lg + bias_ref[c]
        m = jnp.maximum(jnp.max(s, axis=-1, keepdims=True), l0)
        p = jnp.exp(s - m)
        p0 = jnp.exp(l0 - m)
        den = jnp.sum(p, axis=-1, keepdims=True) + p0
        ps.append(p), p0s.append(p0), dens.append(den), lses.append(m + jnp.log(den))
    m = jnp.maximum(jnp.maximum(lses[0], lses[1]), lses[2])
    es = [jnp.exp(l - m) for l in lses]
    z = es[0] + es[1] + es[2]
    coef = [e / z / den for e, den in zip(es, dens)]
    pmix = coef[0] * ps[0] + coef[1] * ps[1] + coef[2] * ps[2]
    p0mix = coef[0] * p0s[0] + coef[1] * p0s[1] + coef[2] * p0s[2]
    lane = lax.broadcasted_iota(jnp.int32, (1, 128), 1)
    ocols = jnp.zeros((DH_D, 128), f32)
    for h in range(nh):
        oc = jnp.sum(vt_ref[h] * pmix[h:h + 1, :], axis=1, keepdims=True)
        oc = oc + p0mix[h:h + 1, 0:1] * cols[:, 2 * nh + h:2 * nh + h + 1]
        ocols = jnp.where(lane == h, oc, ocols)
    o = lax.dot_general(ocols, eye, TN, precision=HIGHEST, preferred_element_type=f32)
    o_ref[...] = o[0:nh]


def _dil_sample_call(q, kn, vn, cache_k, cache_v, layer, bias, b0):
    nb = q.shape[0]
    depth, _, length = cache_k.shape[:3]
    groups = H_D // SAMPLE_HEADS
    hd = pl.BlockSpec((None, None, SAMPLE_HEADS, DH_D), lambda b, g: (b, g, 0, 0))
    cache_spec = pl.BlockSpec((None, None, None, SAMPLE_HEADS, DH_D, length), lambda b, g: (layer, b, g, 0, 0, 0))
    view = lambda c: jnp.transpose(c, (0, 1, 3, 4, 2)).reshape(depth, nb, groups, SAMPLE_HEADS, DH_D, length)
    to_hd = lambda t: t.reshape(nb, groups, SAMPLE_HEADS, DH_D)
    qkv = jnp.concatenate([to_hd(q), to_hd(kn), to_hd(vn), jnp.zeros_like(to_hd(q))], axis=2)
    out = pl.pallas_call(
        _dil_sample_kernel,
        grid=(nb, groups),
        in_specs=[pl.BlockSpec((None, None, 4 * SAMPLE_HEADS, DH_D), lambda b, g: (b, g, 0, 0)), cache_spec, cache_spec,
                  pl.BlockSpec((len(DIL_CONFIGS), None, SAMPLE_HEADS, length), lambda b, g: (0, g, 0, 0)),
                  pl.BlockSpec((None, SAMPLE_HEADS, 128), lambda b, g: (g, 0, 0))],
        out_specs=hd,
        out_shape=jax.ShapeDtypeStruct((nb, groups, SAMPLE_HEADS, DH_D), f32),
        compiler_params=_cparams("arbitrary", "arbitrary"),
        name="dilated_sample",
    )(qkv, view(cache_k), view(cache_v), bias, b0)
    return out.reshape(nb, QD)


def _route(logits_t, rb):
    per_group = N_EXPERTS // N_EXPERT_GROUPS
    sc = jax.nn.sigmoid(logits_t)
    sel = sc + rb
    sel_r = [sel[e:e + 1, :] for e in range(N_EXPERTS)]
    sc_r = [sc[e:e + 1, :] for e in range(N_EXPERTS)]

    def beats(j, i, vals):
        return (vals[j] >= vals[i]) if j < i else (vals[j] > vals[i])

    gs = []
    for g in range(N_EXPERT_GROUPS):
        a, b, c, d = sel_r[per_group * g:per_group * (g + 1)]
        m1, n1, m2, n2 = jnp.maximum(a, b), jnp.minimum(a, b), jnp.maximum(c, d), jnp.minimum(c, d)
        gs.append(jnp.maximum(m1, m2) + jnp.maximum(jnp.minimum(m1, m2), jnp.maximum(n1, n2)))
    nums = []
    for g in range(N_EXPERT_GROUPS):
        grank = sum(beats(j, g, gs).astype(f32) for j in range(N_EXPERT_GROUPS) if j != g)
        vals = sel_r[per_group * g:per_group * (g + 1)]
        for i in range(per_group):
            rank = sum(beats(j, i, vals).astype(f32) for j in range(per_group) if j != i)
            pick = (grank == 0.0) & (rank < 2.0)
            nums.append(jnp.where(pick, sc_r[per_group * g + i], 0.0))
    den = sum(nums)
    return jnp.concatenate(nums, axis=0) / den


def _wo_kernel(x_ref, oa_ref, md_ref, wo_ref, g1_ref, ng_ref, sc_ref, sh_ref, rwt_ref, rb_ref,
               x1_ref, h_ref, gate_ref):
    y = jnp.dot(oa_ref[...].astype(bf16), wo_ref[0:VA, :], preferred_element_type=f32)
    y = y + jnp.dot(md_ref[...].astype(bf16), wo_ref[VA:, :], preferred_element_type=f32)
    x1 = x_ref[...] + g1_ref[...] * y
    x1_ref[...] = x1
    h = _rmsnorm(x1, ng_ref[...]) * (1.0 + sc_ref[...]) + sh_ref[...]
    h_ref[...] = h.astype(bf16)
    logits_t = lax.dot_general(rwt_ref[...], h, NT, precision=HIGHEST, preferred_element_type=f32)
    gate_ref[...] = _route(logits_t, rb_ref[...])


def _wo_call(x2d, oa, md, wo, g1, ng, sc, sh, rwt, rb, tm, tiles_per_batch):
    n, d = x2d.shape
    row = lambda i: (i, 0)
    const = lambda shape: pl.BlockSpec(shape, lambda i: (0,) * len(shape))
    return pl.pallas_call(
        _wo_kernel,
        grid=(n // tm,),
        in_specs=[
            pl.BlockSpec((tm, d), row),
            pl.BlockSpec((tm, VA), row),
            pl.BlockSpec((tm, QD), row),
            _resident((d, d), lambda i: (0, 0)),
            _mod_spec(g1, tiles_per_batch),
            const((1, d)),
            _mod_spec(sc, tiles_per_batch),
            _mod_spec(sh, tiles_per_batch),
            const((N_EXPERTS, d)),
            const((N_EXPERTS, 1)),
        ],
        out_specs=[pl.BlockSpec((tm, d), row), pl.BlockSpec((tm, d), row), pl.BlockSpec((N_EXPERTS, tm), lambda i: (0, i))],
        out_shape=[jax.ShapeDtypeStruct((n, d), f32), jax.ShapeDtypeStruct((n, d), bf16),
                   jax.ShapeDtypeStruct((N_EXPERTS, n), f32)],
        compiler_params=_cparams("arbitrary"),
        name="wo_router",
    )(x2d, oa, md, wo, g1, ng, sc, sh, rwt, rb)


def _moe_dense_kernel(final, h_ref, gate_ref, wg_ref, wu_ref, wd_ref, x1_ref, g2_ref, fg_ref, out_ref, acc_ref):
    e = pl.program_id(1)

    @pl.when(e == 0)
    def _():
        acc_ref[...] = jnp.zeros_like(acc_ref)

    h = h_ref[...]
    hid = _silu(jnp.dot(h, wg_ref[...], preferred_element_type=f32)) * jnp.dot(h, wu_ref[...], preferred_element_type=f32)
    lane = lax.broadcasted_iota(jnp.int32, (1, N_EXPERTS), 1)
    ge = jnp.sum(jnp.where(lane == e, gate_ref[...], 0.0), axis=-1, keepdims=True)
    acc_ref[...] += jnp.dot((hid * ge).astype(bf16), wd_ref[...], preferred_element_type=f32)

    @pl.when(e == pl.num_programs(1) - 1)
    def _():
        x2 = x1_ref[...] + g2_ref[...] * acc_ref[...]
        out_ref[...] = _rmsnorm(x2, fg_ref[...]) if final else x2


def _moe_dense_call(h, gate, wg, wu, wd, x1, g2, fg, final, tm, tiles_per_batch):
    n, d = x1.shape
    ne, _, ff = wg.shape
    row = lambda i, e: (i, 0)
    return pl.pallas_call(
        functools.partial(_moe_dense_kernel, final),
        grid=(n // tm, ne),
        in_specs=[
            pl.BlockSpec((tm, d), row),
            pl.BlockSpec((tm, N_EXPERTS), row),
            pl.BlockSpec((None, d, ff), lambda i, e: (e, 0, 0)),
            pl.BlockSpec((None, d, ff), lambda i, e: (e, 0, 0)),
            pl.BlockSpec((None, ff, d), lambda i, e: (e, 0, 0)),
            pl.BlockSpec((tm, d), row),
            _mod_spec(g2, tiles_per_batch),
            pl.BlockSpec((1, d), lambda i, e: (0, 0)),
        ],
        out_specs=pl.BlockSpec((tm, d), row),
        out_shape=jax.ShapeDtypeStruct((n, d), f32),
        scratch_shapes=[pltpu.VMEM((tm, d), f32)],
        compiler_params=_cparams("arbitrary", "arbitrary"),
        name="moe_dense",
    )(h, gate, wg, wu, wd, x1, g2, fg)


MOE_TILE = 1024
MOE_ALIGN = 16
MOE_WINDOW = 144
MOE_ROWS = 2560
MOE_SELECT_ROWS = 512
MOE_COMBINE_ROWS = 256
MOE_GROUP = 2


def _plan_kernel(gate_ref, slot_ref, off_ref, cnt_ref):
    t = gate_ref.shape[1]
    gate = gate_ref[...]
    sel = jnp.where(gate > 0.0, 1.0, 0.0)
    before = lax.broadcasted_iota(jnp.int32, (t, t), 0) < lax.broadcasted_iota(jnp.int32, (t, t), 1)
    rank = jnp.dot(sel.astype(bf16), jnp.where(before, 1.0, 0.0).astype(bf16), preferred_element_type=f32)
    cnt = jnp.sum(sel, axis=1, keepdims=True)
    cpad = jnp.floor((cnt + (MOE_ALIGN - 1)) * (1.0 / MOE_ALIGN)) * MOE_ALIGN
    offs, run = [], jnp.zeros((1, 1), f32)
    for e in range(N_EXPERTS):
        offs.append(run)
        run = run + cpad[e:e + 1]
    off = jnp.concatenate(offs, axis=0)
    dest = off + rank
    seen = jnp.zeros((1, t), f32)
    dest_a = dest_b = w_a = w_b = jnp.zeros((1, t), f32)
    n_a = n_b = jnp.zeros((1, t), f32)
    for e in range(N_EXPERTS):
        s_e = sel[e:e + 1]
        is_a = s_e * jnp.where(seen == 0.0, 1.0, 0.0)
        is_b = s_e * jnp.where(seen == 1.0, 1.0, 0.0)
        dest_a, w_a, n_a = dest_a + is_a * dest[e:e + 1], w_a + is_a * gate[e:e + 1], n_a + is_a
        dest_b, w_b, n_b = dest_b + is_b * dest[e:e + 1], w_b + is_b * gate[e:e + 1], n_b + is_b
        seen = seen + s_e
    dest_a = jnp.where(n_a > 0.0, dest_a, -1.0)
    dest_b = jnp.where(n_b > 0.0, dest_b, -1.0)
    slot_ref[...] = jnp.concatenate([dest_a, dest_b, w_a, w_b, jnp.zeros((4, t), f32)], axis=0)
    off_ref[...] = jnp.broadcast_to(off, (N_EXPERTS, 128))
    cnt_ref[...] = jnp.broadcast_to(cnt, (N_EXPERTS, 128))


def _plan_call(gate_t):
    n = gate_t.shape[1]
    nt = n // MOE_TILE
    slots, off, cnt = pl.pallas_call(
        _plan_kernel,
        grid=(nt,),
        in_specs=[pl.BlockSpec((N_EXPERTS, MOE_TILE), lambda i: (0, i))],
        out_specs=[pl.BlockSpec((8, MOE_TILE), lambda i: (0, i)),
                   pl.BlockSpec((None, N_EXPERTS, 128), lambda i: (i, 0, 0)),
                   pl.BlockSpec((None, N_EXPERTS, 128), lambda i: (i, 0, 0))],
        out_shape=[jax.ShapeDtypeStruct((8, n), f32), jax.ShapeDtypeStruct((nt, N_EXPERTS, 128), f32),
                   jax.ShapeDtypeStruct((nt, N_EXPERTS, 128), f32)],
        compiler_params=_cparams("arbitrary"),
        name="moe_plan",
    )(gate_t)
    to_smem = lambda a: a[:, :, 0].astype(jnp.int32).reshape(nt * N_EXPERTS)
    return slots, to_smem(off), to_smem(cnt)


def _moe_sparse_kernel(off_ref, cnt_ref, h_ref, srow_ref, scol_ref, wg_ref, wu_ref, wd_ref, y_ref, xs_ref, ys_ref):
    i, e = pl.program_id(0), pl.program_id(1)
    t = MOE_TILE
    nt = h_ref.shape[0] // t

    @pl.when(e == 0)
    def _():
        for s in range(nt):
            tok = slice(s * t, (s + 1) * t)
            dest_a, dest_b = srow_ref[0:1, tok], srow_ref[1:2, tok]
            for c in range(MOE_ROWS // MOE_SELECT_ROWS):
                r = (lax.broadcasted_iota(jnp.int32, (MOE_SELECT_ROWS, t), 0) + c * MOE_SELECT_ROWS).astype(f32)
                sel = (jnp.where(r == dest_a, 1.0, 0.0) + jnp.where(r == dest_b, 1.0, 0.0)).astype(bf16)
                rows = slice(c * MOE_SELECT_ROWS, (c + 1) * MOE_SELECT_ROWS)
                xs_ref[s, rows, :] = jnp.dot(sel, h_ref[tok, :], preferred_element_type=f32).astype(bf16)
        ys_ref[...] = jnp.zeros_like(ys_ref)

    for s in range(nt):
        off = off_ref[(i * nt + s) * N_EXPERTS + e]
        cnt = cnt_ref[(i * nt + s) * N_EXPERTS + e]

        def window(j, carry, s=s, off=off):
            start = pl.multiple_of(off + j * MOE_WINDOW, MOE_ALIGN)
            xw = xs_ref[s, pl.ds(start, MOE_WINDOW), :]
            hid = _silu(jnp.dot(xw, wg_ref[...], preferred_element_type=f32)) * jnp.dot(xw, wu_ref[...], preferred_element_type=f32)
            ys_ref[s, pl.ds(start, MOE_WINDOW), :] = jnp.dot(hid.astype(bf16), wd_ref[...], preferred_element_type=f32).astype(bf16)
            return carry

        lax.fori_loop(0, (cnt + MOE_WINDOW - 1) // MOE_WINDOW, window, 0)

    @pl.when(e == pl.num_programs(1) - 1)
    def _():
        for s in range(nt):
            for c in range(t // MOE_COMBINE_ROWS):
                rows = slice(s * t + c * MOE_COMBINE_ROWS, s * t + (c + 1) * MOE_COMBINE_ROWS)
                sc = scol_ref[rows, :]
                r = lax.broadcasted_iota(jnp.int32, (MOE_COMBINE_ROWS, MOE_ROWS), 1).astype(f32)
                wsel = jnp.where(r == sc[:, 0:1], sc[:, 2:3], 0.0) + jnp.where(r == sc[:, 1:2], sc[:, 3:4], 0.0)
                y_ref[rows, :] = jnp.dot(wsel.astype(bf16), ys_ref[s], preferred_element_type=f32).astype(y_ref.dtype)


def _moe_sparse_call(h, gate_t, wg, wu, wd):
    n, d = h.shape
    ne, _, ff = wg.shape
    t = MOE_TILE
    tg = MOE_GROUP * t
    assert MOE_ROWS >= 2 * t + (N_EXPERTS - 1) * (MOE_ALIGN - 1) + MOE_WINDOW and MOE_ROWS % MOE_SELECT_ROWS == 0
    slots, off, cnt = _plan_call(gate_t)
    row = lambda i, e, *_: (i, 0)
    grid_spec = pltpu.PrefetchScalarGridSpec(
        num_scalar_prefetch=2,
        grid=(n // tg, ne),
        in_specs=[
            pl.BlockSpec((tg, d), row),
            pl.BlockSpec((8, tg), lambda i, e, *_: (0, i)),
            pl.BlockSpec((tg, 8), row),
            pl.BlockSpec((None, d, ff), lambda i, e, *_: (e, 0, 0)),
            pl.BlockSpec((None, d, ff), lambda i, e, *_: (e, 0, 0)),
            pl.BlockSpec((None, ff, d), lambda i, e, *_: (e, 0, 0)),
        ],
        out_specs=pl.BlockSpec((tg, d), row),
        scratch_shapes=[pltpu.VMEM((MOE_GROUP, MOE_ROWS, d), bf16), pltpu.VMEM((MOE_GROUP, MOE_ROWS, d), bf16)],
    )
    return pl.pallas_call(
        _moe_sparse_kernel,
        grid_spec=grid_spec,
        out_shape=jax.ShapeDtypeStruct((n, d), bf16),
        compiler_params=_cparams("arbitrary", "arbitrary"),
        name="moe_sparse",
    )(off, cnt, h, slots, slots.T, wg, wu, wd)


def _residual_norm_kernel(x1_ref, y_ref, g2_ref, fg_ref, out_ref):
    out_ref[...] = _rmsnorm(x1_ref[...] + g2_ref[...] * y_ref[...].astype(f32), fg_ref[...])


def _residual_norm_call(x1, y, g2, fg, tm, tiles_per_batch):
    n, d = x1.shape
    row = lambda i: (i, 0)
    return pl.pallas_call(
        _residual_norm_kernel,
        grid=(n // tm,),
        in_specs=[pl.BlockSpec((tm, d), row), pl.BlockSpec((tm, d), row), _mod_spec(g2, tiles_per_batch),
                  pl.BlockSpec((1, d), lambda i: (0, 0))],
        out_specs=pl.BlockSpec((tm, d), row),
        out_shape=jax.ShapeDtypeStruct((n, d), f32),
        compiler_params=_cparams("arbitrary"),
        name="residual_norm",
    )(x1, y, g2, fg)


def _reorder_w_in(w_in):
    o = np.cumsum((0, QA, QA, VA, GATE_RANK, VA, QD, QD, QD))
    pad = jnp.zeros(w_in.shape[:-1] + (128 - GATE_RANK,), w_in.dtype)
    parts = [w_in[..., o[0]:o[3]], w_in[..., o[4]:o[5]], w_in[..., o[3]:o[4]], pad, w_in[..., o[5]:o[8]]]
    return jnp.concatenate(parts, axis=-1).astype(bf16)


def _trunk(x, mods, prompt, gla_state, cache_k, cache_v, p):
    nb, s, d = x.shape
    n = nb * s
    depth = p["w_in"].shape[0]
    sh1, sc1, g1, sh2, sc2, g2 = mods
    if prompt:
        tm = 512
        tpb = s // tm
        keep = min(2048, s)
        keep_tiles = keep // tm
    else:
        tm, tpb, keep_tiles = n, 0, 0
    x2d = x.reshape(n, d)
    new_s, new_k, new_v = [], [], []
    resid = None
    for l in range(depth):
        outs = _inproj_call(x2d, sc1[l], sh1[l], p["norm1_g"][l], p["w_in"][l], tm, tpb, keep_tiles, resid)
        if resid is not None:
            x2d, outs = outs[0], outs[1:]
        a, q, k, v = outs[:4]
        if prompt:
            oa, st = _gla_prompt_call(a, p["wg"][l], p["bg"][l], p["gn"][l], nb, tm)
            new_s.append(_state_from_blockdiag(st))
            new_k.append(outs[4][:, 1:].reshape(nb, keep, H_D, DH_D))
            new_v.append(outs[5][:, 1:].reshape(nb, keep, H_D, DH_D))
            qkv = {1: (q, k, v)}
            for j, dil in enumerate(STRIDED_DILS):
                qkv[dil] = outs[6 + 3 * j:9 + 3 * j]
            os, ls = zip(*[_dil_prompt_call(*qkv[dil], p["bias_p"][c], nb, dil) for c, (_, dil) in enumerate(DIL_CONFIGS)])
            md = _combine_call(os, ls, tm)
        else:
            oa, st = _gla_sample_call(a, p["wg"][l], p["bg"][l], p["gn"][l], gla_state[l])
            oa = oa.reshape(n, VA)
            new_s.append(st.reshape(nb, H_A, DK_A, DV_A))
            new_k.append(k.reshape(nb, s, H_D, DH_D))
            new_v.append(v.reshape(nb, s, H_D, DH_D))
            md = _dil_sample_call(q, k, v, cache_k, cache_v, l, *p["bias_s"])
        x1, h2, gate_t = _wo_call(x2d, oa, md, p["w_o"][l], g1[l], p["norm2_g"][l], sc2[l], sh2[l],
                                  p["rwt"], p["rb"], tm, tpb)
        experts = (p["moe_wg"][l], p["moe_wu"][l], p["moe_wd"][l])
        if prompt:
            x2d, resid = x1, (_moe_sparse_call(h2, gate_t, *experts), g2[l])
        else:
            x2d = _moe_dense_call(h2, gate_t.T, *experts, x1, g2[l], p["final_g"], l == depth - 1, tm, tpb)
    if prompt:
        x2d = _residual_norm_call(x2d, *resid, p["final_g"], tm, tpb)
    return x2d.reshape(nb, s, d), jnp.stack(new_s), jnp.stack(new_k), jnp.stack(new_v)


def kernel(x_prompt, x_sample, state_gla, cache_win_k, cache_win_v, c_prompt, c_sample, w_ada, b_ada, norm1_g, norm2_g, w_in, w_gate_up, b_gate, gla_norm_g, w_o, rel_bias, router_w, router_b, moe_w_gate, moe_w_up, moe_w_down, final_norm_g):
    nbp, seq, d = x_prompt.shape
    nbs, dec_seq, _ = x_sample.shape
    depth = w_in.shape[0]
    assert dec_seq == 1 and cache_win_k.shape[2] == DIL_CONFIGS[-1][0], "sample path: one new token over a full window"
    assert seq % DIL_CONFIGS[-1][0] == 0, "prompt length must be a whole number of the widest window"

    c_all = jnp.concatenate([c_prompt, c_sample], axis=0)
    c_all = jnp.pad(c_all, ((0, -c_all.shape[0] % 8), (0, 0)))
    mod = _ada_call(c_all, w_ada, b_ada)
    mods_p = [m[:, :nbp, None, :] for m in jnp.split(mod, 6, axis=-1)]
    mods_s = [m[:, None, nbp:nbp + nbs, :] for m in jnp.split(mod, 6, axis=-1)]

    p = {
        "w_in": _reorder_w_in(w_in),
        "norm1_g": norm1_g.reshape(depth, 1, d),
        "norm2_g": norm2_g.reshape(depth, 1, d),
        "wg": jnp.pad(w_gate_up, ((0, 0), (0, 128 - GATE_RANK), (0, 0))),
        "bg": b_gate.reshape(depth, 1, QA),
        "gn": gla_norm_g.reshape(depth, 1, VA),
        "w_o": w_o.astype(bf16),
        "rwt": router_w.T,
        "rb": router_b.reshape(N_EXPERTS, 1),
        "moe_wg": moe_w_gate.astype(bf16),
        "moe_wu": moe_w_up.astype(bf16),
        "moe_wd": moe_w_down.astype(bf16),
        "final_g": final_norm_g.reshape(1, d),
        "bias_p": [_prompt_bias(rel_bias, dil) for (_, dil) in DIL_CONFIGS],
        "bias_s": _sample_bias(rel_bias, cache_win_k.shape[2]),
    }
    y_p, s_p, k_p, v_p = _trunk(x_prompt, mods_p, True, None, None, None, p)
    y_s, s_s, k_s, v_s = _trunk(x_sample, mods_s, False, state_gla, cache_win_k, cache_win_v, p)
    return (y_p, y_s, s_p, k_p, v_p, s_s, k_s, v_s)
```

```python
import functools
import math

import jax
import jax.numpy as jnp
import numpy as np
from jax import lax
from jax.experimental import pallas as pl
from jax.experimental.pallas import tpu as pltpu

f32 = jnp.float32
bf16 = jnp.bfloat16

H_A, DK_A, DV_A = 4, 32, 64
GATE_RANK = 16
GATE_TAU = 16.0
H_D, DH_D = 12, 64
DIL_CONFIGS = ((128, 1), (512, 4), (2048, 16))
N_BUCKETS = 32
MAX_DISTANCE = 2048
N_EXPERTS = 16
N_EXPERT_GROUPS = 4
EPS = 1e-6

QA = H_A * DK_A
VA = H_A * DV_A
QD = H_D * DH_D
A_WIDTH = 2 * QA + 2 * VA + 128
W_IN_COLS = A_WIDTH + 3 * QD
BLK = 128
N_PAIRS = H_D // 2
NEG = -1e30
GLA_SUB = 16
GLA_ROWS = 128
EXP_CLAMP = 80.0
VMEM_LIMIT = 48 * 1024 * 1024

HIGHEST = lax.Precision.HIGHEST
NT = (((1,), (1,)), ((), ()))
TN = (((0,), (0,)), ((), ()))


def _cparams(*sem):
    return pltpu.CompilerParams(dimension_semantics=sem, vmem_limit_bytes=VMEM_LIMIT)


def _resident(shape, index_map):
    return pl.BlockSpec(shape, index_map, pipeline_mode=pl.Buffered(1))


def _silu(x):
    return x * jax.nn.sigmoid(x)


def _split_bf16(x, terms):
    parts = []
    for _ in range(terms):
        p = x.astype(bf16)
        parts.append(p)
        x = x - p.astype(f32)
    return parts


def _rmsnorm(x, g):
    return x * lax.rsqrt(jnp.mean(x * x, axis=-1, keepdims=True) + EPS) * g


def _ada_kernel(c_ref, w_ref, b_ref, o_ref):
    cs = _silu(c_ref[...]).astype(bf16)
    o_ref[...] = jnp.dot(cs, w_ref[...].astype(bf16), preferred_element_type=f32) + b_ref[...]


def _ada_call(c, w_ada, b_ada):
    depth, d, n6 = w_ada.shape
    m = c.shape[0]
    tn = 1536
    return pl.pallas_call(
        _ada_kernel,
        grid=(depth, n6 // tn),
        in_specs=[
            pl.BlockSpec((m, d), lambda l, j: (0, 0)),
            pl.BlockSpec((None, d, tn), lambda l, j: (l, 0, j)),
            pl.BlockSpec((None, 1, tn), lambda l, j: (l, 0, j)),
        ],
        out_specs=pl.BlockSpec((None, m, tn), lambda l, j: (l, 0, j)),
        out_shape=jax.ShapeDtypeStruct((depth, m, n6), f32),
        compiler_params=_cparams("arbitrary", "arbitrary"),
        name="adaln",
    )(c, w_ada, b_ada.reshape(depth, 1, n6))


def _inproj_kernel(prompt, resid, x_ref, *refs):
    x = x_ref[...]
    if resid:
        y_ref, g2_ref, x2_ref = refs[0], refs[1], refs[6]
        x = x + g2_ref[...] * y_ref[...].astype(f32)
        x2_ref[...] = x
        refs = refs[2:6] + refs[7:]
    sc_ref, sh_ref, g_ref, w_ref = refs[:4]
    outs = refs[4:]
    h = _rmsnorm(x, g_ref[...])
    h = (h * (1.0 + sc_ref[...]) + sh_ref[...]).astype(bf16)

    def proj(lo, hi):
        return jnp.dot(h, w_ref[:, lo:hi], preferred_element_type=f32)

    a_ref, q_ref, k_ref, v_ref = outs[:4]
    a_ref[...] = proj(0, A_WIDTH)
    q = proj(A_WIDTH, A_WIDTH + QD)
    k = proj(A_WIDTH + QD, A_WIDTH + 2 * QD)
    v = proj(A_WIDTH + 2 * QD, A_WIDTH + 3 * QD)
    if prompt:
        n_cls = len(STRIDED_DILS)
        kw_ref, vw_ref = outs[4:6]
        cls_refs = outs[6:6 + 3 * n_cls]
        scratch = outs[6 + 3 * n_cls:]
        q = q * (DH_D ** -0.5)
        kw_ref[...] = k
        vw_ref[...] = v
        for j, (val, nat_ref) in enumerate(((q, q_ref), (k, k_ref), (v, v_ref))):
            nat_ref[...] = val.astype(bf16)
            _to_class_major(val, scratch[j], [cls_refs[3 * c + j] for c in range(n_cls)])
    else:
        q_ref[...] = q
        k_ref[...] = k
        v_ref[...] = v


STRIDED_DILS = tuple(dil for (_, dil) in DIL_CONFIGS if dil > 1)
LANE_TILES = QD // 128


def _to_class_major(val, planes, dst_refs):
    rows = val.shape[0]
    for c in range(LANE_TILES):
        planes[c] = val[:, c * 128:(c + 1) * 128]
    for dil, dst in zip(STRIDED_DILS, dst_refs):
        for c in range(LANE_TILES):
            for r in range(dil):
                lo = r * QD + c * 128
                dst[:, lo:lo + 128] = planes[c, pl.ds(r, rows // dil, stride=dil), :].astype(dst.dtype)


def _from_class_major(src_ref, planes, dil, width):
    rows = planes.shape[1]
    for c in range(width // 128):
        for r in range(dil):
            lo = r * width + c * 128
            planes[c, pl.ds(r, rows // dil, stride=dil), :] = src_ref[:, lo:lo + 128].astype(f32)


def _mod_spec(arr, tiles_per_batch):
    rows, d = arr.shape[1], arr.shape[2]
    if tiles_per_batch:
        return pl.BlockSpec((None, rows, d), lambda i, *_: (i // tiles_per_batch, 0, 0))
    return pl.BlockSpec((None, rows, d), lambda i, *_: (0, 0, 0))


def _inproj_call(x2d, sc, sh, g, w, tm, tiles_per_batch, keep_tiles, resid=None):
    n, d = x2d.shape
    prompt = bool(tiles_per_batch)
    qkv_dt = bf16 if prompt else f32
    row = lambda i: (i, 0)
    out_specs = [pl.BlockSpec((tm, A_WIDTH), row)] + [pl.BlockSpec((tm, QD), row)] * 3
    out_shape = [jax.ShapeDtypeStruct((n, A_WIDTH), f32)] + [jax.ShapeDtypeStruct((n, QD), qkv_dt)] * 3
    scratch = []
    if prompt:
        nb = n // (tm * tiles_per_batch)
        first_keep = tiles_per_batch - keep_tiles

        def win(i):
            return (i // tiles_per_batch, jnp.maximum(i % tiles_per_batch - first_keep + 1, 0), 0, 0)

        out_specs += [pl.BlockSpec((None, None, tm, QD), win)] * 2
        out_shape += [jax.ShapeDtypeStruct((nb, keep_tiles + 1, tm, QD), f32)] * 2
        for dil in STRIDED_DILS:
            out_specs += [pl.BlockSpec((tm // dil, dil * QD), row)] * 3
            out_shape += [jax.ShapeDtypeStruct((n // dil, dil * QD), bf16)] * 3
        scratch = [pltpu.VMEM((LANE_TILES, tm, 128), f32)] * 3
    in_specs = [
        _mod_spec(sc, tiles_per_batch),
        _mod_spec(sh, tiles_per_batch),
        pl.BlockSpec((1, d), lambda i: (0, 0)),
        _resident((d, W_IN_COLS), lambda i: (0, 0)),
    ]
    args = (sc, sh, g, w)
    if resid is not None:
        in_specs = [pl.BlockSpec((tm, d), row), _mod_spec(resid[1], tiles_per_batch)] + in_specs
        args = resid + args
        out_specs = [pl.BlockSpec((tm, d), row)] + out_specs
        out_shape = [jax.ShapeDtypeStruct((n, d), f32)] + out_shape
    return pl.pallas_call(
        functools.partial(_inproj_kernel, prompt, resid is not None),
        grid=(n // tm,),
        in_specs=[pl.BlockSpec((tm, d), row)] + in_specs,
        out_specs=out_specs,
        out_shape=out_shape,
        scratch_shapes=scratch,
        compiler_params=_cparams("arbitrary"),
        name="inproj_prompt" if prompt else "inproj_sample",
    )(x2d, *args)


def _log_gate(glr, wg_ref, bg_ref):
    x = jnp.dot(glr.astype(bf16), wg_ref[...].astype(bf16), preferred_element_type=f32) + bg_ref[...]
    return (jnp.minimum(x, 0.0) - jnp.log1p(jnp.exp(-jnp.abs(x)))) * (1.0 / GATE_TAU)


def _gla_prompt_kernel(a_ref, wg_ref, bg_ref, gn_ref, o_ref, s_ref, st_ref):
    R, C = GLA_ROWS, GLA_SUB

    @pl.when(pl.program_id(1) == 0)
    def _():
        st_ref[...] = jnp.zeros_like(st_ref)

    def iota(shape, dim, shift=0):
        return lax.shift_right_logical(lax.broadcasted_iota(jnp.int32, shape, dim), shift)

    sub, lk, lv = int(math.log2(C)), int(math.log2(DK_A)), int(math.log2(DV_A))
    ri = lax.bitwise_and(iota((H_A * R, R), 0), R - 1)
    ci = iota((H_A * R, R), 1)
    same4 = lax.shift_right_logical(ri, sub) == lax.shift_right_logical(ci, sub)
    causal4 = same4 & (ci <= ri)
    sums = jnp.concatenate([jnp.where(causal4, 1.0, 0.0)[0:R], jnp.where(same4, 1.0, 0.0)[0:R]], axis=0).astype(bf16)
    lane_q = iota((1, QA), 1, lk)
    lane_v = iota((1, VA), 1, lv)
    bd = iota((QA, VA), 0, lk) == iota((QA, VA), 1, lv)
    hmean = jnp.where(iota((VA, VA), 0, lv) == iota((VA, VA), 1, lv), 1.0 / DV_A, 0.0).astype(bf16)
    row_sub = iota((R, 1), 0, sub)
    sub_mean = jnp.where(iota((R, 128), 0, sub) == iota((R, 128), 1), 1.0 / C, 0.0).astype(bf16)
    nsub = R // C

    def prepare(c):
        rows = slice(c * R, (c + 1) * R)
        blk = a_ref[rows, :]
        q = blk[:, 0:QA] * (DK_A ** -0.5)
        k = blk[:, QA:2 * QA]
        v = blk[:, 2 * QA:2 * QA + VA]
        r = blk[:, 2 * QA + VA:2 * QA + 2 * VA]
        g = _log_gate(blk[:, 2 * QA + 2 * VA:], wg_ref, bg_ref)
        cums = sum(jnp.dot(sums, part, preferred_element_type=f32) for part in _split_bf16(g, 3))
        b, bt = cums[0:R], cums[R:2 * R]
        qb = q * jnp.exp(b)
        kd = (k * jnp.exp(bt - b)).astype(bf16)
        ki = (k * jnp.exp(jnp.minimum(-b, EXP_CLAMP))).astype(bf16)
        vb = v.astype(bf16)
        qs = jnp.concatenate([jnp.where(lane_q == h, qb, 0.0) for h in range(H_A)], axis=0).astype(bf16)
        att = lax.dot_general(qs, ki, NT, preferred_element_type=f32)
        att = jnp.where(causal4, att, 0.0).astype(bf16)
        res = jnp.dot(att, vb, preferred_element_type=f32)
        o = jnp.where(lane_v == 0, res[0:R], 0.0)
        for h in range(1, H_A):
            o = o + jnp.where(lane_v == h, res[h * R:(h + 1) * R], 0.0)
        qbb = qb.astype(bf16)
        zero = jnp.zeros((), bf16)
        v_exp = jnp.concatenate([jnp.where(row_sub == i, vb, zero) for i in range(nsub)], axis=1)
        kv_all = lax.dot_general(kd, v_exp, TN, preferred_element_type=f32)
        decay = sum(lax.dot_general(part, sub_mean, TN, preferred_element_type=f32)
                    for part in _split_bf16(jnp.exp(bt), 3))
        q_exp = jnp.concatenate([jnp.where(row_sub == i, qbb, zero) for i in range(nsub)], axis=1)
        return o, kv_all, decay, q_exp, _silu(r)

    def finish(c, st, o, kv_all, decay, q_exp, gate):
        states = []
        for i in range(nsub):
            states.append(st.astype(bf16))
            st = st * decay[:, i:i + 1] + jnp.where(bd, kv_all[:, i * VA:(i + 1) * VA], 0.0)
        o = o + jnp.dot(q_exp, jnp.concatenate(states, axis=0), preferred_element_type=f32)
        ms = sum(jnp.dot(part, hmean, preferred_element_type=f32) for part in _split_bf16(o * o, 2))
        y = o * lax.rsqrt(ms + EPS) * gn_ref[...]
        o_ref[c * R:(c + 1) * R, :] = (y * gate).astype(o_ref.dtype)
        return st

    nblk = a_ref.shape[0] // R
    prepared = [prepare(c) for c in range(nblk)]
    st = st_ref[...]
    for c in range(nblk):
        st = finish(c, st, *prepared[c])
    st_ref[...] = st
    s_ref[...] = st


def _gla_prompt_call(a, wg, bg, gn, nb, tm):
    n = a.shape[0]
    tpb = n // nb // tm
    return pl.pallas_call(
        _gla_prompt_kernel,
        grid=(nb, tpb),
        in_specs=[
            pl.BlockSpec((tm, A_WIDTH), lambda b, j: (b * tpb + j, 0)),
            pl.BlockSpec((128, QA), lambda b, j: (0, 0)),
            pl.BlockSpec((1, QA), lambda b, j: (0, 0)),
            pl.BlockSpec((1, VA), lambda b, j: (0, 0)),
        ],
        out_specs=[
            pl.BlockSpec((tm, VA), lambda b, j: (b * tpb + j, 0)),
            pl.BlockSpec((None, QA, VA), lambda b, j: (b, 0, 0)),
        ],
        out_shape=[jax.ShapeDtypeStruct((n, VA), bf16), jax.ShapeDtypeStruct((nb, QA, VA), f32)],
        scratch_shapes=[pltpu.VMEM((QA, VA), f32)],
        compiler_params=_cparams("arbitrary", "arbitrary"),
        name="gla_prompt",
    )(a, wg, bg, gn)


def _state_from_blockdiag(st):
    b = st.shape[0]
    st5 = st.reshape(b, H_A, DK_A, H_A, DV_A)
    return jnp.stack([st5[:, h, :, h, :] for h in range(H_A)], axis=1)


def _gla_sample_kernel(a_ref, at_ref, wgt_ref, bgt_ref, gn_ref, s_ref, o_ref, so_ref):
    nb = a_ref.shape[0]
    qt = at_ref[0:QA, :] * (DK_A ** -0.5)
    kt = at_ref[QA:2 * QA, :]
    xt = jnp.dot(wgt_ref[...].astype(bf16), at_ref[2 * QA + 2 * VA:, :].astype(bf16), preferred_element_type=f32) + bgt_ref[...]
    et = jnp.exp((jnp.minimum(xt, 0.0) - jnp.log1p(jnp.exp(-jnp.abs(xt)))) * (1.0 / GATE_TAU))
    gn = gn_ref[...]
    for t in range(nb):
        row = a_ref[t:t + 1, :]
        vt = jnp.concatenate(
            [jnp.broadcast_to(row[:, 2 * QA + h * DV_A:2 * QA + (h + 1) * DV_A], (DK_A, DV_A)) for h in range(H_A)], axis=0)
        s_new = s_ref[t] * et[:, t:t + 1] + kt[:, t:t + 1] * vt
        so_ref[t] = s_new
        o = jnp.sum((qt[:, t:t + 1] * s_new).reshape(H_A, DK_A, DV_A), axis=1)
        y = _rmsnorm(o, gn)
        r = jnp.concatenate([row[:, 2 * QA + VA + h * DV_A:2 * QA + VA + (h + 1) * DV_A] for h in range(H_A)], axis=0)
        o_ref[t] = y * _silu(r)


def _gla_sample_call(a, wg, bg, gn, state):
    nb = a.shape[0]
    return pl.pallas_call(
        _gla_sample_kernel,
        out_shape=[jax.ShapeDtypeStruct((nb, H_A, DV_A), f32), jax.ShapeDtypeStruct((nb, QA, DV_A), f32)],
        compiler_params=pltpu.CompilerParams(vmem_limit_bytes=VMEM_LIMIT),
        name="gla_sample",
    )(a, a.T, wg.T, bg.reshape(QA, 1), gn.reshape(H_A, DV_A), state.reshape(nb, QA, DV_A))


def _rel_bucket_np(dist):
    max_exact = N_BUCKETS // 2
    d = np.maximum(dist, 0)
    df = np.maximum(d, 1).astype(np.float32)
    large = max_exact + (np.log(df / np.float32(max_exact)) / np.float32(math.log(MAX_DISTANCE / max_exact))
                         * np.float32(N_BUCKETS - max_exact)).astype(np.int32)
    large = np.minimum(large, N_BUCKETS - 1)
    return np.where(d < max_exact, d, large).astype(np.int32)


def _bias_lookup(rel_bias, buckets):
    onehot = np.zeros((buckets.size, N_BUCKETS), np.float32)
    onehot[np.arange(buckets.size), buckets.reshape(-1)] = 1.0
    out = jnp.dot(jnp.asarray(onehot), rel_bias.astype(f32), precision=HIGHEST)
    return out.reshape(buckets.shape + (rel_bias.shape[-1],))


def _prompt_bias(rel_bias, dil):
    i = np.arange(BLK)[:, None]
    j = np.arange(2 * BLK)[None, :]
    delta = BLK + i - j
    band = (delta >= 0) & (delta <= BLK)
    tbl = _bias_lookup(rel_bias, _rel_bucket_np(delta * dil))
    tbl = jnp.where(jnp.asarray(band)[:, :, None], tbl, NEG)
    first = jnp.where(jnp.asarray(j >= BLK)[:, :, None], tbl, NEG)
    pairs = lambda t: jnp.transpose(t, (2, 0, 1)).reshape(N_PAIRS, 2 * BLK, 2 * BLK)
    return jnp.concatenate([pairs(tbl), pairs(first)], axis=0)


def _dil_prompt_kernel(q_ref, kp_ref, kc_ref, vp_ref, vc_ref, bias_ref, o_ref, lse_ref):
    lane = lax.broadcasted_iota(jnp.int32, (1, 2 * DH_D), 1)
    lo = lane < DH_D
    variant = jnp.where(pl.program_id(2) == 0, N_PAIRS, 0)
    lse_acc = jnp.zeros((BLK, 128), f32)
    lane_h = lax.broadcasted_iota(jnp.int32, (1, 128), 1)
    zero = jnp.zeros((), bf16)
    for p in range(N_PAIRS):
        cols = slice(p * 2 * DH_D, (p + 1) * 2 * DH_D)
        qp = q_ref[:, cols]
        k2 = jnp.concatenate([kp_ref[:, cols], kc_ref[:, cols]], axis=0)
        v2 = jnp.concatenate([vp_ref[:, cols], vc_ref[:, cols]], axis=0)
        halves = []
        for hh, qh in enumerate((jnp.where(lo, qp, zero), jnp.where(lo, zero, qp))):
            s = lax.dot_general(qh, k2, NT, preferred_element_type=f32) + bias_ref[variant + p, hh * BLK:(hh + 1) * BLK, :]
            m = jnp.max(s, axis=-1, keepdims=True)
            e = jnp.exp(s - m)
            den = jnp.sum(e, axis=-1, keepdims=True)
            halves.append(jnp.dot(e.astype(bf16), v2, preferred_element_type=f32) / den)
            lse_acc = jnp.where(lane_h == 2 * p + hh, m + jnp.log(den), lse_acc)
        o_ref[:, cols] = jnp.where(lo, halves[0], halves[1]).astype(o_ref.dtype)
    lse_ref[...] = lse_acc


def _dil_prompt_call(q, k, v, bias, nb, dil):
    rows = q.shape[0] // nb
    nblk = rows // BLK
    view = lambda t: t.reshape(nb, rows, t.shape[-1])
    cur = pl.BlockSpec((None, BLK, QD), lambda b, r, i: (b, i, r))
    prev = pl.BlockSpec((None, BLK, QD), lambda b, r, i: (b, jnp.maximum(i - 1, 0), r))
    o, lse = pl.pallas_call(
        _dil_prompt_kernel,
        grid=(nb, dil, nblk),
        in_specs=[cur, prev, cur, prev, cur, _resident((2 * N_PAIRS, 2 * BLK, 2 * BLK), lambda b, r, i: (0, 0, 0))],
        out_specs=[cur, pl.BlockSpec((None, BLK, 128), lambda b, r, i: (b, i, r))],
        out_shape=[jax.ShapeDtypeStruct((nb, rows, dil * QD), bf16), jax.ShapeDtypeStruct((nb, rows, dil * 128), f32)],
        compiler_params=_cparams("arbitrary", "arbitrary", "arbitrary"),
        name=f"dilated_prompt_d{dil}",
    )(view(q), view(k), view(k), view(v), view(v), bias)
    return o.reshape(nb * rows, dil * QD), lse.reshape(nb * rows, dil * 128)


def _head_expand_np():
    e = np.zeros((128, QD), np.float32)
    for h in range(H_D):
        e[h, h * DH_D:(h + 1) * DH_D] = 1.0
    return e


def _expand(w, e_ref):
    hi = w.astype(bf16)
    lo = (w - hi.astype(f32)).astype(bf16)
    return jnp.dot(hi, e_ref[...], preferred_element_type=f32) + jnp.dot(lo, e_ref[...], preferred_element_type=f32)


def _combine_kernel(*refs):
    nbr = len(DIL_CONFIGS)
    o_refs, l_refs = refs[:nbr], refs[nbr:2 * nbr]
    e_ref, out_ref = refs[2 * nbr], refs[2 * nbr + 1]
    scratch = refs[2 * nbr + 2:]
    o_planes, lses, used = [], [], 0
    for c, (_, dil) in enumerate(DIL_CONFIGS):
        if dil == 1:
            o_planes.append(None)
            lses.append(l_refs[c][...])
        else:
            op, lp = scratch[used], scratch[used + 1]
            used += 2
            _from_class_major(o_refs[c], op, dil, QD)
            _from_class_major(l_refs[c], lp, dil, 128)
            o_planes.append(op)
            lses.append(lp[0])
    m = functools.reduce(jnp.maximum, lses)
    es = [jnp.exp(l - m) for l in lses]
    z = functools.reduce(lambda a, b: a + b, es)
    ws = [_expand(e / z, e_ref) for e in es]
    for t in range(LANE_TILES):
        cols = slice(t * 128, (t + 1) * 128)
        acc = None
        for c in range(nbr):
            o = o_refs[c][:, cols].astype(f32) if o_planes[c] is None else o_planes[c][t]
            acc = ws[c][:, cols] * o if acc is None else acc + ws[c][:, cols] * o
        out_ref[:, cols] = acc.astype(out_ref.dtype)


def _combine_call(os, ls, tm):
    n = os[0].shape[0] * DIL_CONFIGS[0][1]
    row = lambda i: (i, 0)
    o_specs = [pl.BlockSpec((tm // dil, dil * QD), row) for (_, dil) in DIL_CONFIGS]
    l_specs = [pl.BlockSpec((tm // dil, dil * 128), row) for (_, dil) in DIL_CONFIGS]
    scratch = []
    for (_, dil) in DIL_CONFIGS:
        if dil > 1:
            scratch += [pltpu.VMEM((LANE_TILES, tm, 128), f32), pltpu.VMEM((1, tm, 128), f32)]
    return pl.pallas_call(
        _combine_kernel,
        grid=(n // tm,),
        in_specs=o_specs + l_specs + [pl.BlockSpec((128, QD), lambda i: (0, 0))],
        out_specs=pl.BlockSpec((tm, QD), row),
        out_shape=jax.ShapeDtypeStruct((n, QD), bf16),
        scratch_shapes=scratch,
        compiler_params=_cparams("arbitrary"),
        name="branch_combine",
    )(*os, *ls, jnp.asarray(_head_expand_np(), bf16))


SAMPLE_HEADS = 6


def _sample_bias(rel_bias, length):
    dist = length - np.arange(length)
    tbl = _bias_lookup(rel_bias, _rel_bucket_np(dist)).T
    rows = []
    for (_, dil) in DIL_CONFIGS:
        valid = (dist % dil == 0) & (dist <= BLK * dil)
        rows.append(jnp.where(jnp.asarray(valid)[None, :], tbl, NEG))
    b0 = _bias_lookup(rel_bias, _rel_bucket_np(np.zeros((1,), np.int64)))
    groups = H_D // SAMPLE_HEADS
    return (jnp.stack(rows).reshape(len(DIL_CONFIGS), groups, SAMPLE_HEADS, length),
            jnp.broadcast_to(b0.reshape(groups, SAMPLE_HEADS, 1), (groups, SAMPLE_HEADS, 128)))


def _dil_sample_kernel(qkv_ref, kt_ref, vt_ref, bias_ref, b0_ref, o_ref):
    nh = SAMPLE_HEADS
    eye = jnp.where(lax.broadcasted_iota(jnp.int32, (DH_D, DH_D), 0) == lax.broadcasted_iota(jnp.int32, (DH_D, DH_D), 1),
                    1.0, 0.0).astype(f32)
    cols = lax.dot_general(eye, qkv_ref[...], NT, precision=HIGHEST, preferred_element_type=f32)
    qc = cols[:, 0:nh] * (DH_D ** -0.5)
    lg = jnp.concatenate([jnp.sum(kt_ref[h] * qc[:, h:h + 1], axis=0, keepdims=True) for h in range(nh)], axis=0)
    l0 = jnp.concatenate([jnp.sum(qc[:, h:h + 1] * cols[:, nh + h:nh + h + 1], axis=0, keepdims=True) for h in range(nh)],
                         axis=0) + b0_ref[:, 0:1]
    ps, p0s, dens, lses = [], [], [], []
    for c in range(len(DIL_CONFIGS)):
        s = lg + bias_ref[c]
        m = jnp.maximum(jnp.max(s, axis=-1, keepdims=True), l0)
        p = jnp.exp(s - m)
        p0 = jnp.exp(l0 - m)
        den = jnp.sum(p, axis=-1, keepdims=True) + p0
        ps.append(p), p0s.append(p0), dens.append(den), lses.append(m + jnp.log(den))
    m = jnp.maximum(jnp.maximum(lses[0], lses[1]), lses[2])
    es = [jnp.exp(l - m) for l in lses]
    z = es[0] + es[1] + es[2]
    coef = [e / z / den for e, den in zip(es, dens)]
    pmix = coef[0] * ps[0] + coef[1] * ps[1] + coef[2] * ps[2]
    p0mix = coef[0] * p0s[0] + coef[1] * p0s[1] + coef[2] * p0s[2]
    lane = lax.broadcasted_iota(jnp.int32, (1, 128), 1)
    ocols = jnp.zeros((DH_D, 128), f32)
    for h in range(nh):
        oc = jnp.sum(vt_ref[h] * pmix[h:h + 1, :], axis=1, keepdims=True)
        oc = oc + p0mix[h:h + 1, 0:1] * cols[:, 2 * nh + h:2 * nh + h + 1]
        ocols = jnp.where(lane == h, oc, ocols)
    o = lax.dot_general(ocols, eye, TN, precision=HIGHEST, preferred_element_type=f32)
    o_ref[...] = o[0:nh]


def _dil_sample_call(q, kn, vn, cache_k, cache_v, layer, bias, b0):
    nb = q.shape[0]
    depth, _, length = cache_k.shape[:3]
    groups = H_D // SAMPLE_HEADS
    hd = pl.BlockSpec((None, None, SAMPLE_HEADS, DH_D), lambda b, g: (b, g, 0, 0))
    cache_spec = pl.BlockSpec((None, None, None, SAMPLE_HEADS, DH_D, length), lambda b, g: (layer, b, g, 0, 0, 0))
    view = lambda c: jnp.transpose(c, (0, 1, 3, 4, 2)).reshape(depth, nb, groups, SAMPLE_HEADS, DH_D, length)
    to_hd = lambda t: t.reshape(nb, groups, SAMPLE_HEADS, DH_D)
    qkv = jnp.concatenate([to_hd(q), to_hd(kn), to_hd(vn), jnp.zeros_like(to_hd(q))], axis=2)
    out = pl.pallas_call(
        _dil_sample_kernel,
        grid=(nb, groups),
        in_specs=[pl.BlockSpec((None, None, 4 * SAMPLE_HEADS, DH_D), lambda b, g: (b, g, 0, 0)), cache_spec, cache_spec,
                  pl.BlockSpec((len(DIL_CONFIGS), None, SAMPLE_HEADS, length), lambda b, g: (0, g, 0, 0)),
                  pl.BlockSpec((None, SAMPLE_HEADS, 128), lambda b, g: (g, 0, 0))],
        out_specs=hd,
        out_shape=jax.ShapeDtypeStruct((nb, groups, SAMPLE_HEADS, DH_D), f32),
        compiler_params=_cparams("arbitrary", "arbitrary"),
        name="dilated_sample",
    )(qkv, view(cache_k), view(cache_v), bias, b0)
    return out.reshape(nb, QD)


def _route(logits_t, rb):
    per_group = N_EXPERTS // N_EXPERT_GROUPS
    sc = jax.nn.sigmoid(logits_t)
    sel = sc + rb
    sel_r = [sel[e:e + 1, :] for e in range(N_EXPERTS)]
    sc_r = [sc[e:e + 1, :] for e in range(N_EXPERTS)]

    def beats(j, i, vals):
        return (vals[j] >= vals[i]) if j < i else (vals[j] > vals[i])

    gs = []
    for g in range(N_EXPERT_GROUPS):
        a, b, c, d = sel_r[per_group * g:per_group * (g + 1)]
        m1, n1, m2, n2 = jnp.maximum(a, b), jnp.minimum(a, b), jnp.maximum(c, d), jnp.minimum(c, d)
        gs.append(jnp.maximum(m1, m2) + jnp.maximum(jnp.minimum(m1, m2), jnp.maximum(n1, n2)))
    nums = []
    for g in range(N_EXPERT_GROUPS):
        grank = sum(beats(j, g, gs).astype(f32) for j in range(N_EXPERT_GROUPS) if j != g)
        vals = sel_r[per_group * g:per_group * (g + 1)]
        for i in range(per_group):
            rank = sum(beats(j, i, vals).astype(f32) for j in range(per_group) if j != i)
            pick = (grank == 0.0) & (rank < 2.0)
            nums.append(jnp.where(pick, sc_r[per_group * g + i], 0.0))
    den = sum(nums)
    return jnp.concatenate(nums, axis=0) / den


def _wo_kernel(x_ref, oa_ref, md_ref, wo_ref, g1_ref, ng_ref, sc_ref, sh_ref, rwt_ref, rb_ref,
               x1_ref, h_ref, gate_ref):
    y = jnp.dot(oa_ref[...].astype(bf16), wo_ref[0:VA, :], preferred_element_type=f32)
    y = y + jnp.dot(md_ref[...].astype(bf16), wo_ref[VA:, :], preferred_element_type=f32)
    x1 = x_ref[...] + g1_ref[...] * y
    x1_ref[...] = x1
    h = _rmsnorm(x1, ng_ref[...]) * (1.0 + sc_ref[...]) + sh_ref[...]
    h_ref[...] = h.astype(bf16)
    r_hi, r_lo = _split_bf16(rwt_ref[...], 2)
    h_hi, h_lo = _split_bf16(h, 2)
    logits_t = (lax.dot_general(r_hi, h_hi, NT, preferred_element_type=f32)
                + lax.dot_general(r_hi, h_lo, NT, preferred_element_type=f32)
                + lax.dot_general(r_lo, h_hi, NT, preferred_element_type=f32))
    gate_ref[...] = _route(logits_t, rb_ref[...])


def _wo_call(x2d, oa, md, wo, g1, ng, sc, sh, rwt, rb, tm, tiles_per_batch):
    n, d = x2d.shape
    row = lambda i: (i, 0)
    const = lambda shape: pl.BlockSpec(shape, lambda i: (0,) * len(shape))
    return pl.pallas_call(
        _wo_kernel,
        grid=(n // tm,),
        in_specs=[
            pl.BlockSpec((tm, d), row),
            pl.BlockSpec((tm, VA), row),
            pl.BlockSpec((tm, QD), row),
            _resident((d, d), lambda i: (0, 0)),
            _mod_spec(g1, tiles_per_batch),
            const((1, d)),
            _mod_spec(sc, tiles_per_batch),
            _mod_spec(sh, tiles_per_batch),
            const((N_EXPERTS, d)),
            const((N_EXPERTS, 1)),
        ],
        out_specs=[pl.BlockSpec((tm, d), row), pl.BlockSpec((tm, d), row), pl.BlockSpec((N_EXPERTS, tm), lambda i: (0, i))],
        out_shape=[jax.ShapeDtypeStruct((n, d), f32), jax.ShapeDtypeStruct((n, d), bf16),
                   jax.ShapeDtypeStruct((N_EXPERTS, n), f32)],
        compiler_params=_cparams("arbitrary"),
        name="wo_router",
    )(x2d, oa, md, wo, g1, ng, sc, sh, rwt, rb)


def _moe_dense_kernel(final, h_ref, gate_ref, wg_ref, wu_ref, wd_ref, x1_ref, g2_ref, fg_ref, out_ref, acc_ref):
    e = pl.program_id(1)

    @pl.when(e == 0)
    def _():
        acc_ref[...] = jnp.zeros_like(acc_ref)

    h = h_ref[...]
    hid = _silu(jnp.dot(h, wg_ref[...], preferred_element_type=f32)) * jnp.dot(h, wu_ref[...], preferred_element_type=f32)
    lane = lax.broadcasted_iota(jnp.int32, (1, N_EXPERTS), 1)
    ge = jnp.sum(jnp.where(lane == e, gate_ref[...], 0.0), axis=-1, keepdims=True)
    acc_ref[...] += jnp.dot((hid * ge).astype(bf16), wd_ref[...], preferred_element_type=f32)

    @pl.when(e == pl.num_programs(1) - 1)
    def _():
        x2 = x1_ref[...] + g2_ref[...] * acc_ref[...]
        out_ref[...] = _rmsnorm(x2, fg_ref[...]) if final else x2


def _moe_dense_call(h, gate, wg, wu, wd, x1, g2, fg, final, tm, tiles_per_batch):
    n, d = x1.shape
    ne, _, ff = wg.shape
    row = lambda i, e: (i, 0)
    return pl.pallas_call(
        functools.partial(_moe_dense_kernel, final),
        grid=(n // tm, ne),
        in_specs=[
            pl.BlockSpec((tm, d), row),
            pl.BlockSpec((tm, N_EXPERTS), row),
            pl.BlockSpec((None, d, ff), lambda i, e: (e, 0, 0)),
            pl.BlockSpec((None, d, ff), lambda i, e: (e, 0, 0)),
            pl.BlockSpec((None, ff, d), lambda i, e: (e, 0, 0)),
            pl.BlockSpec((tm, d), row),
            _mod_spec(g2, tiles_per_batch),
            pl.BlockSpec((1, d), lambda i, e: (0, 0)),
        ],
        out_specs=pl.BlockSpec((tm, d), row),
        out_shape=jax.ShapeDtypeStruct((n, d), f32),
        scratch_shapes=[pltpu.VMEM((tm, d), f32)],
        compiler_params=_cparams("arbitrary", "arbitrary"),
        name="moe_dense",
    )(h, gate, wg, wu, wd, x1, g2, fg)


MOE_TILE = 1024
MOE_ALIGN = 16
MOE_WINDOW = 144
MOE_ROWS = 2560
MOE_SELECT_ROWS = 512
MOE_COMBINE_ROWS = 256
MOE_GROUP = 2


def _plan_kernel(gate_ref, slot_ref, off_ref, cnt_ref):
    t = gate_ref.shape[1]
    gate = gate_ref[...]
    sel = jnp.where(gate > 0.0, 1.0, 0.0)
    before = lax.broadcasted_iota(jnp.int32, (t, t), 0) < lax.broadcasted_iota(jnp.int32, (t, t), 1)
    rank = jnp.dot(sel.astype(bf16), jnp.where(before, 1.0, 0.0).astype(bf16), preferred_element_type=f32)
    cnt = jnp.sum(sel, axis=1, keepdims=True)
    cpad = jnp.floor((cnt + (MOE_ALIGN - 1)) * (1.0 / MOE_ALIGN)) * MOE_ALIGN
    offs, run = [], jnp.zeros((1, 1), f32)
    for e in range(N_EXPERTS):
        offs.append(run)
        run = run + cpad[e:e + 1]
    off = jnp.concatenate(offs, axis=0)
    dest = off + rank
    seen = jnp.zeros((1, t), f32)
    dest_a = dest_b = w_a = w_b = jnp.zeros((1, t), f32)
    n_a = n_b = jnp.zeros((1, t), f32)
    for e in range(N_EXPERTS):
        s_e = sel[e:e + 1]
        is_a = s_e * jnp.where(seen == 0.0, 1.0, 0.0)
        is_b = s_e * jnp.where(seen == 1.0, 1.0, 0.0)
        dest_a, w_a, n_a = dest_a + is_a * dest[e:e + 1], w_a + is_a * gate[e:e + 1], n_a + is_a
        dest_b, w_b, n_b = dest_b + is_b * dest[e:e + 1], w_b + is_b * gate[e:e + 1], n_b + is_b
        seen = seen + s_e
    dest_a = jnp.where(n_a > 0.0, dest_a, -1.0)
    dest_b = jnp.where(n_b > 0.0, dest_b, -1.0)
    slot_ref[...] = jnp.concatenate([dest_a, dest_b, w_a, w_b, jnp.zeros((4, t), f32)], axis=0)
    off_ref[...] = jnp.broadcast_to(off, (N_EXPERTS, 128))
    cnt_ref[...] = jnp.broadcast_to(cnt, (N_EXPERTS, 128))


def _plan_call(gate_t):
    n = gate_t.shape[1]
    nt = n // MOE_TILE
    slots, off, cnt = pl.pallas_call(
        _plan_kernel,
        grid=(nt,),
        in_specs=[pl.BlockSpec((N_EXPERTS, MOE_TILE), lambda i: (0, i))],
        out_specs=[pl.BlockSpec((8, MOE_TILE), lambda i: (0, i)),
                   pl.BlockSpec((None, N_EXPERTS, 128), lambda i: (i, 0, 0)),
                   pl.BlockSpec((None, N_EXPERTS, 128), lambda i: (i, 0, 0))],
        out_shape=[jax.ShapeDtypeStruct((8, n), f32), jax.ShapeDtypeStruct((nt, N_EXPERTS, 128), f32),
                   jax.ShapeDtypeStruct((nt, N_EXPERTS, 128), f32)],
        compiler_params=_cparams("arbitrary"),
        name="moe_plan",
    )(gate_t)
    to_smem = lambda a: a[:, :, 0].astype(jnp.int32).reshape(nt * N_EXPERTS)
    return slots, to_smem(off), to_smem(cnt)


def _moe_sparse_kernel(off_ref, cnt_ref, h_ref, srow_ref, scol_ref, wg_ref, wu_ref, wd_ref, y_ref, xs_ref, ys_ref):
    i, e = pl.program_id(0), pl.program_id(1)
    t = MOE_TILE
    nt = h_ref.shape[0] // t

    @pl.when(e == 0)
    def _():
        for s in range(nt):
            tok = slice(s * t, (s + 1) * t)
            dest_a, dest_b = srow_ref[0:1, tok], srow_ref[1:2, tok]
            for c in range(MOE_ROWS // MOE_SELECT_ROWS):
                r = (lax.broadcasted_iota(jnp.int32, (MOE_SELECT_ROWS, t), 0) + c * MOE_SELECT_ROWS).astype(f32)
                sel = (jnp.where(r == dest_a, 1.0, 0.0) + jnp.where(r == dest_b, 1.0, 0.0)).astype(bf16)
                rows = slice(c * MOE_SELECT_ROWS, (c + 1) * MOE_SELECT_ROWS)
                xs_ref[s, rows, :] = jnp.dot(sel, h_ref[tok, :], preferred_element_type=f32).astype(bf16)
        ys_ref[...] = jnp.zeros_like(ys_ref)

    for s in range(nt):
        off = off_ref[(i * nt + s) * N_EXPERTS + e]
        cnt = cnt_ref[(i * nt + s) * N_EXPERTS + e]

        def window(j, carry, s=s, off=off):
            start = pl.multiple_of(off + j * MOE_WINDOW, MOE_ALIGN)
            xw = xs_ref[s, pl.ds(start, MOE_WINDOW), :]
            hid = _silu(jnp.dot(xw, wg_ref[...], preferred_element_type=f32)) * jnp.dot(xw, wu_ref[...], preferred_element_type=f32)
            ys_ref[s, pl.ds(start, MOE_WINDOW), :] = jnp.dot(hid.astype(bf16), wd_ref[...], preferred_element_type=f32).astype(bf16)
            return carry

        lax.fori_loop(0, (cnt + MOE_WINDOW - 1) // MOE_WINDOW, window, 0)

    @pl.when(e == pl.num_programs(1) - 1)
    def _():
        for s in range(nt):
            for c in range(t // MOE_COMBINE_ROWS):
                rows = slice(s * t + c * MOE_COMBINE_ROWS, s * t + (c + 1) * MOE_COMBINE_ROWS)
                sc = scol_ref[rows, :]
                r = lax.broadcasted_iota(jnp.int32, (MOE_COMBINE_ROWS, MOE_ROWS), 1).astype(f32)
                wsel = jnp.where(r == sc[:, 0:1], sc[:, 2:3], 0.0) + jnp.where(r == sc[:, 1:2], sc[:, 3:4], 0.0)
                y_ref[rows, :] = jnp.dot(wsel.astype(bf16), ys_ref[s], preferred_element_type=f32).astype(y_ref.dtype)


def _moe_sparse_call(h, gate_t, wg, wu, wd):
    n, d = h.shape
    ne, _, ff = wg.shape
    t = MOE_TILE
    tg = MOE_GROUP * t
    assert MOE_ROWS >= 2 * t + (N_EXPERTS - 1) * (MOE_ALIGN - 1) + MOE_WINDOW and MOE_ROWS % MOE_SELECT_ROWS == 0
    slots, off, cnt = _plan_call(gate_t)
    row = lambda i, e, *_: (i, 0)
    grid_spec = pltpu.PrefetchScalarGridSpec(
        num_scalar_prefetch=2,
        grid=(n // tg, ne),
        in_specs=[
            pl.BlockSpec((tg, d), row),
            pl.BlockSpec((8, tg), lambda i, e, *_: (0, i)),
            pl.BlockSpec((tg, 8), row),
            pl.BlockSpec((None, d, ff), lambda i, e, *_: (e, 0, 0)),
            pl.BlockSpec((None, d, ff), lambda i, e, *_: (e, 0, 0)),
            pl.BlockSpec((None, ff, d), lambda i, e, *_: (e, 0, 0)),
        ],
        out_specs=pl.BlockSpec((tg, d), row),
        scratch_shapes=[pltpu.VMEM((MOE_GROUP, MOE_ROWS, d), bf16), pltpu.VMEM((MOE_GROUP, MOE_ROWS, d), bf16)],
    )
    return pl.pallas_call(
        _moe_sparse_kernel,
        grid_spec=grid_spec,
        out_shape=jax.ShapeDtypeStruct((n, d), bf16),
        compiler_params=_cparams("arbitrary", "arbitrary"),
        name="moe_sparse",
    )(off, cnt, h, slots, slots.T, wg, wu, wd)


def _residual_norm_kernel(x1_ref, y_ref, g2_ref, fg_ref, out_ref):
    out_ref[...] = _rmsnorm(x1_ref[...] + g2_ref[...] * y_ref[...].astype(f32), fg_ref[...])


def _residual_norm_call(x1, y, g2, fg, tm, tiles_per_batch):
    n, d = x1.shape
    row = lambda i: (i, 0)
    return pl.pallas_call(
        _residual_norm_kernel,
        grid=(n // tm,),
        in_specs=[pl.BlockSpec((tm, d), row), pl.BlockSpec((tm, d), row), _mod_spec(g2, tiles_per_batch),
                  pl.BlockSpec((1, d), lambda i: (0, 0))],
        out_specs=pl.BlockSpec((tm, d), row),
        out_shape=jax.ShapeDtypeStruct((n, d), f32),
        compiler_params=_cparams("arbitrary"),
        name="residual_norm",
    )(x1, y, g2, fg)


def _reorder_w_in(w_in):
    o = np.cumsum((0, QA, QA, VA, GATE_RANK, VA, QD, QD, QD))
    pad = jnp.zeros(w_in.shape[:-1] + (128 - GATE_RANK,), w_in.dtype)
    parts = [w_in[..., o[0]:o[3]], w_in[..., o[4]:o[5]], w_in[..., o[3]:o[4]], pad, w_in[..., o[5]:o[8]]]
    return jnp.concatenate(parts, axis=-1).astype(bf16)


def _trunk(x, mods, prompt, gla_state, cache_k, cache_v, p):
    nb, s, d = x.shape
    n = nb * s
    depth = p["w_in"].shape[0]
    sh1, sc1, g1, sh2, sc2, g2 = mods
    if prompt:
        tm = 512
        tpb = s // tm
        keep = min(2048, s)
        keep_tiles = keep // tm
    else:
        tm, tpb, keep_tiles = n, 0, 0
    x2d = x.reshape(n, d)
    new_s, new_k, new_v = [], [], []
    resid = None
    for l in range(depth):
        outs = _inproj_call(x2d, sc1[l], sh1[l], p["norm1_g"][l], p["w_in"][l], tm, tpb, keep_tiles, resid)
        if resid is not None:
            x2d, outs = outs[0], outs[1:]
        a, q, k, v = outs[:4]
        if prompt:
            oa, st = _gla_prompt_call(a, p["wg"][l], p["bg"][l], p["gn"][l], nb, tm)
            new_s.append(_state_from_blockdiag(st))
            new_k.append(outs[4])
            new_v.append(outs[5])
            qkv = {1: (q, k, v)}
            for j, dil in enumerate(STRIDED_DILS):
                qkv[dil] = outs[6 + 3 * j:9 + 3 * j]
            os, ls = zip(*[_dil_prompt_call(*qkv[dil], p["bias_p"][c], nb, dil) for c, (_, dil) in enumerate(DIL_CONFIGS)])
            md = _combine_call(os, ls, tm)
        else:
            oa, st = _gla_sample_call(a, p["wg"][l], p["bg"][l], p["gn"][l], gla_state[l])
            oa = oa.reshape(n, VA)
            new_s.append(st.reshape(nb, H_A, DK_A, DV_A))
            new_k.append(k.reshape(nb, s, H_D, DH_D))
            new_v.append(v.reshape(nb, s, H_D, DH_D))
            md = _dil_sample_call(q, k, v, cache_k, cache_v, l, *p["bias_s"])
        x1, h2, gate_t = _wo_call(x2d, oa, md, p["w_o"][l], g1[l], p["norm2_g"][l], sc2[l], sh2[l],
                                  p["rwt"], p["rb"], tm, tpb)
        experts = (p["moe_wg"][l], p["moe_wu"][l], p["moe_wd"][l])
        if prompt:
            x2d, resid = x1, (_moe_sparse_call(h2, gate_t, *experts), g2[l])
        else:
            x2d = _moe_dense_call(h2, gate_t.T, *experts, x1, g2[l], p["final_g"], l == depth - 1, tm, tpb)
    new_k, new_v = jnp.stack(new_k), jnp.stack(new_v)
    if prompt:
        x2d = _residual_norm_call(x2d, *resid, p["final_g"], tm, tpb)
        new_k = new_k[:, :, 1:].reshape(depth, nb, keep, H_D, DH_D)
        new_v = new_v[:, :, 1:].reshape(depth, nb, keep, H_D, DH_D)
    return x2d.reshape(nb, s, d), jnp.stack(new_s), new_k, new_v


def kernel(x_prompt, x_sample, state_gla, cache_win_k, cache_win_v, c_prompt, c_sample, w_ada, b_ada, norm1_g, norm2_g, w_in, w_gate_up, b_gate, gla_norm_g, w_o, rel_bias, router_w, router_b, moe_w_gate, moe_w_up, moe_w_down, final_norm_g):
    nbp, seq, d = x_prompt.shape
    nbs, dec_seq, _ = x_sample.shape
    depth = w_in.shape[0]
    assert dec_seq == 1 and cache_win_k.shape[2] == DIL_CONFIGS[-1][0], "sample path: one new token over a full window"
    assert seq % DIL_CONFIGS[-1][0] == 0, "prompt length must be a whole number of the widest window"

    c_all = jnp.concatenate([c_prompt, c_sample], axis=0)
    c_all = jnp.pad(c_all, ((0, -c_all.shape[0] % 8), (0, 0)))
    mod = _ada_call(c_all, w_ada, b_ada)
    mods_p = [m[:, :nbp, None, :] for m in jnp.split(mod, 6, axis=-1)]
    mods_s = [m[:, None, nbp:nbp + nbs, :] for m in jnp.split(mod, 6, axis=-1)]

    p = {
        "w_in": _reorder_w_in(w_in),
        "norm1_g": norm1_g.reshape(depth, 1, d),
        "norm2_g": norm2_g.reshape(depth, 1, d),
        "wg": jnp.pad(w_gate_up, ((0, 0), (0, 128 - GATE_RANK), (0, 0))),
        "bg": b_gate.reshape(depth, 1, QA),
        "gn": gla_norm_g.reshape(depth, 1, VA),
        "w_o": w_o.astype(bf16),
        "rwt": router_w.T,
        "rb": router_b.reshape(N_EXPERTS, 1),
        "moe_wg": moe_w_gate.astype(bf16),
        "moe_wu": moe_w_up.astype(bf16),
        "moe_wd": moe_w_down.astype(bf16),
        "final_g": final_norm_g.reshape(1, d),
        "bias_p": [_prompt_bias(rel_bias, dil) for (_, dil) in DIL_CONFIGS],
        "bias_s": _sample_bias(rel_bias, cache_win_k.shape[2]),
    }
    y_p, s_p, k_p, v_p = _trunk(x_prompt, mods_p, True, None, None, None, p)
    y_s, s_s, k_s, v_s = _trunk(x_sample, mods_s, False, state_gla, cache_win_k, cache_win_v, p)
    return (y_p, y_s, s_p, k_p, v_p, s_s, k_s, v_s)
```

```python
import functools
import math

import jax
import jax.numpy as jnp
import numpy as np
from jax import lax
from jax.experimental import pallas as pl
from jax.experimental.pallas import tpu as pltpu

f32 = jnp.float32
bf16 = jnp.bfloat16

H_A, DK_A, DV_A = 4, 32, 64
GATE_RANK = 16
GATE_TAU = 16.0
H_D, DH_D = 12, 64
DIL_CONFIGS = ((128, 1), (512, 4), (2048, 16))
N_BUCKETS = 32
MAX_DISTANCE = 2048
N_EXPERTS = 16
N_EXPERT_GROUPS = 4
EPS = 1e-6

QA = H_A * DK_A
VA = H_A * DV_A
QD = H_D * DH_D
A_WIDTH = 2 * QA + 2 * VA + 128
W_IN_COLS = A_WIDTH + 3 * QD
BLK = 128
N_PAIRS = H_D // 2
NEG = -1e30
GLA_SUB = 16
GLA_ROWS = 128
EXP_CLAMP = 80.0
VMEM_LIMIT = 48 * 1024 * 1024

HIGHEST = lax.Precision.HIGHEST
NT = (((1,), (1,)), ((), ()))
TN = (((0,), (0,)), ((), ()))


def _cparams(*sem):
    return pltpu.CompilerParams(dimension_semantics=sem, vmem_limit_bytes=VMEM_LIMIT)


def _resident(shape, index_map):
    return pl.BlockSpec(shape, index_map, pipeline_mode=pl.Buffered(1))


def _silu(x):
    return x * jax.nn.sigmoid(x)


def _split_bf16(x, terms):
    parts = []
    for _ in range(terms):
        p = x.astype(bf16)
        parts.append(p)
        x = x - p.astype(f32)
    return parts


def _rmsnorm(x, g):
    return x * lax.rsqrt(jnp.mean(x * x, axis=-1, keepdims=True) + EPS) * g


def _ada_kernel(c_ref, w_ref, b_ref, o_ref):
    cs = _silu(c_ref[...]).astype(bf16)
    o_ref[...] = jnp.dot(cs, w_ref[...].astype(bf16), preferred_element_type=f32) + b_ref[...]


def _ada_call(c, w_ada, b_ada):
    depth, d, n6 = w_ada.shape
    m = c.shape[0]
    tn = 1536
    return pl.pallas_call(
        _ada_kernel,
        grid=(depth, n6 // tn),
        in_specs=[
            pl.BlockSpec((m, d), lambda l, j: (0, 0)),
            pl.BlockSpec((None, d, tn), lambda l, j: (l, 0, j)),
            pl.BlockSpec((None, 1, tn), lambda l, j: (l, 0, j)),
        ],
        out_specs=pl.BlockSpec((None, m, tn), lambda l, j: (l, 0, j)),
        out_shape=jax.ShapeDtypeStruct((depth, m, n6), f32),
        compiler_params=_cparams("arbitrary", "arbitrary"),
        name="adaln",
    )(c, w_ada, b_ada.reshape(depth, 1, n6))


def _inproj_kernel(prompt, resid, x_ref, *refs):
    x = x_ref[...]
    if resid:
        y_ref, g2_ref, x2_ref = refs[0], refs[1], refs[6]
        x = x + g2_ref[...] * y_ref[...].astype(f32)
        x2_ref[...] = x
        refs = refs[2:6] + refs[7:]
    sc_ref, sh_ref, g_ref, w_ref = refs[:4]
    outs = refs[4:]
    h = _rmsnorm(x, g_ref[...])
    h = (h * (1.0 + sc_ref[...]) + sh_ref[...]).astype(bf16)

    def proj(lo, hi):
        return jnp.dot(h, w_ref[:, lo:hi], preferred_element_type=f32)

    a_ref, q_ref, k_ref, v_ref = outs[:4]
    a_ref[...] = proj(0, A_WIDTH)
    q = proj(A_WIDTH, A_WIDTH + QD)
    k = proj(A_WIDTH + QD, A_WIDTH + 2 * QD)
    v = proj(A_WIDTH + 2 * QD, A_WIDTH + 3 * QD)
    if prompt:
        n_cls = len(STRIDED_DILS)
        kw_ref, vw_ref = outs[4:6]
        cls_refs = outs[6:6 + 3 * n_cls]
        scratch = outs[6 + 3 * n_cls:]
        q = q * (DH_D ** -0.5)
        kw_ref[...] = k
        vw_ref[...] = v
        for j, (val, nat_ref) in enumerate(((q, q_ref), (k, k_ref), (v, v_ref))):
            nat_ref[...] = val.astype(bf16)
            _to_class_major(val, scratch[j], [cls_refs[3 * c + j] for c in range(n_cls)])
    else:
        q_ref[...] = q
        k_ref[...] = k
        v_ref[...] = v


STRIDED_DILS = tuple(dil for (_, dil) in DIL_CONFIGS if dil > 1)
LANE_TILES = QD // 128


def _to_class_major(val, planes, dst_refs):
    rows = val.shape[0]
    for c in range(LANE_TILES):
        planes[c] = val[:, c * 128:(c + 1) * 128]
    for dil, dst in zip(STRIDED_DILS, dst_refs):
        for c in range(LANE_TILES):
            for r in range(dil):
                lo = r * QD + c * 128
                dst[:, lo:lo + 128] = planes[c, pl.ds(r, rows // dil, stride=dil), :].astype(dst.dtype)


def _from_class_major(src_ref, planes, dil, width):
    rows = planes.shape[1]
    for c in range(width // 128):
        for r in range(dil):
            lo = r * width + c * 128
            planes[c, pl.ds(r, rows // dil, stride=dil), :] = src_ref[:, lo:lo + 128].astype(f32)


def _mod_spec(arr, tiles_per_batch):
    rows, d = arr.shape[1], arr.shape[2]
    if tiles_per_batch:
        return pl.BlockSpec((None, rows, d), lambda i, *_: (i // tiles_per_batch, 0, 0))
    return pl.BlockSpec((None, rows, d), lambda i, *_: (0, 0, 0))


def _inproj_call(x2d, sc, sh, g, w, tm, tiles_per_batch, keep_tiles, resid=None):
    n, d = x2d.shape
    prompt = bool(tiles_per_batch)
    qkv_dt = bf16 if prompt else f32
    row = lambda i: (i, 0)
    out_specs = [pl.BlockSpec((tm, A_WIDTH), row)] + [pl.BlockSpec((tm, QD), row)] * 3
    out_shape = [jax.ShapeDtypeStruct((n, A_WIDTH), f32)] + [jax.ShapeDtypeStruct((n, QD), qkv_dt)] * 3
    scratch = []
    if prompt:
        nb = n // (tm * tiles_per_batch)
        first_keep = tiles_per_batch - keep_tiles

        def win(i):
            return (i // tiles_per_batch, jnp.maximum(i % tiles_per_batch - first_keep + 1, 0), 0, 0)

        out_specs += [pl.BlockSpec((None, None, tm, QD), win)] * 2
        out_shape += [jax.ShapeDtypeStruct((nb, keep_tiles + 1, tm, QD), f32)] * 2
        for dil in STRIDED_DILS:
            out_specs += [pl.BlockSpec((tm // dil, dil * QD), row)] * 3
            out_shape += [jax.ShapeDtypeStruct((n // dil, dil * QD), bf16)] * 3
        scratch = [pltpu.VMEM((LANE_TILES, tm, 128), f32)] * 3
    in_specs = [
        _mod_spec(sc, tiles_per_batch),
        _mod_spec(sh, tiles_per_batch),
        pl.BlockSpec((1, d), lambda i: (0, 0)),
        _resident((d, W_IN_COLS), lambda i: (0, 0)),
    ]
    args = (sc, sh, g, w)
    if resid is not None:
        in_specs = [pl.BlockSpec((tm, d), row), _mod_spec(resid[1], tiles_per_batch)] + in_specs
        args = resid + args
        out_specs = [pl.BlockSpec((tm, d), row)] + out_specs
        out_shape = [jax.ShapeDtypeStruct((n, d), f32)] + out_shape
    return pl.pallas_call(
        functools.partial(_inproj_kernel, prompt, resid is not None),
        grid=(n // tm,),
        in_specs=[pl.BlockSpec((tm, d), row)] + in_specs,
        out_specs=out_specs,
        out_shape=out_shape,
        scratch_shapes=scratch,
        compiler_params=_cparams("arbitrary"),
        name="inproj_prompt" if prompt else "inproj_sample",
    )(x2d, *args)


def _log_gate(glr, wg_ref, bg_ref):
    x = jnp.dot(glr.astype(bf16), wg_ref[...].astype(bf16), preferred_element_type=f32) + bg_ref[...]
    return (jnp.minimum(x, 0.0) - jnp.log1p(jnp.exp(-jnp.abs(x)))) * (1.0 / GATE_TAU)


def _gla_prompt_kernel(a_ref, wg_ref, bg_ref, gn_ref, o_ref, s_ref, st_ref):
    R, C = GLA_ROWS, GLA_SUB

    @pl.when(pl.program_id(1) == 0)
    def _():
        st_ref[...] = jnp.zeros_like(st_ref)

    def iota(shape, dim, shift=0):
        return lax.shift_right_logical(lax.broadcasted_iota(jnp.int32, shape, dim), shift)

    sub, lk, lv = int(math.log2(C)), int(math.log2(DK_A)), int(math.log2(DV_A))
    ri = lax.bitwise_and(iota((H_A * R, R), 0), R - 1)
    ci = iota((H_A * R, R), 1)
    same4 = lax.shift_right_logical(ri, sub) == lax.shift_right_logical(ci, sub)
    causal4 = same4 & (ci <= ri)
    sums = jnp.concatenate([jnp.where(causal4, 1.0, 0.0)[0:R], jnp.where(same4, 1.0, 0.0)[0:R]], axis=0).astype(bf16)
    lane_q = iota((1, QA), 1, lk)
    lane_v = iota((1, VA), 1, lv)
    bd = iota((QA, VA), 0, lk) == iota((QA, VA), 1, lv)
    hmean = jnp.where(iota((VA, VA), 0, lv) == iota((VA, VA), 1, lv), 1.0 / DV_A, 0.0).astype(bf16)
    row_sub = iota((R, 1), 0, sub)
    sub_mean = jnp.where(iota((R, 128), 0, sub) == iota((R, 128), 1), 1.0 / C, 0.0).astype(bf16)
    nsub = R // C

    def prepare(c):
        rows = slice(c * R, (c + 1) * R)
        blk = a_ref[rows, :]
        q = blk[:, 0:QA] * (DK_A ** -0.5)
        k = blk[:, QA:2 * QA]
        v = blk[:, 2 * QA:2 * QA + VA]
        r = blk[:, 2 * QA + VA:2 * QA + 2 * VA]
        g = _log_gate(blk[:, 2 * QA + 2 * VA:], wg_ref, bg_ref)
        cums = sum(jnp.dot(sums, part, preferred_element_type=f32) for part in _split_bf16(g, 3))
        b, bt = cums[0:R], cums[R:2 * R]
        qb = q * jnp.exp(b)
        kd = (k * jnp.exp(bt - b)).astype(bf16)
        ki = (k * jnp.exp(jnp.minimum(-b, EXP_CLAMP))).astype(bf16)
        vb = v.astype(bf16)
        qs = jnp.concatenate([jnp.where(lane_q == h, qb, 0.0) for h in range(H_A)], axis=0).astype(bf16)
        att = lax.dot_general(qs, ki, NT, preferred_element_type=f32)
        att = jnp.where(causal4, att, 0.0).astype(bf16)
        res = jnp.dot(att, vb, preferred_element_type=f32)
        o = jnp.where(lane_v == 0, res[0:R], 0.0)
        for h in range(1, H_A):
            o = o + jnp.where(lane_v == h, res[h * R:(h + 1) * R], 0.0)
        qbb = qb.astype(bf16)
        zero = jnp.zeros((), bf16)
        v_exp = jnp.concatenate([jnp.where(row_sub == i, vb, zero) for i in range(nsub)], axis=1)
        kv_all = lax.dot_general(kd, v_exp, TN, preferred_element_type=f32)
        decay = sum(lax.dot_general(part, sub_mean, TN, preferred_element_type=f32)
                    for part in _split_bf16(jnp.exp(bt), 3))
        q_exp = jnp.concatenate([jnp.where(row_sub == i, qbb, zero) for i in range(nsub)], axis=1)
        return o, kv_all, decay, q_exp, _silu(r)

    def finish(c, st, o, kv_all, decay, q_exp, gate):
        states = []
        for i in range(nsub):
            states.append(st.astype(bf16))
            st = st * decay[:, i:i + 1] + jnp.where(bd, kv_all[:, i * VA:(i + 1) * VA], 0.0)
        o = o + jnp.dot(q_exp, jnp.concatenate(states, axis=0), preferred_element_type=f32)
        ms = sum(jnp.dot(part, hmean, preferred_element_type=f32) for part in _split_bf16(o * o, 2))
        y = o * lax.rsqrt(ms + EPS) * gn_ref[...]
        o_ref[c * R:(c + 1) * R, :] = (y * gate).astype(o_ref.dtype)
        return st

    nblk = a_ref.shape[0] // R
    prepared = [prepare(c) for c in range(nblk)]
    st = st_ref[...]
    for c in range(nblk):
        st = finish(c, st, *prepared[c])
    st_ref[...] = st
    s_ref[...] = st


def _gla_prompt_call(a, wg, bg, gn, nb, tm):
    n = a.shape[0]
    tpb = n // nb // tm
    return pl.pallas_call(
        _gla_prompt_kernel,
        grid=(nb, tpb),
        in_specs=[
            pl.BlockSpec((tm, A_WIDTH), lambda b, j: (b * tpb + j, 0)),
            pl.BlockSpec((128, QA), lambda b, j: (0, 0)),
            pl.BlockSpec((1, QA), lambda b, j: (0, 0)),
            pl.BlockSpec((1, VA), lambda b, j: (0, 0)),
        ],
        out_specs=[
            pl.BlockSpec((tm, VA), lambda b, j: (b * tpb + j, 0)),
            pl.BlockSpec((None, QA, VA), lambda b, j: (b, 0, 0)),
        ],
        out_shape=[jax.ShapeDtypeStruct((n, VA), bf16), jax.ShapeDtypeStruct((nb, QA, VA), f32)],
        scratch_shapes=[pltpu.VMEM((QA, VA), f32)],
        compiler_params=_cparams("arbitrary", "arbitrary"),
        name="gla_prompt",
    )(a, wg, bg, gn)


def _state_from_blockdiag(st):
    b = st.shape[0]
    st5 = st.reshape(b, H_A, DK_A, H_A, DV_A)
    return jnp.stack([st5[:, h, :, h, :] for h in range(H_A)], axis=1)


def _gla_sample_kernel(a_ref, at_ref, wgt_ref, bgt_ref, gn_ref, s_ref, o_ref, so_ref):
    nb = a_ref.shape[0]
    qt = at_ref[0:QA, :] * (DK_A ** -0.5)
    kt = at_ref[QA:2 * QA, :]
    xt = jnp.dot(wgt_ref[...].astype(bf16), at_ref[2 * QA + 2 * VA:, :].astype(bf16), preferred_element_type=f32) + bgt_ref[...]
    et = jnp.exp((jnp.minimum(xt, 0.0) - jnp.log1p(jnp.exp(-jnp.abs(xt)))) * (1.0 / GATE_TAU))
    gn = gn_ref[...]
    for t in range(nb):
        row = a_ref[t:t + 1, :]
        vt = jnp.concatenate(
            [jnp.broadcast_to(row[:, 2 * QA + h * DV_A:2 * QA + (h + 1) * DV_A], (DK_A, DV_A)) for h in range(H_A)], axis=0)
        s_new = s_ref[t] * et[:, t:t + 1] + kt[:, t:t + 1] * vt
        so_ref[t] = s_new
        o = jnp.sum((qt[:, t:t + 1] * s_new).reshape(H_A, DK_A, DV_A), axis=1)
        y = _rmsnorm(o, gn)
        r = jnp.concatenate([row[:, 2 * QA + VA + h * DV_A:2 * QA + VA + (h + 1) * DV_A] for h in range(H_A)], axis=0)
        o_ref[t] = y * _silu(r)


def _gla_sample_call(a, wg, bg, gn, state):
    nb = a.shape[0]
    return pl.pallas_call(
        _gla_sample_kernel,
        out_shape=[jax.ShapeDtypeStruct((nb, H_A, DV_A), f32), jax.ShapeDtypeStruct((nb, QA, DV_A), f32)],
        compiler_params=pltpu.CompilerParams(vmem_limit_bytes=VMEM_LIMIT),
        name="gla_sample",
    )(a, a.T, wg.T, bg.reshape(QA, 1), gn.reshape(H_A, DV_A), state.reshape(nb, QA, DV_A))


def _rel_bucket_np(dist):
    max_exact = N_BUCKETS // 2
    d = np.maximum(dist, 0)
    df = np.maximum(d, 1).astype(np.float32)
    large = max_exact + (np.log(df / np.float32(max_exact)) / np.float32(math.log(MAX_DISTANCE / max_exact))
                         * np.float32(N_BUCKETS - max_exact)).astype(np.int32)
    large = np.minimum(large, N_BUCKETS - 1)
    return np.where(d < max_exact, d, large).astype(np.int32)


def _bias_lookup(rel_bias, buckets):
    onehot = np.zeros((buckets.size, N_BUCKETS), np.float32)
    onehot[np.arange(buckets.size), buckets.reshape(-1)] = 1.0
    out = jnp.dot(jnp.asarray(onehot), rel_bias.astype(f32), precision=HIGHEST)
    return out.reshape(buckets.shape + (rel_bias.shape[-1],))


def _prompt_bias(rel_bias, dil):
    i = np.arange(BLK)[:, None]
    j = np.arange(2 * BLK)[None, :]
    delta = BLK + i - j
    band = (delta >= 0) & (delta <= BLK)
    tbl = _bias_lookup(rel_bias, _rel_bucket_np(delta * dil))
    tbl = jnp.where(jnp.asarray(band)[:, :, None], tbl, NEG)
    first = jnp.where(jnp.asarray(j >= BLK)[:, :, None], tbl, NEG)
    pairs = lambda t: jnp.transpose(t, (2, 0, 1)).reshape(N_PAIRS, 2 * BLK, 2 * BLK)
    return jnp.concatenate([pairs(tbl), pairs(first)], axis=0)


def _dil_prompt_kernel(q_ref, kp_ref, kc_ref, vp_ref, vc_ref, bias_ref, o_ref, lse_ref):
    lane = lax.broadcasted_iota(jnp.int32, (1, 2 * DH_D), 1)
    lo = lane < DH_D
    variant = jnp.where(pl.program_id(2) == 0, N_PAIRS, 0)
    lse_acc = jnp.zeros((BLK, 128), f32)
    lane_h = lax.broadcasted_iota(jnp.int32, (1, 128), 1)
    zero = jnp.zeros((), bf16)
    for p in range(N_PAIRS):
        cols = slice(p * 2 * DH_D, (p + 1) * 2 * DH_D)
        qp = q_ref[:, cols]
        k2 = jnp.concatenate([kp_ref[:, cols], kc_ref[:, cols]], axis=0)
        v2 = jnp.concatenate([vp_ref[:, cols], vc_ref[:, cols]], axis=0)
        halves = []
        for hh, qh in enumerate((jnp.where(lo, qp, zero), jnp.where(lo, zero, qp))):
            s = lax.dot_general(qh, k2, NT, preferred_element_type=f32) + bias_ref[variant + p, hh * BLK:(hh + 1) * BLK, :]
            m = jnp.max(s, axis=-1, keepdims=True)
            e = jnp.exp(s - m)
            den = jnp.sum(e, axis=-1, keepdims=True)
            halves.append(jnp.dot(e.astype(bf16), v2, preferred_element_type=f32) / den)
            lse_acc = jnp.where(lane_h == 2 * p + hh, m + jnp.log(den), lse_acc)
        o_ref[:, cols] = jnp.where(lo, halves[0], halves[1]).astype(o_ref.dtype)
    lse_ref[...] = lse_acc


def _dil_prompt_call(q, k, v, bias, nb, dil):
    rows = q.shape[0] // nb
    nblk = rows // BLK
    view = lambda t: t.reshape(nb, rows, t.shape[-1])
    cur = pl.BlockSpec((None, BLK, QD), lambda b, r, i: (b, i, r))
    prev = pl.BlockSpec((None, BLK, QD), lambda b, r, i: (b, jnp.maximum(i - 1, 0), r))
    o, lse = pl.pallas_call(
        _dil_prompt_kernel,
        grid=(nb, dil, nblk),
        in_specs=[cur, prev, cur, prev, cur, _resident((2 * N_PAIRS, 2 * BLK, 2 * BLK), lambda b, r, i: (0, 0, 0))],
        out_specs=[cur, pl.BlockSpec((None, BLK, 128), lambda b, r, i: (b, i, r))],
        out_shape=[jax.ShapeDtypeStruct((nb, rows, dil * QD), bf16), jax.ShapeDtypeStruct((nb, rows, dil * 128), f32)],
        compiler_params=_cparams("arbitrary", "arbitrary", "arbitrary"),
        name=f"dilated_prompt_d{dil}",
    )(view(q), view(k), view(k), view(v), view(v), bias)
    return o.reshape(nb * rows, dil * QD), lse.reshape(nb * rows, dil * 128)


def _head_expand_np():
    e = np.zeros((128, QD), np.float32)
    for h in range(H_D):
        e[h, h * DH_D:(h + 1) * DH_D] = 1.0
    return e


def _expand(w, e_ref):
    hi = w.astype(bf16)
    lo = (w - hi.astype(f32)).astype(bf16)
    return jnp.dot(hi, e_ref[...], preferred_element_type=f32) + jnp.dot(lo, e_ref[...], preferred_element_type=f32)


def _combine_kernel(*refs):
    nbr = len(DIL_CONFIGS)
    o_refs, l_refs = refs[:nbr], refs[nbr:2 * nbr]
    e_ref, out_ref = refs[2 * nbr], refs[2 * nbr + 1]
    scratch = refs[2 * nbr + 2:]
    o_planes, lses, used = [], [], 0
    for c, (_, dil) in enumerate(DIL_CONFIGS):
        if dil == 1:
            o_planes.append(None)
            lses.append(l_refs[c][...])
        else:
            op, lp = scratch[used], scratch[used + 1]
            used += 2
            _from_class_major(o_refs[c], op, dil, QD)
            _from_class_major(l_refs[c], lp, dil, 128)
            o_planes.append(op)
            lses.append(lp[0])
    m = functools.reduce(jnp.maximum, lses)
    es = [jnp.exp(l - m) for l in lses]
    z = functools.reduce(lambda a, b: a + b, es)
    ws = [_expand(e / z, e_ref) for e in es]
    for t in range(LANE_TILES):
        cols = slice(t * 128, (t + 1) * 128)
        acc = None
        for c in range(nbr):
            o = o_refs[c][:, cols].astype(f32) if o_planes[c] is None else o_planes[c][t]
            acc = ws[c][:, cols] * o if acc is None else acc + ws[c][:, cols] * o
        out_ref[:, cols] = acc.astype(out_ref.dtype)


def _combine_call(os, ls, tm):
    n = os[0].shape[0] * DIL_CONFIGS[0][1]
    row = lambda i: (i, 0)
    o_specs = [pl.BlockSpec((tm // dil, dil * QD), row) for (_, dil) in DIL_CONFIGS]
    l_specs = [pl.BlockSpec((tm // dil, dil * 128), row) for (_, dil) in DIL_CONFIGS]
    scratch = []
    for (_, dil) in DIL_CONFIGS:
        if dil > 1:
            scratch += [pltpu.VMEM((LANE_TILES, tm, 128), f32), pltpu.VMEM((1, tm, 128), f32)]
    return pl.pallas_call(
        _combine_kernel,
        grid=(n // tm,),
        in_specs=o_specs + l_specs + [pl.BlockSpec((128, QD), lambda i: (0, 0))],
        out_specs=pl.BlockSpec((tm, QD), row),
        out_shape=jax.ShapeDtypeStruct((n, QD), bf16),
        scratch_shapes=scratch,
        compiler_params=_cparams("arbitrary"),
        name="branch_combine",
    )(*os, *ls, jnp.asarray(_head_expand_np(), bf16))


SAMPLE_HEADS = 12


def _sample_bias(rel_bias, length):
    dist = length - np.arange(length)
    tbl = _bias_lookup(rel_bias, _rel_bucket_np(dist)).T
    rows = []
    for (_, dil) in DIL_CONFIGS:
        valid = (dist % dil == 0) & (dist <= BLK * dil)
        rows.append(jnp.where(jnp.asarray(valid)[None, :], tbl, NEG))
    b0 = _bias_lookup(rel_bias, _rel_bucket_np(np.zeros((1,), np.int64)))
    groups = H_D // SAMPLE_HEADS
    return (jnp.stack(rows).reshape(len(DIL_CONFIGS), groups, SAMPLE_HEADS, length),
            jnp.broadcast_to(b0.reshape(groups, SAMPLE_HEADS, 1), (groups, SAMPLE_HEADS, 128)))


def _dil_sample_kernel(qkv_ref, kt_ref, vt_ref, bias_ref, b0_ref, o_ref):
    nh = SAMPLE_HEADS
    eye = jnp.where(lax.broadcasted_iota(jnp.int32, (DH_D, DH_D), 0) == lax.broadcasted_iota(jnp.int32, (DH_D, DH_D), 1),
                    1.0, 0.0).astype(f32)
    cols = lax.dot_general(eye, qkv_ref[...], NT, precision=HIGHEST, preferred_element_type=f32)
    qc = cols[:, 0:nh] * (DH_D ** -0.5)
    lg = jnp.concatenate([jnp.sum(kt_ref[h] * qc[:, h:h + 1], axis=0, keepdims=True) for h in range(nh)], axis=0)
    l0 = jnp.concatenate([jnp.sum(qc[:, h:h + 1] * cols[:, nh + h:nh + h + 1], axis=0, keepdims=True) for h in range(nh)],
                         axis=0) + b0_ref[:, 0:1]
    ps, p0s, dens, lses = [], [], [], []
    for c in range(len(DIL_CONFIGS)):
        s = lg + bias_ref[c]
        m = jnp.maximum(jnp.max(s, axis=-1, keepdims=True), l0)
        p = jnp.exp(s - m)
        p0 = jnp.exp(l0 - m)
        den = jnp.sum(p, axis=-1, keepdims=True) + p0
        ps.append(p), p0s.append(p0), dens.append(den), lses.append(m + jnp.log(den))
    m = jnp.maximum(jnp.maximum(lses[0], lses[1]), lses[2])
    es = [jnp.exp(l - m) for l in lses]
    z = es[0] + es[1] + es[2]
    coef = [e / z / den for e, den in zip(es, dens)]
    pmix = coef[0] * ps[0] + coef[1] * ps[1] + coef[2] * ps[2]
    p0mix = coef[0] * p0s[0] + coef[1] * p0s[1] + coef[2] * p0s[2]
    lane = lax.broadcasted_iota(jnp.int32, (1, 128), 1)
    ocols = jnp.zeros((DH_D, 128), f32)
    for h in range(nh):
        oc = jnp.sum(vt_ref[h] * pmix[h:h + 1, :], axis=1, keepdims=True)
        oc = oc + p0mix[h:h + 1, 0:1] * cols[:, 2 * nh + h:2 * nh + h + 1]
        ocols = jnp.where(lane == h, oc, ocols)
    o = lax.dot_general(ocols, eye, TN, precision=HIGHEST, preferred_element_type=f32)
    o_ref[...] = o[0:nh]


def _dil_sample_call(q, kn, vn, cache_k, cache_v, layer, bias, b0):
    nb = q.shape[0]
    depth, _, length = cache_k.shape[:3]
    groups = H_D // SAMPLE_HEADS
    hd = pl.BlockSpec((None, None, SAMPLE_HEADS, DH_D), lambda b, g: (b, g, 0, 0))
    cache_spec = pl.BlockSpec((None, None, None, SAMPLE_HEADS, DH_D, length), lambda b, g: (layer, b, g, 0, 0, 0))
    view = lambda c: jnp.transpose(c, (0, 1, 3, 4, 2)).reshape(depth, nb, groups, SAMPLE_HEADS, DH_D, length)
    to_hd = lambda t: t.reshape(nb, groups, SAMPLE_HEADS, DH_D)
    qkv = jnp.concatenate([to_hd(q), to_hd(kn), to_hd(vn), jnp.zeros_like(to_hd(q))], axis=2)
    out = pl.pallas_call(
        _dil_sample_kernel,
        grid=(nb, groups),
        in_specs=[pl.BlockSpec((None, None, 4 * SAMPLE_HEADS, DH_D), lambda b, g: (b, g, 0, 0)), cache_spec, cache_spec,
                  pl.BlockSpec((len(DIL_CONFIGS), None, SAMPLE_HEADS, length), lambda b, g: (0, g, 0, 0)),
                  pl.BlockSpec((None, SAMPLE_HEADS, 128), lambda b, g: (g, 0, 0))],
        out_specs=hd,
        out_shape=jax.ShapeDtypeStruct((nb, groups, SAMPLE_HEADS, DH_D), f32),
        compiler_params=_cparams("arbitrary", "arbitrary"),
        name="dilated_sample",
    )(qkv, view(cache_k), view(cache_v), bias, b0)
    return out.reshape(nb, QD)


def _route(logits_t, rb):
    per_group = N_EXPERTS // N_EXPERT_GROUPS
    sc = jax.nn.sigmoid(logits_t)
    sel = sc + rb
    sel_r = [sel[e:e + 1, :] for e in range(N_EXPERTS)]
    sc_r = [sc[e:e + 1, :] for e in range(N_EXPERTS)]

    def beats(j, i, vals):
        return (vals[j] >= vals[i]) if j < i else (vals[j] > vals[i])

    gs = []
    for g in range(N_EXPERT_GROUPS):
        a, b, c, d = sel_r[per_group * g:per_group * (g + 1)]
        m1, n1, m2, n2 = jnp.maximum(a, b), jnp.minimum(a, b), jnp.maximum(c, d), jnp.minimum(c, d)
        gs.append(jnp.maximum(m1, m2) + jnp.maximum(jnp.minimum(m1, m2), jnp.maximum(n1, n2)))
    nums = []
    for g in range(N_EXPERT_GROUPS):
        grank = sum(beats(j, g, gs).astype(f32) for j in range(N_EXPERT_GROUPS) if j != g)
        vals = sel_r[per_group * g:per_group * (g + 1)]
        for i in range(per_group):
            rank = sum(beats(j, i, vals).astype(f32) for j in range(per_group) if j != i)
            pick = (grank == 0.0) & (rank < 2.0)
            nums.append(jnp.where(pick, sc_r[per_group * g + i], 0.0))
    den = sum(nums)
    return jnp.concatenate(nums, axis=0) / den


def _wo_kernel(x_ref, oa_ref, md_ref, wo_ref, g1_ref, ng_ref, sc_ref, sh_ref, rwt_ref, rb_ref,
               x1_ref, h_ref, gate_ref):
    y = jnp.dot(oa_ref[...].astype(bf16), wo_ref[0:VA, :], preferred_element_type=f32)
    y = y + jnp.dot(md_ref[...].astype(bf16), wo_ref[VA:, :], preferred_element_type=f32)
    x1 = x_ref[...] + g1_ref[...] * y
    x1_ref[...] = x1
    h = _rmsnorm(x1, ng_ref[...]) * (1.0 + sc_ref[...]) + sh_ref[...]
    h_ref[...] = h.astype(bf16)
    r_hi, r_lo = _split_bf16(rwt_ref[...], 2)
    h_hi, h_lo = _split_bf16(h, 2)
    logits_t = (lax.dot_general(r_hi, h_hi, NT, preferred_element_type=f32)
                + lax.dot_general(r_hi, h_lo, NT, preferred_element_type=f32)
                + lax.dot_general(r_lo, h_hi, NT, preferred_element_type=f32))
    gate_ref[...] = _route(logits_t, rb_ref[...])


def _wo_call(x2d, oa, md, wo, g1, ng, sc, sh, rwt, rb, tm, tiles_per_batch):
    n, d = x2d.shape
    row = lambda i: (i, 0)
    const = lambda shape: pl.BlockSpec(shape, lambda i: (0,) * len(shape))
    return pl.pallas_call(
        _wo_kernel,
        grid=(n // tm,),
        in_specs=[
            pl.BlockSpec((tm, d), row),
            pl.BlockSpec((tm, VA), row),
            pl.BlockSpec((tm, QD), row),
            _resident((d, d), lambda i: (0, 0)),
            _mod_spec(g1, tiles_per_batch),
            const((1, d)),
            _mod_spec(sc, tiles_per_batch),
            _mod_spec(sh, tiles_per_batch),
            const((N_EXPERTS, d)),
            const((N_EXPERTS, 1)),
        ],
        out_specs=[pl.BlockSpec((tm, d), row), pl.BlockSpec((tm, d), row), pl.BlockSpec((N_EXPERTS, tm), lambda i: (0, i))],
        out_shape=[jax.ShapeDtypeStruct((n, d), f32), jax.ShapeDtypeStruct((n, d), bf16),
                   jax.ShapeDtypeStruct((N_EXPERTS, n), f32)],
        compiler_params=_cparams("arbitrary"),
        name="wo_router",
    )(x2d, oa, md, wo, g1, ng, sc, sh, rwt, rb)


def _moe_dense_kernel(final, h_ref, gate_ref, wg_ref, wu_ref, wd_ref, x1_ref, g2_ref, fg_ref, out_ref, acc_ref):
    e = pl.program_id(1)

    @pl.when(e == 0)
    def _():
        acc_ref[...] = jnp.zeros_like(acc_ref)

    h = h_ref[...]
    hid = _silu(jnp.dot(h, wg_ref[...], preferred_element_type=f32)) * jnp.dot(h, wu_ref[...], preferred_element_type=f32)
    lane = lax.broadcasted_iota(jnp.int32, (1, N_EXPERTS), 1)
    ge = jnp.sum(jnp.where(lane == e, gate_ref[...], 0.0), axis=-1, keepdims=True)
    acc_ref[...] += jnp.dot((hid * ge).astype(bf16), wd_ref[...], preferred_element_type=f32)

    @pl.when(e == pl.num_programs(1) - 1)
    def _():
        x2 = x1_ref[...] + g2_ref[...] * acc_ref[...]
        out_ref[...] = _rmsnorm(x2, fg_ref[...]) if final else x2


def _moe_dense_call(h, gate, wg, wu, wd, x1, g2, fg, final, tm, tiles_per_batch):
    n, d = x1.shape
    ne, _, ff = wg.shape
    row = lambda i, e: (i, 0)
    return pl.pallas_call(
        functools.partial(_moe_dense_kernel, final),
        grid=(n // tm, ne),
        in_specs=[
            pl.BlockSpec((tm, d), row),
            pl.BlockSpec((tm, N_EXPERTS), row),
            pl.BlockSpec((None, d, ff), lambda i, e: (e, 0, 0)),
            pl.BlockSpec((None, d, ff), lambda i, e: (e, 0, 0)),
            pl.BlockSpec((None, ff, d), lambda i, e: (e, 0, 0)),
            pl.BlockSpec((tm, d), row),
            _mod_spec(g2, tiles_per_batch),
            pl.BlockSpec((1, d), lambda i, e: (0, 0)),
        ],
        out_specs=pl.BlockSpec((tm, d), row),
        out_shape=jax.ShapeDtypeStruct((n, d), f32),
        scratch_shapes=[pltpu.VMEM((tm, d), f32)],
        compiler_params=_cparams("arbitrary", "arbitrary"),
        name="moe_dense",
    )(h, gate, wg, wu, wd, x1, g2, fg)


MOE_TILE = 512
MOE_ALIGN = 16
MOE_WINDOW = 80
MOE_ROWS = 1344
MOE_SELECT_ROWS = 672
MOE_COMBINE_ROWS = 512
MOE_VMEM_LIMIT = 56 * 1024 * 1024
MOE_GROUP = 4


def _plan_kernel(gate_ref, slot_ref, off_ref, cnt_ref):
    t = gate_ref.shape[1]
    gate = gate_ref[...]
    sel = jnp.where(gate > 0.0, 1.0, 0.0)
    before = lax.broadcasted_iota(jnp.int32, (t, t), 0) < lax.broadcasted_iota(jnp.int32, (t, t), 1)
    rank = jnp.dot(sel.astype(bf16), jnp.where(before, 1.0, 0.0).astype(bf16), preferred_element_type=f32)
    cnt = jnp.sum(sel, axis=1, keepdims=True)
    cpad = jnp.floor((cnt + (MOE_ALIGN - 1)) * (1.0 / MOE_ALIGN)) * MOE_ALIGN
    offs, run = [], jnp.zeros((1, 1), f32)
    for e in range(N_EXPERTS):
        offs.append(run)
        run = run + cpad[e:e + 1]
    off = jnp.concatenate(offs, axis=0)
    dest = off + rank
    seen = jnp.zeros((1, t), f32)
    dest_a = dest_b = w_a = w_b = jnp.zeros((1, t), f32)
    n_a = n_b = jnp.zeros((1, t), f32)
    for e in range(N_EXPERTS):
        s_e = sel[e:e + 1]
        is_a = s_e * jnp.where(seen == 0.0, 1.0, 0.0)
        is_b = s_e * jnp.where(seen == 1.0, 1.0, 0.0)
        dest_a, w_a, n_a = dest_a + is_a * dest[e:e + 1], w_a + is_a * gate[e:e + 1], n_a + is_a
        dest_b, w_b, n_b = dest_b + is_b * dest[e:e + 1], w_b + is_b * gate[e:e + 1], n_b + is_b
        seen = seen + s_e
    dest_a = jnp.where(n_a > 0.0, dest_a, -1.0)
    dest_b = jnp.where(n_b > 0.0, dest_b, -1.0)
    slot_ref[...] = jnp.concatenate([dest_a, dest_b, w_a, w_b, jnp.zeros((4, t), f32)], axis=0)
    off_ref[...] = jnp.broadcast_to(off, (N_EXPERTS, 128))
    cnt_ref[...] = jnp.broadcast_to(cnt, (N_EXPERTS, 128))


def _plan_call(gate_t):
    n = gate_t.shape[1]
    nt = n // MOE_TILE
    slots, off, cnt = pl.pallas_call(
        _plan_kernel,
        grid=(nt,),
        in_specs=[pl.BlockSpec((N_EXPERTS, MOE_TILE), lambda i: (0, i))],
        out_specs=[pl.BlockSpec((8, MOE_TILE), lambda i: (0, i)),
                   pl.BlockSpec((None, N_EXPERTS, 128), lambda i: (i, 0, 0)),
                   pl.BlockSpec((None, N_EXPERTS, 128), lambda i: (i, 0, 0))],
        out_shape=[jax.ShapeDtypeStruct((8, n), f32), jax.ShapeDtypeStruct((nt, N_EXPERTS, 128), f32),
                   jax.ShapeDtypeStruct((nt, N_EXPERTS, 128), f32)],
        compiler_params=_cparams("arbitrary"),
        name="moe_plan",
    )(gate_t)
    to_smem = lambda a: a[:, :, 0].astype(jnp.int32).reshape(nt * N_EXPERTS)
    return slots, to_smem(off), to_smem(cnt)


def _moe_sparse_kernel(off_ref, cnt_ref, h_ref, srow_ref, scol_ref, wg_ref, wu_ref, wd_ref, y_ref, xs_ref, ys_ref):
    i, e = pl.program_id(0), pl.program_id(1)
    t = MOE_TILE
    nt = h_ref.shape[0] // t

    @pl.when(e == 0)
    def _():
        for s in range(nt):
            tok = slice(s * t, (s + 1) * t)
            dest_a, dest_b = srow_ref[0:1, tok], srow_ref[1:2, tok]
            for c in range(MOE_ROWS // MOE_SELECT_ROWS):
                r = (lax.broadcasted_iota(jnp.int32, (MOE_SELECT_ROWS, t), 0) + c * MOE_SELECT_ROWS).astype(f32)
                sel = (jnp.where(r == dest_a, 1.0, 0.0) + jnp.where(r == dest_b, 1.0, 0.0)).astype(bf16)
                rows = slice(c * MOE_SELECT_ROWS, (c + 1) * MOE_SELECT_ROWS)
                xs_ref[s, rows, :] = jnp.dot(sel, h_ref[tok, :], preferred_element_type=f32).astype(bf16)
        ys_ref[...] = jnp.zeros_like(ys_ref)

    offs = [off_ref[(i * nt + s) * N_EXPERTS + e] for s in range(nt)]
    nwin = [(cnt_ref[(i * nt + s) * N_EXPERTS + e] + MOE_WINDOW - 1) // MOE_WINDOW for s in range(nt)]

    def window(j, carry):
        starts = [pl.multiple_of(offs[s] + jnp.maximum(jnp.minimum(j, nwin[s] - 1), 0) * MOE_WINDOW, MOE_ALIGN)
                  for s in range(nt)]
        xw = jnp.concatenate([xs_ref[s, pl.ds(starts[s], MOE_WINDOW), :] for s in range(nt)], axis=0)
        hid = _silu(jnp.dot(xw, wg_ref[...], preferred_element_type=f32)) * jnp.dot(xw, wu_ref[...], preferred_element_type=f32)
        yw = jnp.dot(hid.astype(bf16), wd_ref[...], preferred_element_type=f32).astype(bf16)
        for s in range(nt):
            ys_ref[s, pl.ds(starts[s], MOE_WINDOW), :] = yw[s * MOE_WINDOW:(s + 1) * MOE_WINDOW]
        return carry

    lax.fori_loop(0, functools.reduce(jnp.maximum, nwin), window, 0)

    @pl.when(e == pl.num_programs(1) - 1)
    def _():
        for s in range(nt):
            for c in range(t // MOE_COMBINE_ROWS):
                rows = slice(s * t + c * MOE_COMBINE_ROWS, s * t + (c + 1) * MOE_COMBINE_ROWS)
                sc = scol_ref[rows, :]
                r = lax.broadcasted_iota(jnp.int32, (MOE_COMBINE_ROWS, MOE_ROWS), 1).astype(f32)
                wsel = jnp.where(r == sc[:, 0:1], sc[:, 2:3], 0.0) + jnp.where(r == sc[:, 1:2], sc[:, 3:4], 0.0)
                y_ref[rows, :] = jnp.dot(wsel.astype(bf16), ys_ref[s], preferred_element_type=f32).astype(y_ref.dtype)


def _moe_sparse_call(h, gate_t, wg, wu, wd):
    n, d = h.shape
    ne, _, ff = wg.shape
    t = MOE_TILE
    tg = MOE_GROUP * t
    assert MOE_ROWS >= 2 * t + (N_EXPERTS - 1) * (MOE_ALIGN - 1) + MOE_WINDOW and MOE_ROWS % MOE_SELECT_ROWS == 0
    slots, off, cnt = _plan_call(gate_t)
    row = lambda i, e, *_: (i, 0)
    grid_spec = pltpu.PrefetchScalarGridSpec(
        num_scalar_prefetch=2,
        grid=(n // tg, ne),
        in_specs=[
            pl.BlockSpec((tg, d), row),
            pl.BlockSpec((8, tg), lambda i, e, *_: (0, i)),
            pl.BlockSpec((tg, 8), row),
            pl.BlockSpec((None, d, ff), lambda i, e, *_: (e, 0, 0)),
            pl.BlockSpec((None, d, ff), lambda i, e, *_: (e, 0, 0)),
            pl.BlockSpec((None, ff, d), lambda i, e, *_: (e, 0, 0)),
        ],
        out_specs=pl.BlockSpec((tg, d), row),
        scratch_shapes=[pltpu.VMEM((MOE_GROUP, MOE_ROWS, d), bf16), pltpu.VMEM((MOE_GROUP, MOE_ROWS, d), bf16)],
    )
    return pl.pallas_call(
        _moe_sparse_kernel,
        grid_spec=grid_spec,
        out_shape=jax.ShapeDtypeStruct((n, d), bf16),
        compiler_params=pltpu.CompilerParams(dimension_semantics=("arbitrary", "arbitrary"),
                                             vmem_limit_bytes=MOE_VMEM_LIMIT),
        name="moe_sparse",
    )(off, cnt, h, slots, slots.T, wg, wu, wd)


def _residual_norm_kernel(x1_ref, y_ref, g2_ref, fg_ref, out_ref):
    out_ref[...] = _rmsnorm(x1_ref[...] + g2_ref[...] * y_ref[...].astype(f32), fg_ref[...])


def _residual_norm_call(x1, y, g2, fg, tm, tiles_per_batch):
    n, d = x1.shape
    row = lambda i: (i, 0)
    return pl.pallas_call(
        _residual_norm_kernel,
        grid=(n // tm,),
        in_specs=[pl.BlockSpec((tm, d), row), pl.BlockSpec((tm, d), row), _mod_spec(g2, tiles_per_batch),
                  pl.BlockSpec((1, d), lambda i: (0, 0))],
        out_specs=pl.BlockSpec((tm, d), row),
        out_shape=jax.ShapeDtypeStruct((n, d), f32),
        compiler_params=_cparams("arbitrary"),
        name="residual_norm",
    )(x1, y, g2, fg)


def _reorder_w_in(w_in):
    o = np.cumsum((0, QA, QA, VA, GATE_RANK, VA, QD, QD, QD))
    pad = jnp.zeros(w_in.shape[:-1] + (128 - GATE_RANK,), w_in.dtype)
    parts = [w_in[..., o[0]:o[3]], w_in[..., o[4]:o[5]], w_in[..., o[3]:o[4]], pad, w_in[..., o[5]:o[8]]]
    return jnp.concatenate(parts, axis=-1).astype(bf16)


def _trunk(x, mods, prompt, gla_state, cache_k, cache_v, p):
    nb, s, d = x.shape
    n = nb * s
    depth = p["w_in"].shape[0]
    sh1, sc1, g1, sh2, sc2, g2 = mods
    if prompt:
        tm = 512
        tpb = s // tm
        keep = min(2048, s)
        keep_tiles = keep // tm
    else:
        tm, tpb, keep_tiles = n, 0, 0
    x2d = x.reshape(n, d)
    new_s, new_k, new_v = [], [], []
    resid = None
    for l in range(depth):
        outs = _inproj_call(x2d, sc1[l], sh1[l], p["norm1_g"][l], p["w_in"][l], tm, tpb, keep_tiles, resid)
        if resid is not None:
            x2d, outs = outs[0], outs[1:]
        a, q, k, v = outs[:4]
        if prompt:
            oa, st = _gla_prompt_call(a, p["wg"][l], p["bg"][l], p["gn"][l], nb, tm)
            new_s.append(_state_from_blockdiag(st))
            new_k.append(outs[4])
            new_v.append(outs[5])
            qkv = {1: (q, k, v)}
            for j, dil in enumerate(STRIDED_DILS):
                qkv[dil] = outs[6 + 3 * j:9 + 3 * j]
            os, ls = zip(*[_dil_prompt_call(*qkv[dil], p["bias_p"][c], nb, dil) for c, (_, dil) in enumerate(DIL_CONFIGS)])
            md = _combine_call(os, ls, tm)
        else:
            oa, st = _gla_sample_call(a, p["wg"][l], p["bg"][l], p["gn"][l], gla_state[l])
            oa = oa.reshape(n, VA)
            new_s.append(st.reshape(nb, H_A, DK_A, DV_A))
            new_k.append(k.reshape(nb, s, H_D, DH_D))
            new_v.append(v.reshape(nb, s, H_D, DH_D))
            md = _dil_sample_call(q, k, v, cache_k, cache_v, l, *p["bias_s"])
        x1, h2, gate_t = _wo_call(x2d, oa, md, p["w_o"][l], g1[l], p["norm2_g"][l], sc2[l], sh2[l],
                                  p["rwt"], p["rb"], tm, tpb)
        experts = (p["moe_wg"][l], p["moe_wu"][l], p["moe_wd"][l])
        if prompt:
            x2d, resid = x1, (_moe_sparse_call(h2, gate_t, *experts), g2[l])
        else:
            x2d = _moe_dense_call(h2, gate_t.T, *experts, x1, g2[l], p["final_g"], l == depth - 1, tm, tpb)
    new_k, new_v = jnp.stack(new_k), jnp.stack(new_v)
    if prompt:
        x2d = _residual_norm_call(x2d, *resid, p["final_g"], tm, tpb)
        new_k = new_k[:, :, 1:].reshape(depth, nb, keep, H_D, DH_D)
        new_v = new_v[:, :, 1:].reshape(depth, nb, keep, H_D, DH_D)
    return x2d.reshape(nb, s, d), jnp.stack(new_s), new_k, new_v


def kernel(x_prompt, x_sample, state_gla, cache_win_k, cache_win_v, c_prompt, c_sample, w_ada, b_ada, norm1_g, norm2_g, w_in, w_gate_up, b_gate, gla_norm_g, w_o, rel_bias, router_w, router_b, moe_w_gate, moe_w_up, moe_w_down, final_norm_g):
    nbp, seq, d = x_prompt.shape
    nbs, dec_seq, _ = x_sample.shape
    depth = w_in.shape[0]
    assert dec_seq == 1 and cache_win_k.shape[2] == DIL_CONFIGS[-1][0], "sample path: one new token over a full window"
    assert seq % DIL_CONFIGS[-1][0] == 0, "prompt length must be a whole number of the widest window"

    c_all = jnp.concatenate([c_prompt, c_sample], axis=0)
    c_all = jnp.pad(c_all, ((0, -c_all.shape[0] % 8), (0, 0)))
    mod = _ada_call(c_all, w_ada, b_ada)
    mods_p = [m[:, :nbp, None, :] for m in jnp.split(mod, 6, axis=-1)]
    mods_s = [m[:, None, nbp:nbp + nbs, :] for m in jnp.split(mod, 6, axis=-1)]

    p = {
        "w_in": _reorder_w_in(w_in),
        "norm1_g": norm1_g.reshape(depth, 1, d),
        "norm2_g": norm2_g.reshape(depth, 1, d),
        "wg": jnp.pad(w_gate_up, ((0, 0), (0, 128 - GATE_RANK), (0, 0))),
        "bg": b_gate.reshape(depth, 1, QA),
        "gn": gla_norm_g.reshape(depth, 1, VA),
        "w_o": w_o.astype(bf16),
        "rwt": router_w.T,
        "rb": router_b.reshape(N_EXPERTS, 1),
        "moe_wg": moe_w_gate.astype(bf16),
        "moe_wu": moe_w_up.astype(bf16),
        "moe_wd": moe_w_down.astype(bf16),
        "final_g": final_norm_g.reshape(1, d),
        "bias_p": [_prompt_bias(rel_bias, dil) for (_, dil) in DIL_CONFIGS],
        "bias_s": _sample_bias(rel_bias, cache_win_k.shape[2]),
    }
    y_p, s_p, k_p, v_p = _trunk(x_prompt, mods_p, True, None, None, None, p)
    y_s, s_s, k_s, v_s = _trunk(x_sample, mods_s, False, state_gla, cache_win_k, cache_win_v, p)
    return (y_p, y_s, s_p, k_p, v_p, s_s, k_s, v_s)
```

```python
import functools
import math

import jax
import jax.numpy as jnp
import numpy as np
from jax import lax
from jax.experimental import pallas as pl
from jax.experimental.pallas import tpu as pltpu

f32 = jnp.float32
bf16 = jnp.bfloat16

H_A, DK_A, DV_A = 4, 32, 64
GATE_RANK = 16
GATE_TAU = 16.0
H_D, DH_D = 12, 64
DIL_CONFIGS = ((128, 1), (512, 4), (2048, 16))
N_BUCKETS = 32
MAX_DISTANCE = 2048
N_EXPERTS = 16
N_EXPERT_GROUPS = 4
EPS = 1e-6

QA = H_A * DK_A
VA = H_A * DV_A
QD = H_D * DH_D
A_WIDTH = 2 * QA + 2 * VA + 128
W_IN_COLS = A_WIDTH + 3 * QD
BLK = 128
N_PAIRS = H_D // 2
ATTN_QUERY_BLOCKS = 4
NEG = -1e30
GLA_SUB = 16
GLA_ROWS = 128
EXP_CLAMP = 80.0
VMEM_LIMIT = 48 * 1024 * 1024

HIGHEST = lax.Precision.HIGHEST
NT = (((1,), (1,)), ((), ()))
TN = (((0,), (0,)), ((), ()))


def _cparams(*sem):
    return pltpu.CompilerParams(dimension_semantics=sem, vmem_limit_bytes=VMEM_LIMIT)


def _resident(shape, index_map):
    return pl.BlockSpec(shape, index_map, pipeline_mode=pl.Buffered(1))


def _silu(x):
    return x * jax.nn.sigmoid(x)


def _split_bf16(x, terms):
    parts = []
    for _ in range(terms):
        p = x.astype(bf16)
        parts.append(p)
        x = x - p.astype(f32)
    return parts


def _rmsnorm(x, g):
    return x * lax.rsqrt(jnp.mean(x * x, axis=-1, keepdims=True) + EPS) * g


def _ada_kernel(c_ref, w_ref, b_ref, o_ref):
    cs = _silu(c_ref[...]).astype(bf16)
    o_ref[...] = jnp.dot(cs, w_ref[...].astype(bf16), preferred_element_type=f32) + b_ref[...]


def _ada_call(c, w_ada, b_ada):
    depth, d, n6 = w_ada.shape
    m = c.shape[0]
    tn = 1536
    return pl.pallas_call(
        _ada_kernel,
        grid=(depth, n6 // tn),
        in_specs=[
            pl.BlockSpec((m, d), lambda l, j: (0, 0)),
            pl.BlockSpec((None, d, tn), lambda l, j: (l, 0, j)),
            pl.BlockSpec((None, 1, tn), lambda l, j: (l, 0, j)),
        ],
        out_specs=pl.BlockSpec((None, m, tn), lambda l, j: (l, 0, j)),
        out_shape=jax.ShapeDtypeStruct((depth, m, n6), f32),
        compiler_params=_cparams("arbitrary", "arbitrary"),
        name="adaln",
    )(c, w_ada, b_ada.reshape(depth, 1, n6))


def _inproj_kernel(prompt, resid, x_ref, *refs):
    x = x_ref[...]
    if resid:
        y_ref, g2_ref, x2_ref = refs[0], refs[1], refs[6]
        x = x + g2_ref[...] * y_ref[...].astype(f32)
        x2_ref[...] = x
        refs = refs[2:6] + refs[7:]
    sc_ref, sh_ref, g_ref, w_ref = refs[:4]
    outs = refs[4:]
    h = _rmsnorm(x, g_ref[...])
    h = (h * (1.0 + sc_ref[...]) + sh_ref[...]).astype(bf16)

    def proj(lo, hi):
        return jnp.dot(h, w_ref[:, lo:hi], preferred_element_type=f32)

    a_ref, q_ref, k_ref, v_ref = outs[:4]
    a_ref[...] = proj(0, A_WIDTH)
    q = proj(A_WIDTH, A_WIDTH + QD)
    k = proj(A_WIDTH + QD, A_WIDTH + 2 * QD)
    v = proj(A_WIDTH + 2 * QD, A_WIDTH + 3 * QD)
    if prompt:
        n_cls = len(STRIDED_DILS)
        kw_ref, vw_ref = outs[4:6]
        cls_refs = outs[6:6 + 3 * n_cls]
        scratch = outs[6 + 3 * n_cls:]
        q = q * (DH_D ** -0.5)
        kw_ref[...] = k
        vw_ref[...] = v
        for j, (val, nat_ref) in enumerate(((q, q_ref), (k, k_ref), (v, v_ref))):
            nat_ref[...] = val.astype(bf16)
            _to_class_major(val, scratch[j], [cls_refs[3 * c + j] for c in range(n_cls)])
    else:
        q_ref[...] = q
        k_ref[...] = k
        v_ref[...] = v


STRIDED_DILS = tuple(dil for (_, dil) in DIL_CONFIGS if dil > 1)
LANE_TILES = QD // 128


def _to_class_major(val, planes, dst_refs):
    rows = val.shape[0]
    for c in range(LANE_TILES):
        planes[c] = val[:, c * 128:(c + 1) * 128]
    for dil, dst in zip(STRIDED_DILS, dst_refs):
        for c in range(LANE_TILES):
            for r in range(dil):
                lo = r * QD + c * 128
                dst[:, lo:lo + 128] = planes[c, pl.ds(r, rows // dil, stride=dil), :].astype(dst.dtype)


def _from_class_major(src_ref, planes, dil, width):
    rows = planes.shape[1]
    for c in range(width // 128):
        for r in range(dil):
            lo = r * width + c * 128
            planes[c, pl.ds(r, rows // dil, stride=dil), :] = src_ref[:, lo:lo + 128].astype(f32)


def _mod_spec(arr, tiles_per_batch):
    rows, d = arr.shape[1], arr.shape[2]
    if tiles_per_batch:
        return pl.BlockSpec((None, rows, d), lambda i, *_: (i // tiles_per_batch, 0, 0))
    return pl.BlockSpec((None, rows, d), lambda i, *_: (0, 0, 0))


def _inproj_call(x2d, sc, sh, g, w, tm, tiles_per_batch, keep_tiles, resid=None):
    n, d = x2d.shape
    prompt = bool(tiles_per_batch)
    qkv_dt = bf16 if prompt else f32
    row = lambda i: (i, 0)
    out_specs = [pl.BlockSpec((tm, A_WIDTH), row)] + [pl.BlockSpec((tm, QD), row)] * 3
    out_shape = [jax.ShapeDtypeStruct((n, A_WIDTH), f32)] + [jax.ShapeDtypeStruct((n, QD), qkv_dt)] * 3
    scratch = []
    if prompt:
        nb = n // (tm * tiles_per_batch)
        first_keep = tiles_per_batch - keep_tiles

        def win(i):
            return (i // tiles_per_batch, jnp.maximum(i % tiles_per_batch - first_keep + 1, 0), 0, 0)

        out_specs += [pl.BlockSpec((None, None, tm, QD), win)] * 2
        out_shape += [jax.ShapeDtypeStruct((nb, keep_tiles + 1, tm, QD), f32)] * 2
        for dil in STRIDED_DILS:
            out_specs += [pl.BlockSpec((tm // dil, dil * QD), row)] * 3
            out_shape += [jax.ShapeDtypeStruct((n // dil, dil * QD), bf16)] * 3
        scratch = [pltpu.VMEM((LANE_TILES, tm, 128), f32)] * 3
    in_specs = [
        _mod_spec(sc, tiles_per_batch),
        _mod_spec(sh, tiles_per_batch),
        pl.BlockSpec((1, d), lambda i: (0, 0)),
        _resident((d, W_IN_COLS), lambda i: (0, 0)),
    ]
    args = (sc, sh, g, w)
    if resid is not None:
        in_specs = [pl.BlockSpec((tm, d), row), _mod_spec(resid[1], tiles_per_batch)] + in_specs
        args = resid + args
        out_specs = [pl.BlockSpec((tm, d), row)] + out_specs
        out_shape = [jax.ShapeDtypeStruct((n, d), f32)] + out_shape
    return pl.pallas_call(
        functools.partial(_inproj_kernel, prompt, resid is not None),
        grid=(n // tm,),
        in_specs=[pl.BlockSpec((tm, d), row)] + in_specs,
        out_specs=out_specs,
        out_shape=out_shape,
        scratch_shapes=scratch,
        compiler_params=_cparams("arbitrary"),
        name="inproj_prompt" if prompt else "inproj_sample",
    )(x2d, *args)


def _log_gate(glr, wg_ref, bg_ref):
    x = jnp.dot(glr.astype(bf16), wg_ref[...].astype(bf16), preferred_element_type=f32) + bg_ref[...]
    return (jnp.minimum(x, 0.0) - jnp.log1p(jnp.exp(-jnp.abs(x)))) * (1.0 / GATE_TAU)


def _gla_prompt_kernel(a_ref, wg_ref, bg_ref, gn_ref, o_ref, s_ref, st_ref):
    R, C = GLA_ROWS, GLA_SUB

    @pl.when(pl.program_id(1) == 0)
    def _():
        st_ref[...] = jnp.zeros_like(st_ref)

    def iota(shape, dim, shift=0):
        return lax.shift_right_logical(lax.broadcasted_iota(jnp.int32, shape, dim), shift)

    sub, lk, lv = int(math.log2(C)), int(math.log2(DK_A)), int(math.log2(DV_A))
    ri = lax.bitwise_and(iota((H_A * R, R), 0), R - 1)
    ci = iota((H_A * R, R), 1)
    same4 = lax.shift_right_logical(ri, sub) == lax.shift_right_logical(ci, sub)
    causal4 = same4 & (ci <= ri)
    sums = jnp.concatenate([jnp.where(causal4, 1.0, 0.0)[0:R], jnp.where(same4, 1.0, 0.0)[0:R]], axis=0).astype(bf16)
    lane_q = iota((1, QA), 1, lk)
    lane_v = iota((1, VA), 1, lv)
    bd = iota((QA, VA), 0, lk) == iota((QA, VA), 1, lv)
    hmean = jnp.where(iota((VA, VA), 0, lv) == iota((VA, VA), 1, lv), 1.0 / DV_A, 0.0).astype(bf16)
    row_sub = iota((R, 1), 0, sub)
    sub_mean = jnp.where(iota((R, 128), 0, sub) == iota((R, 128), 1), 1.0 / C, 0.0).astype(bf16)
    nsub = R // C

    def prepare(c):
        rows = slice(c * R, (c + 1) * R)
        blk = a_ref[rows, :]
        q = blk[:, 0:QA] * (DK_A ** -0.5)
        k = blk[:, QA:2 * QA]
        v = blk[:, 2 * QA:2 * QA + VA]
        r = blk[:, 2 * QA + VA:2 * QA + 2 * VA]
        g = _log_gate(blk[:, 2 * QA + 2 * VA:], wg_ref, bg_ref)
        cums = sum(jnp.dot(sums, part, preferred_element_type=f32) for part in _split_bf16(g, 3))
        b, bt = cums[0:R], cums[R:2 * R]
        qb = q * jnp.exp(b)
        kd = (k * jnp.exp(bt - b)).astype(bf16)
        ki = (k * jnp.exp(jnp.minimum(-b, EXP_CLAMP))).astype(bf16)
        vb = v.astype(bf16)
        qs = jnp.concatenate([jnp.where(lane_q == h, qb, 0.0) for h in range(H_A)], axis=0).astype(bf16)
        att = lax.dot_general(qs, ki, NT, preferred_element_type=f32)
        att = jnp.where(causal4, att, 0.0).astype(bf16)
        res = jnp.dot(att, vb, preferred_element_type=f32)
        o = jnp.where(lane_v == 0, res[0:R], 0.0)
        for h in range(1, H_A):
            o = o + jnp.where(lane_v == h, res[h * R:(h + 1) * R], 0.0)
        qbb = qb.astype(bf16)
        zero = jnp.zeros((), bf16)
        v_exp = jnp.concatenate([jnp.where(row_sub == i, vb, zero) for i in range(nsub)], axis=1)
        kv_all = lax.dot_general(kd, v_exp, TN, preferred_element_type=f32)
        decay = sum(lax.dot_general(part, sub_mean, TN, preferred_element_type=f32)
                    for part in _split_bf16(jnp.exp(bt), 3))
        q_exp = jnp.concatenate([jnp.where(row_sub == i, qbb, zero) for i in range(nsub)], axis=1)
        return o, kv_all, decay, q_exp, _silu(r)

    def finish(c, st, o, kv_all, decay, q_exp, gate):
        states = []
        for i in range(nsub):
            states.append(st.astype(bf16))
            st = st * decay[:, i:i + 1] + jnp.where(bd, kv_all[:, i * VA:(i + 1) * VA], 0.0)
        o = o + jnp.dot(q_exp, jnp.concatenate(states, axis=0), preferred_element_type=f32)
        ms = sum(jnp.dot(part, hmean, preferred_element_type=f32) for part in _split_bf16(o * o, 2))
        y = o * lax.rsqrt(ms + EPS) * gn_ref[...]
        o_ref[c * R:(c + 1) * R, :] = (y * gate).astype(o_ref.dtype)
        return st

    nblk = a_ref.shape[0] // R
    prepared = [prepare(c) for c in range(nblk)]
    st = st_ref[...]
    for c in range(nblk):
        st = finish(c, st, *prepared[c])
    st_ref[...] = st
    s_ref[...] = st


def _gla_prompt_call(a, wg, bg, gn, nb, tm):
    n = a.shape[0]
    tpb = n // nb // tm
    return pl.pallas_call(
        _gla_prompt_kernel,
        grid=(nb, tpb),
        in_specs=[
            pl.BlockSpec((tm, A_WIDTH), lambda b, j: (b * tpb + j, 0)),
            pl.BlockSpec((128, QA), lambda b, j: (0, 0)),
            pl.BlockSpec((1, QA), lambda b, j: (0, 0)),
            pl.BlockSpec((1, VA), lambda b, j: (0, 0)),
        ],
        out_specs=[
            pl.BlockSpec((tm, VA), lambda b, j: (b * tpb + j, 0)),
            pl.BlockSpec((None, QA, VA), lambda b, j: (b, 0, 0)),
        ],
        out_shape=[jax.ShapeDtypeStruct((n, VA), bf16), jax.ShapeDtypeStruct((nb, QA, VA), f32)],
        scratch_shapes=[pltpu.VMEM((QA, VA), f32)],
        compiler_params=_cparams("arbitrary", "arbitrary"),
        name="gla_prompt",
    )(a, wg, bg, gn)


def _state_from_blockdiag(st):
    b = st.shape[0]
    st5 = st.reshape(b, H_A, DK_A, H_A, DV_A)
    return jnp.stack([st5[:, h, :, h, :] for h in range(H_A)], axis=1)


def _gla_sample_kernel(a_ref, at_ref, wgt_ref, bgt_ref, gn_ref, s_ref, o_ref, so_ref):
    nb = a_ref.shape[0]
    qt = at_ref[0:QA, :] * (DK_A ** -0.5)
    kt = at_ref[QA:2 * QA, :]
    xt = jnp.dot(wgt_ref[...].astype(bf16), at_ref[2 * QA + 2 * VA:, :].astype(bf16), preferred_element_type=f32) + bgt_ref[...]
    et = jnp.exp((jnp.minimum(xt, 0.0) - jnp.log1p(jnp.exp(-jnp.abs(xt)))) * (1.0 / GATE_TAU))
    gn = gn_ref[...]
    for t in range(nb):
        row = a_ref[t:t + 1, :]
        vt = jnp.concatenate(
            [jnp.broadcast_to(row[:, 2 * QA + h * DV_A:2 * QA + (h + 1) * DV_A], (DK_A, DV_A)) for h in range(H_A)], axis=0)
        s_new = s_ref[t] * et[:, t:t + 1] + kt[:, t:t + 1] * vt
        so_ref[t] = s_new
        o = jnp.sum((qt[:, t:t + 1] * s_new).reshape(H_A, DK_A, DV_A), axis=1)
        y = _rmsnorm(o, gn)
        r = jnp.concatenate([row[:, 2 * QA + VA + h * DV_A:2 * QA + VA + (h + 1) * DV_A] for h in range(H_A)], axis=0)
        o_ref[t] = y * _silu(r)


def _gla_sample_call(a, wg, bg, gn, state):
    nb = a.shape[0]
    return pl.pallas_call(
        _gla_sample_kernel,
        out_shape=[jax.ShapeDtypeStruct((nb, H_A, DV_A), f32), jax.ShapeDtypeStruct((nb, QA, DV_A), f32)],
        compiler_params=pltpu.CompilerParams(vmem_limit_bytes=VMEM_LIMIT),
        name="gla_sample",
    )(a, a.T, wg.T, bg.reshape(QA, 1), gn.reshape(H_A, DV_A), state.reshape(nb, QA, DV_A))


def _rel_bucket_np(dist):
    max_exact = N_BUCKETS // 2
    d = np.maximum(dist, 0)
    df = np.maximum(d, 1).astype(np.float32)
    large = max_exact + (np.log(df / np.float32(max_exact)) / np.float32(math.log(MAX_DISTANCE / max_exact))
                         * np.float32(N_BUCKETS - max_exact)).astype(np.int32)
    large = np.minimum(large, N_BUCKETS - 1)
    return np.where(d < max_exact, d, large).astype(np.int32)


def _bias_lookup(rel_bias, buckets):
    onehot = np.zeros((buckets.size, N_BUCKETS), np.float32)
    onehot[np.arange(buckets.size), buckets.reshape(-1)] = 1.0
    out = jnp.dot(jnp.asarray(onehot), rel_bias.astype(f32), precision=HIGHEST)
    return out.reshape(buckets.shape + (rel_bias.shape[-1],))


def _prompt_bias(rel_bias, dil):
    i = np.arange(BLK)[:, None]
    j = np.arange(2 * BLK)[None, :]
    delta = BLK + i - j
    band = (delta >= 0) & (delta <= BLK)
    tbl = _bias_lookup(rel_bias, _rel_bucket_np(delta * dil))
    tbl = jnp.where(jnp.asarray(band)[:, :, None], tbl, NEG)
    first = jnp.where(jnp.asarray(j >= BLK)[:, :, None], tbl, NEG)
    pairs = lambda t: jnp.transpose(t, (2, 0, 1)).reshape(N_PAIRS, 2 * BLK, 2 * BLK)
    return jnp.concatenate([pairs(tbl), pairs(first)], axis=0)


def _dil_prompt_kernel(q_ref, kp_ref, kc_ref, vp_ref, vc_ref, bias_ref, o_ref, lse_ref):
    lane = lax.broadcasted_iota(jnp.int32, (1, 2 * DH_D), 1)
    lo = lane < DH_D
    first = jnp.where(pl.program_id(2) == 0, N_PAIRS, 0)
    lane_h = lax.broadcasted_iota(jnp.int32, (1, 128), 1)
    zero = jnp.zeros((), bf16)
    nq = q_ref.shape[0] // BLK
    lse_acc = [jnp.zeros((BLK, 128), f32) for _ in range(nq)]
    for p in range(N_PAIRS):
        cols = slice(p * 2 * DH_D, (p + 1) * 2 * DH_D)
        k_all = jnp.concatenate([kp_ref[:, cols], kc_ref[:, cols]], axis=0)
        v_all = jnp.concatenate([vp_ref[:, cols], vc_ref[:, cols]], axis=0)
        for qb in range(nq):
            rows = slice(qb * BLK, (qb + 1) * BLK)
            qp = q_ref[rows, cols]
            k2, v2 = k_all[qb * BLK:(qb + 2) * BLK], v_all[qb * BLK:(qb + 2) * BLK]
            variant = first if qb == 0 else 0
            halves = []
            for hh, qh in enumerate((jnp.where(lo, qp, zero), jnp.where(lo, zero, qp))):
                s = lax.dot_general(qh, k2, NT, preferred_element_type=f32) + bias_ref[variant + p, hh * BLK:(hh + 1) * BLK, :]
                m = jnp.max(s, axis=-1, keepdims=True)
                e = jnp.exp(s - m)
                den = jnp.sum(e, axis=-1, keepdims=True)
                halves.append(jnp.dot(e.astype(bf16), v2, preferred_element_type=f32) / den)
                lse_acc[qb] = jnp.where(lane_h == 2 * p + hh, m + jnp.log(den), lse_acc[qb])
            o_ref[rows, cols] = jnp.where(lo, halves[0], halves[1]).astype(o_ref.dtype)
    for qb in range(nq):
        lse_ref[qb * BLK:(qb + 1) * BLK, :] = lse_acc[qb]


def _dil_prompt_call(q, k, v, bias, nb, dil):
    rows = q.shape[0] // nb
    nq = ATTN_QUERY_BLOCKS
    nblk = rows // (nq * BLK)
    view = lambda t: t.reshape(nb, rows, t.shape[-1])
    cur = pl.BlockSpec((None, nq * BLK, QD), lambda b, r, i: (b, i, r))
    prev = pl.BlockSpec((None, BLK, QD), lambda b, r, i: (b, jnp.maximum(nq * i - 1, 0), r))
    o, lse = pl.pallas_call(
        _dil_prompt_kernel,
        grid=(nb, dil, nblk),
        in_specs=[cur, prev, cur, prev, cur, _resident((2 * N_PAIRS, 2 * BLK, 2 * BLK), lambda b, r, i: (0, 0, 0))],
        out_specs=[cur, pl.BlockSpec((None, nq * BLK, 128), lambda b, r, i: (b, i, r))],
        out_shape=[jax.ShapeDtypeStruct((nb, rows, dil * QD), bf16), jax.ShapeDtypeStruct((nb, rows, dil * 128), f32)],
        compiler_params=_cparams("arbitrary", "arbitrary", "arbitrary"),
        name=f"dilated_prompt_d{dil}",
    )(view(q), view(k), view(k), view(v), view(v), bias)
    return o.reshape(nb * rows, dil * QD), lse.reshape(nb * rows, dil * 128)


def _head_expand_np():
    e = np.zeros((128, QD), np.float32)
    for h in range(H_D):
        e[h, h * DH_D:(h + 1) * DH_D] = 1.0
    return e


def _expand(w, e_ref):
    hi = w.astype(bf16)
    lo = (w - hi.astype(f32)).astype(bf16)
    return jnp.dot(hi, e_ref[...], preferred_element_type=f32) + jnp.dot(lo, e_ref[...], preferred_element_type=f32)


def _combine_kernel(*refs):
    nbr = len(DIL_CONFIGS)
    o_refs, l_refs = refs[:nbr], refs[nbr:2 * nbr]
    e_ref, out_ref = refs[2 * nbr], refs[2 * nbr + 1]
    scratch = refs[2 * nbr + 2:]
    o_planes, lses, used = [], [], 0
    for c, (_, dil) in enumerate(DIL_CONFIGS):
        if dil == 1:
            o_planes.append(None)
            lses.append(l_refs[c][...])
        else:
            op, lp = scratch[used], scratch[used + 1]
            used += 2
            _from_class_major(o_refs[c], op, dil, QD)
            _from_class_major(l_refs[c], lp, dil, 128)
            o_planes.append(op)
            lses.append(lp[0])
    m = functools.reduce(jnp.maximum, lses)
    es = [jnp.exp(l - m) for l in lses]
    z = functools.reduce(lambda a, b: a + b, es)
    ws = [_expand(e / z, e_ref) for e in es]
    for t in range(LANE_TILES):
        cols = slice(t * 128, (t + 1) * 128)
        acc = None
        for c in range(nbr):
            o = o_refs[c][:, cols].astype(f32) if o_planes[c] is None else o_planes[c][t]
            acc = ws[c][:, cols] * o if acc is None else acc + ws[c][:, cols] * o
        out_ref[:, cols] = acc.astype(out_ref.dtype)


def _combine_call(os, ls, tm):
    n = os[0].shape[0] * DIL_CONFIGS[0][1]
    row = lambda i: (i, 0)
    o_specs = [pl.BlockSpec((tm // dil, dil * QD), row) for (_, dil) in DIL_CONFIGS]
    l_specs = [pl.BlockSpec((tm // dil, dil * 128), row) for (_, dil) in DIL_CONFIGS]
    scratch = []
    for (_, dil) in DIL_CONFIGS:
        if dil > 1:
            scratch += [pltpu.VMEM((LANE_TILES, tm, 128), f32), pltpu.VMEM((1, tm, 128), f32)]
    return pl.pallas_call(
        _combine_kernel,
        grid=(n // tm,),
        in_specs=o_specs + l_specs + [pl.BlockSpec((128, QD), lambda i: (0, 0))],
        out_specs=pl.BlockSpec((tm, QD), row),
        out_shape=jax.ShapeDtypeStruct((n, QD), bf16),
        scratch_shapes=scratch,
        compiler_params=_cparams("arbitrary"),
        name="branch_combine",
    )(*os, *ls, jnp.asarray(_head_expand_np(), bf16))


SAMPLE_HEADS = 12


def _sample_bias(rel_bias, length):
    dist = length - np.arange(length)
    tbl = _bias_lookup(rel_bias, _rel_bucket_np(dist)).T
    rows = []
    for (_, dil) in DIL_CONFIGS:
        valid = (dist % dil == 0) & (dist <= BLK * dil)
        rows.append(jnp.where(jnp.asarray(valid)[None, :], tbl, NEG))
    b0 = _bias_lookup(rel_bias, _rel_bucket_np(np.zeros((1,), np.int64)))
    groups = H_D // SAMPLE_HEADS
    return (jnp.stack(rows).reshape(len(DIL_CONFIGS), groups, SAMPLE_HEADS, length),
            jnp.broadcast_to(b0.reshape(groups, SAMPLE_HEADS, 1), (groups, SAMPLE_HEADS, 128)))


def _dil_sample_kernel(qkv_ref, kt_ref, vt_ref, bias_ref, b0_ref, o_ref):
    nh = SAMPLE_HEADS
    eye = jnp.where(lax.broadcasted_iota(jnp.int32, (DH_D, DH_D), 0) == lax.broadcasted_iota(jnp.int32, (DH_D, DH_D), 1),
                    1.0, 0.0).astype(f32)
    cols = lax.dot_general(eye, qkv_ref[...], NT, precision=HIGHEST, preferred_element_type=f32)
    qc = cols[:, 0:nh] * (DH_D ** -0.5)
    lg = jnp.concatenate([jnp.sum(kt_ref[h] * qc[:, h:h + 1], axis=0, keepdims=True) for h in range(nh)], axis=0)
    l0 = jnp.concatenate([jnp.sum(qc[:, h:h + 1] * cols[:, nh + h:nh + h + 1], axis=0, keepdims=True) for h in range(nh)],
                         axis=0) + b0_ref[:, 0:1]
    ps, p0s, dens, lses = [], [], [], []
    for c in range(len(DIL_CONFIGS)):
        s = lg + bias_ref[c]
        m = jnp.maximum(jnp.max(s, axis=-1, keepdims=True), l0)
        p = jnp.exp(s - m)
        p0 = jnp.exp(l0 - m)
        den = jnp.sum(p, axis=-1, keepdims=True) + p0
        ps.append(p), p0s.append(p0), dens.append(den), lses.append(m + jnp.log(den))
    m = jnp.maximum(jnp.maximum(lses[0], lses[1]), lses[2])
    es = [jnp.exp(l - m) for l in lses]
    z = es[0] + es[1] + es[2]
    coef = [e / z / den for e, den in zip(es, dens)]
    pmix = coef[0] * ps[0] + coef[1] * ps[1] + coef[2] * ps[2]
    p0mix = coef[0] * p0s[0] + coef[1] * p0s[1] + coef[2] * p0s[2]
    lane = lax.broadcasted_iota(jnp.int32, (1, 128), 1)
    ocols = jnp.zeros((DH_D, 128), f32)
    for h in range(nh):
        oc = jnp.sum(vt_ref[h] * pmix[h:h + 1, :], axis=1, keepdims=True)
        oc = oc + p0mix[h:h + 1, 0:1] * cols[:, 2 * nh + h:2 * nh + h + 1]
        ocols = jnp.where(lane == h, oc, ocols)
    o = lax.dot_general(ocols, eye, TN, precision=HIGHEST, preferred_element_type=f32)
    o_ref[...] = o[0:nh]


def _dil_sample_call(q, kn, vn, cache_k, cache_v, layer, bias, b0):
    nb = q.shape[0]
    depth, _, length = cache_k.shape[:3]
    groups = H_D // SAMPLE_HEADS
    hd = pl.BlockSpec((None, None, SAMPLE_HEADS, DH_D), lambda b, g: (b, g, 0, 0))
    cache_spec = pl.BlockSpec((None, None, None, SAMPLE_HEADS, DH_D, length), lambda b, g: (layer, b, g, 0, 0, 0))
    view = lambda c: jnp.transpose(c, (0, 1, 3, 4, 2)).reshape(depth, nb, groups, SAMPLE_HEADS, DH_D, length)
    to_hd = lambda t: t.reshape(nb, groups, SAMPLE_HEADS, DH_D)
    qkv = jnp.concatenate([to_hd(q), to_hd(kn), to_hd(vn), jnp.zeros_like(to_hd(q))], axis=2)
    out = pl.pallas_call(
        _dil_sample_kernel,
        grid=(nb, groups),
        in_specs=[pl.BlockSpec((None, None, 4 * SAMPLE_HEADS, DH_D), lambda b, g: (b, g, 0, 0)), cache_spec, cache_spec,
                  pl.BlockSpec((len(DIL_CONFIGS), None, SAMPLE_HEADS, length), lambda b, g: (0, g, 0, 0)),
                  pl.BlockSpec((None, SAMPLE_HEADS, 128), lambda b, g: (g, 0, 0))],
        out_specs=hd,
        out_shape=jax.ShapeDtypeStruct((nb, groups, SAMPLE_HEADS, DH_D), f32),
        compiler_params=_cparams("arbitrary", "arbitrary"),
        name="dilated_sample",
    )(qkv, view(cache_k), view(cache_v), bias, b0)
    return out.reshape(nb, QD)


def _route(logits_t, rb):
    per_group = N_EXPERTS // N_EXPERT_GROUPS
    sc = jax.nn.sigmoid(logits_t)
    sel = sc + rb
    sel_r = [sel[e:e + 1, :] for e in range(N_EXPERTS)]
    sc_r = [sc[e:e + 1, :] for e in range(N_EXPERTS)]

    def beats(j, i, vals):
        return (vals[j] >= vals[i]) if j < i else (vals[j] > vals[i])

    gs = []
    for g in range(N_EXPERT_GROUPS):
        a, b, c, d = sel_r[per_group * g:per_group * (g + 1)]
        m1, n1, m2, n2 = jnp.maximum(a, b), jnp.minimum(a, b), jnp.maximum(c, d), jnp.minimum(c, d)
        gs.append(jnp.maximum(m1, m2) + jnp.maximum(jnp.minimum(m1, m2), jnp.maximum(n1, n2)))
    nums = []
    for g in range(N_EXPERT_GROUPS):
        grank = sum(beats(j, g, gs).astype(f32) for j in range(N_EXPERT_GROUPS) if j != g)
        vals = sel_r[per_group * g:per_group * (g + 1)]
        for i in range(per_group):
            rank = sum(beats(j, i, vals).astype(f32) for j in range(per_group) if j != i)
            pick = (grank == 0.0) & (rank < 2.0)
            nums.append(jnp.where(pick, sc_r[per_group * g + i], 0.0))
    den = sum(nums)
    return jnp.concatenate(nums, axis=0) / den


def _wo_kernel(x_ref, oa_ref, md_ref, wo_ref, g1_ref, ng_ref, sc_ref, sh_ref, rwt_ref, rb_ref,
               x1_ref, h_ref, gate_ref):
    y = jnp.dot(oa_ref[...].astype(bf16), wo_ref[0:VA, :], preferred_element_type=f32)
    y = y + jnp.dot(md_ref[...].astype(bf16), wo_ref[VA:, :], preferred_element_type=f32)
    x1 = x_ref[...] + g1_ref[...] * y
    x1_ref[...] = x1
    h = _rmsnorm(x1, ng_ref[...]) * (1.0 + sc_ref[...]) + sh_ref[...]
    h_ref[...] = h.astype(bf16)
    r_hi, r_lo = _split_bf16(rwt_ref[...], 2)
    h_hi, h_lo = _split_bf16(h, 2)
    logits_t = (lax.dot_general(r_hi, h_hi, NT, preferred_element_type=f32)
                + lax.dot_general(r_hi, h_lo, NT, preferred_element_type=f32)
                + lax.dot_general(r_lo, h_hi, NT, preferred_element_type=f32))
    gate_ref[...] = _route(logits_t, rb_ref[...])


def _wo_call(x2d, oa, md, wo, g1, ng, sc, sh, rwt, rb, tm, tiles_per_batch):
    n, d = x2d.shape
    row = lambda i: (i, 0)
    const = lambda shape: pl.BlockSpec(shape, lambda i: (0,) * len(shape))
    return pl.pallas_call(
        _wo_kernel,
        grid=(n // tm,),
        in_specs=[
            pl.BlockSpec((tm, d), row),
            pl.BlockSpec((tm, VA), row),
            pl.BlockSpec((tm, QD), row),
            _resident((d, d), lambda i: (0, 0)),
            _mod_spec(g1, tiles_per_batch),
            const((1, d)),
            _mod_spec(sc, tiles_per_batch),
            _mod_spec(sh, tiles_per_batch),
            const((N_EXPERTS, d)),
            const((N_EXPERTS, 1)),
        ],
        out_specs=[pl.BlockSpec((tm, d), row), pl.BlockSpec((tm, d), row), pl.BlockSpec((N_EXPERTS, tm), lambda i: (0, i))],
        out_shape=[jax.ShapeDtypeStruct((n, d), f32), jax.ShapeDtypeStruct((n, d), bf16),
                   jax.ShapeDtypeStruct((N_EXPERTS, n), f32)],
        compiler_params=_cparams("arbitrary"),
        name="wo_router",
    )(x2d, oa, md, wo, g1, ng, sc, sh, rwt, rb)


def _moe_dense_kernel(final, h_ref, gate_ref, wg_ref, wu_ref, wd_ref, x1_ref, g2_ref, fg_ref, out_ref, acc_ref):
    e = pl.program_id(1)

    @pl.when(e == 0)
    def _():
        acc_ref[...] = jnp.zeros_like(acc_ref)

    h = h_ref[...]
    hid = _silu(jnp.dot(h, wg_ref[...], preferred_element_type=f32)) * jnp.dot(h, wu_ref[...], preferred_element_type=f32)
    lane = lax.broadcasted_iota(jnp.int32, (1, N_EXPERTS), 1)
    ge = jnp.sum(jnp.where(lane == e, gate_ref[...], 0.0), axis=-1, keepdims=True)
    acc_ref[...] += jnp.dot((hid * ge).astype(bf16), wd_ref[...], preferred_element_type=f32)

    @pl.when(e == pl.num_programs(1) - 1)
    def _():
        x2 = x1_ref[...] + g2_ref[...] * acc_ref[...]
        out_ref[...] = _rmsnorm(x2, fg_ref[...]) if final else x2


def _moe_dense_call(h, gate, wg, wu, wd, x1, g2, fg, final, tm, tiles_per_batch):
    n, d = x1.shape
    ne, _, ff = wg.shape
    row = lambda i, e: (i, 0)
    return pl.pallas_call(
        functools.partial(_moe_dense_kernel, final),
        grid=(n // tm, ne),
        in_specs=[
            pl.BlockSpec((tm, d), row),
            pl.BlockSpec((tm, N_EXPERTS), row),
            pl.BlockSpec((None, d, ff), lambda i, e: (e, 0, 0)),
            pl.BlockSpec((None, d, ff), lambda i, e: (e, 0, 0)),
            pl.BlockSpec((None, ff, d), lambda i, e: (e, 0, 0)),
            pl.BlockSpec((tm, d), row),
            _mod_spec(g2, tiles_per_batch),
            pl.BlockSpec((1, d), lambda i, e: (0, 0)),
        ],
        out_specs=pl.BlockSpec((tm, d), row),
        out_shape=jax.ShapeDtypeStruct((n, d), f32),
        scratch_shapes=[pltpu.VMEM((tm, d), f32)],
        compiler_params=_cparams("arbitrary", "arbitrary"),
        name="moe_dense",
    )(h, gate, wg, wu, wd, x1, g2, fg)


MOE_TILE = 512
MOE_ALIGN = 16
MOE_WINDOW = 80
MOE_ROWS = 1344
MOE_SELECT_ROWS = 672
MOE_COMBINE_ROWS = 512
MOE_VMEM_LIMIT = 56 * 1024 * 1024
MOE_GROUP = 4
MOE_STEP_EXPERTS = 2


def _plan_kernel(gate_ref, slot_ref, off_ref, cnt_ref):
    t = gate_ref.shape[1]
    gate = gate_ref[...]
    sel = jnp.where(gate > 0.0, 1.0, 0.0)
    before = lax.broadcasted_iota(jnp.int32, (t, t), 0) < lax.broadcasted_iota(jnp.int32, (t, t), 1)
    rank = jnp.dot(sel.astype(bf16), jnp.where(before, 1.0, 0.0).astype(bf16), preferred_element_type=f32)
    cnt = jnp.sum(sel, axis=1, keepdims=True)
    cpad = jnp.floor((cnt + (MOE_ALIGN - 1)) * (1.0 / MOE_ALIGN)) * MOE_ALIGN
    offs, run = [], jnp.zeros((1, 1), f32)
    for e in range(N_EXPERTS):
        offs.append(run)
        run = run + cpad[e:e + 1]
    off = jnp.concatenate(offs, axis=0)
    dest = off + rank
    seen = jnp.zeros((1, t), f32)
    dest_a = dest_b = w_a = w_b = jnp.zeros((1, t), f32)
    n_a = n_b = jnp.zeros((1, t), f32)
    for e in range(N_EXPERTS):
        s_e = sel[e:e + 1]
        is_a = s_e * jnp.where(seen == 0.0, 1.0, 0.0)
        is_b = s_e * jnp.where(seen == 1.0, 1.0, 0.0)
        dest_a, w_a, n_a = dest_a + is_a * dest[e:e + 1], w_a + is_a * gate[e:e + 1], n_a + is_a
        dest_b, w_b, n_b = dest_b + is_b * dest[e:e + 1], w_b + is_b * gate[e:e + 1], n_b + is_b
        seen = seen + s_e
    dest_a = jnp.where(n_a > 0.0, dest_a, -1.0)
    dest_b = jnp.where(n_b > 0.0, dest_b, -1.0)
    slot_ref[...] = jnp.concatenate([dest_a, dest_b, w_a, w_b, jnp.zeros((4, t), f32)], axis=0)
    off_ref[...] = jnp.broadcast_to(off, (N_EXPERTS, 128))
    cnt_ref[...] = jnp.broadcast_to(cnt, (N_EXPERTS, 128))


def _plan_call(gate_t):
    n = gate_t.shape[1]
    nt = n // MOE_TILE
    slots, off, cnt = pl.pallas_call(
        _plan_kernel,
        grid=(nt,),
        in_specs=[pl.BlockSpec((N_EXPERTS, MOE_TILE), lambda i: (0, i))],
        out_specs=[pl.BlockSpec((8, MOE_TILE), lambda i: (0, i)),
                   pl.BlockSpec((None, N_EXPERTS, 128), lambda i: (i, 0, 0)),
                   pl.BlockSpec((None, N_EXPERTS, 128), lambda i: (i, 0, 0))],
        out_shape=[jax.ShapeDtypeStruct((8, n), f32), jax.ShapeDtypeStruct((nt, N_EXPERTS, 128), f32),
                   jax.ShapeDtypeStruct((nt, N_EXPERTS, 128), f32)],
        compiler_params=_cparams("arbitrary"),
        name="moe_plan",
    )(gate_t)
    to_smem = lambda a: a[:, :, 0].astype(jnp.int32).reshape(nt * N_EXPERTS)
    return slots, to_smem(off), to_smem(cnt)


def _moe_sparse_kernel(off_ref, cnt_ref, h_ref, srow_ref, scol_ref, wg_ref, wu_ref, wd_ref, y_ref, xs_ref, ys_ref):
    i, e = pl.program_id(0), pl.program_id(1)
    t = MOE_TILE
    nt = h_ref.shape[0] // t

    @pl.when(e == 0)
    def _():
        for s in range(nt):
            tok = slice(s * t, (s + 1) * t)
            dest_a, dest_b = srow_ref[0:1, tok], srow_ref[1:2, tok]
            for c in range(MOE_ROWS // MOE_SELECT_ROWS):
                r = (lax.broadcasted_iota(jnp.int32, (MOE_SELECT_ROWS, t), 0) + c * MOE_SELECT_ROWS).astype(f32)
                sel = (jnp.where(r == dest_a, 1.0, 0.0) + jnp.where(r == dest_b, 1.0, 0.0)).astype(bf16)
                rows = slice(c * MOE_SELECT_ROWS, (c + 1) * MOE_SELECT_ROWS)
                xs_ref[s, rows, :] = jnp.dot(sel, h_ref[tok, :], preferred_element_type=f32).astype(bf16)
        ys_ref[...] = jnp.zeros_like(ys_ref)

    for k in range(MOE_STEP_EXPERTS):
        ex = e * MOE_STEP_EXPERTS + k
        offs = [off_ref[(i * nt + s) * N_EXPERTS + ex] for s in range(nt)]
        nwin = [(cnt_ref[(i * nt + s) * N_EXPERTS + ex] + MOE_WINDOW - 1) // MOE_WINDOW for s in range(nt)]

        def window(j, carry, k=k, offs=offs, nwin=nwin):
            starts = [pl.multiple_of(offs[s] + jnp.maximum(jnp.minimum(j, nwin[s] - 1), 0) * MOE_WINDOW, MOE_ALIGN)
                      for s in range(nt)]
            xw = jnp.concatenate([xs_ref[s, pl.ds(starts[s], MOE_WINDOW), :] for s in range(nt)], axis=0)
            hid = _silu(jnp.dot(xw, wg_ref[k], preferred_element_type=f32)) * jnp.dot(xw, wu_ref[k], preferred_element_type=f32)
            yw = jnp.dot(hid.astype(bf16), wd_ref[k], preferred_element_type=f32).astype(bf16)
            for s in range(nt):
                ys_ref[s, pl.ds(starts[s], MOE_WINDOW), :] = yw[s * MOE_WINDOW:(s + 1) * MOE_WINDOW]
            return carry

        lax.fori_loop(0, functools.reduce(jnp.maximum, nwin), window, 0)

    @pl.when(e == pl.num_programs(1) - 1)
    def _():
        for s in range(nt):
            for c in range(t // MOE_COMBINE_ROWS):
                rows = slice(s * t + c * MOE_COMBINE_ROWS, s * t + (c + 1) * MOE_COMBINE_ROWS)
                sc = scol_ref[rows, :]
                r = lax.broadcasted_iota(jnp.int32, (MOE_COMBINE_ROWS, MOE_ROWS), 1).astype(f32)
                wsel = jnp.where(r == sc[:, 0:1], sc[:, 2:3], 0.0) + jnp.where(r == sc[:, 1:2], sc[:, 3:4], 0.0)
                y_ref[rows, :] = jnp.dot(wsel.astype(bf16), ys_ref[s], preferred_element_type=f32).astype(y_ref.dtype)


def _moe_sparse_call(h, gate_t, wg, wu, wd):
    n, d = h.shape
    ne, _, ff = wg.shape
    t = MOE_TILE
    tg = MOE_GROUP * t
    assert MOE_ROWS >= 2 * t + (N_EXPERTS - 1) * (MOE_ALIGN - 1) + MOE_WINDOW and MOE_ROWS % MOE_SELECT_ROWS == 0
    slots, off, cnt = _plan_call(gate_t)
    row = lambda i, e, *_: (i, 0)
    grid_spec = pltpu.PrefetchScalarGridSpec(
        num_scalar_prefetch=2,
        grid=(n // tg, ne // MOE_STEP_EXPERTS),
        in_specs=[
            pl.BlockSpec((tg, d), row),
            pl.BlockSpec((8, tg), lambda i, e, *_: (0, i)),
            pl.BlockSpec((tg, 8), row),
            pl.BlockSpec((MOE_STEP_EXPERTS, d, ff), lambda i, e, *_: (e, 0, 0)),
            pl.BlockSpec((MOE_STEP_EXPERTS, d, ff), lambda i, e, *_: (e, 0, 0)),
            pl.BlockSpec((MOE_STEP_EXPERTS, ff, d), lambda i, e, *_: (e, 0, 0)),
        ],
        out_specs=pl.BlockSpec((tg, d), row),
        scratch_shapes=[pltpu.VMEM((MOE_GROUP, MOE_ROWS, d), bf16), pltpu.VMEM((MOE_GROUP, MOE_ROWS, d), bf16)],
    )
    return pl.pallas_call(
        _moe_sparse_kernel,
        grid_spec=grid_spec,
        out_shape=jax.ShapeDtypeStruct((n, d), bf16),
        compiler_params=pltpu.CompilerParams(dimension_semantics=("arbitrary", "arbitrary"),
                                             vmem_limit_bytes=MOE_VMEM_LIMIT),
        name="moe_sparse",
    )(off, cnt, h, slots, slots.T, wg, wu, wd)


def _residual_norm_kernel(x1_ref, y_ref, g2_ref, fg_ref, out_ref):
    out_ref[...] = _rmsnorm(x1_ref[...] + g2_ref[...] * y_ref[...].astype(f32), fg_ref[...])


def _residual_norm_call(x1, y, g2, fg, tm, tiles_per_batch):
    n, d = x1.shape
    row = lambda i: (i, 0)
    return pl.pallas_call(
        _residual_norm_kernel,
        grid=(n // tm,),
        in_specs=[pl.BlockSpec((tm, d), row), pl.BlockSpec((tm, d), row), _mod_spec(g2, tiles_per_batch),
                  pl.BlockSpec((1, d), lambda i: (0, 0))],
        out_specs=pl.BlockSpec((tm, d), row),
        out_shape=jax.ShapeDtypeStruct((n, d), f32),
        compiler_params=_cparams("arbitrary"),
        name="residual_norm",
    )(x1, y, g2, fg)


def _reorder_w_in(w_in):
    o = np.cumsum((0, QA, QA, VA, GATE_RANK, VA, QD, QD, QD))
    pad = jnp.zeros(w_in.shape[:-1] + (128 - GATE_RANK,), w_in.dtype)
    parts = [w_in[..., o[0]:o[3]], w_in[..., o[4]:o[5]], w_in[..., o[3]:o[4]], pad, w_in[..., o[5]:o[8]]]
    return jnp.concatenate(parts, axis=-1).astype(bf16)


def _trunk(x, mods, prompt, gla_state, cache_k, cache_v, p):
    nb, s, d = x.shape
    n = nb * s
    depth = p["w_in"].shape[0]
    sh1, sc1, g1, sh2, sc2, g2 = mods
    if prompt:
        tm = 512
        tpb = s // tm
        keep = min(2048, s)
        keep_tiles = keep // tm
    else:
        tm, tpb, keep_tiles = n, 0, 0
    x2d = x.reshape(n, d)
    new_s, new_k, new_v = [], [], []
    resid = None
    for l in range(depth):
        outs = _inproj_call(x2d, sc1[l], sh1[l], p["norm1_g"][l], p["w_in"][l], tm, tpb, keep_tiles, resid)
        if resid is not None:
            x2d, outs = outs[0], outs[1:]
        a, q, k, v = outs[:4]
        if prompt:
            oa, st = _gla_prompt_call(a, p["wg"][l], p["bg"][l], p["gn"][l], nb, tm)
            new_s.append(_state_from_blockdiag(st))
            new_k.append(outs[4])
            new_v.append(outs[5])
            qkv = {1: (q, k, v)}
            for j, dil in enumerate(STRIDED_DILS):
                qkv[dil] = outs[6 + 3 * j:9 + 3 * j]
            os, ls = zip(*[_dil_prompt_call(*qkv[dil], p["bias_p"][c], nb, dil) for c, (_, dil) in enumerate(DIL_CONFIGS)])
            md = _combine_call(os, ls, tm)
        else:
            oa, st = _gla_sample_call(a, p["wg"][l], p["bg"][l], p["gn"][l], gla_state[l])
            oa = oa.reshape(n, VA)
            new_s.append(st.reshape(nb, H_A, DK_A, DV_A))
            new_k.append(k.reshape(nb, s, H_D, DH_D))
            new_v.append(v.reshape(nb, s, H_D, DH_D))
            md = _dil_sample_call(q, k, v, cache_k, cache_v, l, *p["bias_s"])
        x1, h2, gate_t = _wo_call(x2d, oa, md, p["w_o"][l], g1[l], p["norm2_g"][l], sc2[l], sh2[l],
                                  p["rwt"], p["rb"], tm, tpb)
        experts = (p["moe_wg"][l], p["moe_wu"][l], p["moe_wd"][l])
        if prompt:
            x2d, resid = x1, (_moe_sparse_call(h2, gate_t, *experts), g2[l])
        else:
            x2d = _moe_dense_call(h2, gate_t.T, *experts, x1, g2[l], p["final_g"], l == depth - 1, tm, tpb)
    new_k, new_v = jnp.stack(new_k), jnp.stack(new_v)
    if prompt:
        x2d = _residual_norm_call(x2d, *resid, p["final_g"], tm, tpb)
        new_k = new_k[:, :, 1:].reshape(depth, nb, keep, H_D, DH_D)
        new_v = new_v[:, :, 1:].reshape(depth, nb, keep, H_D, DH_D)
    return x2d.reshape(nb, s, d), jnp.stack(new_s), new_k, new_v


def kernel(x_prompt, x_sample, state_gla, cache_win_k, cache_win_v, c_prompt, c_sample, w_ada, b_ada, norm1_g, norm2_g, w_in, w_gate_up, b_gate, gla_norm_g, w_o, rel_bias, router_w, router_b, moe_w_gate, moe_w_up, moe_w_down, final_norm_g):
    nbp, seq, d = x_prompt.shape
    nbs, dec_seq, _ = x_sample.shape
    depth = w_in.shape[0]
    assert dec_seq == 1 and cache_win_k.shape[2] == DIL_CONFIGS[-1][0], "sample path: one new token over a full window"
    assert seq % (DIL_CONFIGS[-1][0] * ATTN_QUERY_BLOCKS) == 0, "prompt length must split into whole attention steps"

    c_all = jnp.concatenate([c_prompt, c_sample], axis=0)
    c_all = jnp.pad(c_all, ((0, -c_all.shape[0] % 8), (0, 0)))
    mod = _ada_call(c_all, w_ada, b_ada)
    mods_p = [m[:, :nbp, None, :] for m in jnp.split(mod, 6, axis=-1)]
    mods_s = [m[:, None, nbp:nbp + nbs, :] for m in jnp.split(mod, 6, axis=-1)]

    p = {
        "w_in": _reorder_w_in(w_in),
        "norm1_g": norm1_g.reshape(depth, 1, d),
        "norm2_g": norm2_g.reshape(depth, 1, d),
        "wg": jnp.pad(w_gate_up, ((0, 0), (0, 128 - GATE_RANK), (0, 0))),
        "bg": b_gate.reshape(depth, 1, QA),
        "gn": gla_norm_g.reshape(depth, 1, VA),
        "w_o": w_o.astype(bf16),
        "rwt": router_w.T,
        "rb": router_b.reshape(N_EXPERTS, 1),
        "moe_wg": moe_w_gate.astype(bf16),
        "moe_wu": moe_w_up.astype(bf16),
        "moe_wd": moe_w_down.astype(bf16),
        "final_g": final_norm_g.reshape(1, d),
        "bias_p": [_prompt_bias(rel_bias, dil) for (_, dil) in DIL_CONFIGS],
        "bias_s": _sample_bias(rel_bias, cache_win_k.shape[2]),
    }
    y_p, s_p, k_p, v_p = _trunk(x_prompt, mods_p, True, None, None, None, p)
    y_s, s_s, k_s, v_s = _trunk(x_sample, mods_s, False, state_gla, cache_win_k, cache_win_v, p)
    return (y_p, y_s, s_p, k_p, v_p, s_s, k_s, v_s)
```

```python
import functools
import math

import jax
import jax.numpy as jnp
import numpy as np
from jax import lax
from jax.experimental import pallas as pl
from jax.experimental.pallas import tpu as pltpu

f32 = jnp.float32
bf16 = jnp.bfloat16

H_A, DK_A, DV_A = 4, 32, 64
GATE_RANK = 16
GATE_TAU = 16.0
H_D, DH_D = 12, 64
DIL_CONFIGS = ((128, 1), (512, 4), (2048, 16))
N_BUCKETS = 32
MAX_DISTANCE = 2048
N_EXPERTS = 16
N_EXPERT_GROUPS = 4
EPS = 1e-6

QA = H_A * DK_A
VA = H_A * DV_A
QD = H_D * DH_D
A_WIDTH = 2 * QA + 2 * VA + 128
W_IN_COLS = A_WIDTH + 3 * QD
BLK = 128
N_PAIRS = H_D // 2
ATTN_QUERY_BLOCKS = 4
NEG = -1e30
GLA_SUB = 16
GLA_ROWS = 128
EXP_CLAMP = 80.0
VMEM_LIMIT = 48 * 1024 * 1024

HIGHEST = lax.Precision.HIGHEST
NT = (((1,), (1,)), ((), ()))
TN = (((0,), (0,)), ((), ()))


def _cparams(*sem):
    return pltpu.CompilerParams(dimension_semantics=sem, vmem_limit_bytes=VMEM_LIMIT)


def _resident(shape, index_map):
    return pl.BlockSpec(shape, index_map, pipeline_mode=pl.Buffered(1))


def _silu(x):
    return x * jax.nn.sigmoid(x)


def _split_bf16(x, terms):
    parts = []
    for _ in range(terms):
        p = x.astype(bf16)
        parts.append(p)
        x = x - p.astype(f32)
    return parts


def _rmsnorm(x, g):
    return x * lax.rsqrt(jnp.mean(x * x, axis=-1, keepdims=True) + EPS) * g


def _ada_kernel(c_ref, w_ref, b_ref, o_ref):
    cs = _silu(c_ref[...]).astype(bf16)
    o_ref[...] = jnp.dot(cs, w_ref[...].astype(bf16), preferred_element_type=f32) + b_ref[...]


def _ada_call(c, w_ada, b_ada):
    depth, d, n6 = w_ada.shape
    m = c.shape[0]
    tn = 1536
    return pl.pallas_call(
        _ada_kernel,
        grid=(depth, n6 // tn),
        in_specs=[
            pl.BlockSpec((m, d), lambda l, j: (0, 0)),
            pl.BlockSpec((None, d, tn), lambda l, j: (l, 0, j)),
            pl.BlockSpec((None, 1, tn), lambda l, j: (l, 0, j)),
        ],
        out_specs=pl.BlockSpec((None, m, tn), lambda l, j: (l, 0, j)),
        out_shape=jax.ShapeDtypeStruct((depth, m, n6), f32),
        compiler_params=_cparams("arbitrary", "arbitrary"),
        name="adaln",
    )(c, w_ada, b_ada.reshape(depth, 1, n6))


def _inproj_kernel(prompt, resid, x_ref, *refs):
    x = x_ref[...]
    if resid:
        y_ref, g2_ref, x2_ref = refs[0], refs[1], refs[6]
        x = x + g2_ref[...] * y_ref[...].astype(f32)
        x2_ref[...] = x
        refs = refs[2:6] + refs[7:]
    sc_ref, sh_ref, g_ref, w_ref = refs[:4]
    outs = refs[4:]
    h = _rmsnorm(x, g_ref[...])
    h = (h * (1.0 + sc_ref[...]) + sh_ref[...]).astype(bf16)

    def proj(lo, hi):
        return jnp.dot(h, w_ref[:, lo:hi], preferred_element_type=f32)

    a_ref, q_ref, k_ref, v_ref = outs[:4]
    a_ref[...] = proj(0, A_WIDTH)
    q = proj(A_WIDTH, A_WIDTH + QD)
    k = proj(A_WIDTH + QD, A_WIDTH + 2 * QD)
    v = proj(A_WIDTH + 2 * QD, A_WIDTH + 3 * QD)
    if prompt:
        n_cls = len(STRIDED_DILS)
        kw_ref, vw_ref = outs[4:6]
        cls_refs = outs[6:6 + 3 * n_cls]
        scratch = outs[6 + 3 * n_cls:]
        q = q * (DH_D ** -0.5)
        kw_ref[...] = k
        vw_ref[...] = v
        for j, (val, nat_ref) in enumerate(((q, q_ref), (k, k_ref), (v, v_ref))):
            nat_ref[...] = val.astype(bf16)
            _to_class_major(val, scratch[0], scratch[1], [cls_refs[3 * c + j] for c in range(n_cls)])
    else:
        q_ref[...] = q
        k_ref[...] = k
        v_ref[...] = v


STRIDED_DILS = tuple(dil for (_, dil) in DIL_CONFIGS if dil > 1)
LANE_TILES = QD // 128


def _to_class_major(val, planes, class_planes, dst_refs):
    rows = val.shape[0]
    d0, d1 = STRIDED_DILS
    f = d1 // d0
    dst0, dst1 = dst_refs
    for c in range(LANE_TILES):
        planes[c] = val[:, c * 128:(c + 1) * 128]
    for c in range(LANE_TILES):
        for r in range(d0):
            x = planes[c, pl.ds(r, rows // d0, stride=d0), :]
            class_planes[r * LANE_TILES + c] = x
            lo = r * QD + c * 128
            dst0[:, lo:lo + 128] = x.astype(dst0.dtype)
    for c in range(LANE_TILES):
        for r in range(d0):
            for r2 in range(f):
                lo = (d0 * r2 + r) * QD + c * 128
                dst1[:, lo:lo + 128] = class_planes[r * LANE_TILES + c, pl.ds(r2, rows // d1, stride=f), :].astype(dst1.dtype)


def _from_class_major(src_ref, planes, dil, width, class_planes=None):
    rows = planes.shape[1]
    tiles = width // 128
    if class_planes is None:
        for c in range(tiles):
            for r in range(dil):
                lo = r * width + c * 128
                planes[c, pl.ds(r, rows // dil, stride=dil), :] = src_ref[:, lo:lo + 128].astype(f32)
        return
    d0 = STRIDED_DILS[0]
    f = dil // d0
    for c in range(tiles):
        for r in range(d0):
            for r2 in range(f):
                lo = (d0 * r2 + r) * width + c * 128
                class_planes[r * tiles + c, pl.ds(r2, rows // dil, stride=f), :] = src_ref[:, lo:lo + 128].astype(f32)
    for c in range(tiles):
        for r in range(d0):
            planes[c, pl.ds(r, rows // d0, stride=d0), :] = class_planes[r * tiles + c]


def _mod_spec(arr, tiles_per_batch):
    rows, d = arr.shape[1], arr.shape[2]
    if tiles_per_batch:
        return pl.BlockSpec((None, rows, d), lambda i, *_: (i // tiles_per_batch, 0, 0))
    return pl.BlockSpec((None, rows, d), lambda i, *_: (0, 0, 0))


def _inproj_call(x2d, sc, sh, g, w, tm, tiles_per_batch, keep_tiles, resid=None):
    n, d = x2d.shape
    prompt = bool(tiles_per_batch)
    qkv_dt = bf16 if prompt else f32
    row = lambda i: (i, 0)
    out_specs = [pl.BlockSpec((tm, A_WIDTH), row)] + [pl.BlockSpec((tm, QD), row)] * 3
    out_shape = [jax.ShapeDtypeStruct((n, A_WIDTH), f32)] + [jax.ShapeDtypeStruct((n, QD), qkv_dt)] * 3
    scratch = []
    if prompt:
        nb = n // (tm * tiles_per_batch)
        first_keep = tiles_per_batch - keep_tiles

        def win(i):
            return (i // tiles_per_batch, jnp.maximum(i % tiles_per_batch - first_keep + 1, 0), 0, 0)

        out_specs += [pl.BlockSpec((None, None, tm, QD), win)] * 2
        out_shape += [jax.ShapeDtypeStruct((nb, keep_tiles + 1, tm, QD), f32)] * 2
        for dil in STRIDED_DILS:
            out_specs += [pl.BlockSpec((tm // dil, dil * QD), row)] * 3
            out_shape += [jax.ShapeDtypeStruct((n // dil, dil * QD), bf16)] * 3
        d0, d1 = STRIDED_DILS
        assert d1 % d0 == 0
        scratch = [pltpu.VMEM((LANE_TILES, tm, 128), f32), pltpu.VMEM((d0 * LANE_TILES, tm // d0, 128), f32)]
    in_specs = [
        _mod_spec(sc, tiles_per_batch),
        _mod_spec(sh, tiles_per_batch),
        pl.BlockSpec((1, d), lambda i: (0, 0)),
        _resident((d, W_IN_COLS), lambda i: (0, 0)),
    ]
    args = (sc, sh, g, w)
    if resid is not None:
        in_specs = [pl.BlockSpec((tm, d), row), _mod_spec(resid[1], tiles_per_batch)] + in_specs
        args = resid + args
        out_specs = [pl.BlockSpec((tm, d), row)] + out_specs
        out_shape = [jax.ShapeDtypeStruct((n, d), f32)] + out_shape
    return pl.pallas_call(
        functools.partial(_inproj_kernel, prompt, resid is not None),
        grid=(n // tm,),
        in_specs=[pl.BlockSpec((tm, d), row)] + in_specs,
        out_specs=out_specs,
        out_shape=out_shape,
        scratch_shapes=scratch,
        compiler_params=_cparams("arbitrary"),
        name="inproj_prompt" if prompt else "inproj_sample",
    )(x2d, *args)


def _log_gate(glr, wg_ref, bg_ref):
    x = jnp.dot(glr.astype(bf16), wg_ref[...].astype(bf16), preferred_element_type=f32) + bg_ref[...]
    return (jnp.minimum(x, 0.0) - jnp.log1p(jnp.exp(-jnp.abs(x)))) * (1.0 / GATE_TAU)


def _gla_prompt_kernel(a_ref, wg_ref, bg_ref, gn_ref, o_ref, s_ref, st_ref):
    R, C = GLA_ROWS, GLA_SUB

    @pl.when(pl.program_id(1) == 0)
    def _():
        st_ref[...] = jnp.zeros_like(st_ref)

    def iota(shape, dim, shift=0):
        return lax.shift_right_logical(lax.broadcasted_iota(jnp.int32, shape, dim), shift)

    sub, lk, lv = int(math.log2(C)), int(math.log2(DK_A)), int(math.log2(DV_A))
    ri = lax.bitwise_and(iota((H_A * R, R), 0), R - 1)
    ci = iota((H_A * R, R), 1)
    same4 = lax.shift_right_logical(ri, sub) == lax.shift_right_logical(ci, sub)
    causal4 = same4 & (ci <= ri)
    sums = jnp.concatenate([jnp.where(causal4, 1.0, 0.0)[0:R], jnp.where(same4, 1.0, 0.0)[0:R]], axis=0).astype(bf16)
    lane_q = iota((1, QA), 1, lk)
    lane_v = iota((1, VA), 1, lv)
    bd = iota((QA, VA), 0, lk) == iota((QA, VA), 1, lv)
    hmean = jnp.where(iota((VA, VA), 0, lv) == iota((VA, VA), 1, lv), 1.0 / DV_A, 0.0).astype(bf16)
    row_sub = iota((R, 1), 0, sub)
    sub_mean = jnp.where(iota((R, 128), 0, sub) == iota((R, 128), 1), 1.0 / C, 0.0).astype(bf16)
    nsub = R // C

    def prepare(c):
        rows = slice(c * R, (c + 1) * R)
        blk = a_ref[rows, :]
        q = blk[:, 0:QA] * (DK_A ** -0.5)
        k = blk[:, QA:2 * QA]
        v = blk[:, 2 * QA:2 * QA + VA]
        r = blk[:, 2 * QA + VA:2 * QA + 2 * VA]
        g = _log_gate(blk[:, 2 * QA + 2 * VA:], wg_ref, bg_ref)
        cums = sum(jnp.dot(sums, part, preferred_element_type=f32) for part in _split_bf16(g, 3))
        b, bt = cums[0:R], cums[R:2 * R]
        qb = q * jnp.exp(b)
        kd = (k * jnp.exp(bt - b)).astype(bf16)
        ki = (k * jnp.exp(jnp.minimum(-b, EXP_CLAMP))).astype(bf16)
        vb = v.astype(bf16)
        qs = jnp.concatenate([jnp.where(lane_q == h, qb, 0.0) for h in range(H_A)], axis=0).astype(bf16)
        att = lax.dot_general(qs, ki, NT, preferred_element_type=f32)
        att = jnp.where(causal4, att, 0.0).astype(bf16)
        res = jnp.dot(att, vb, preferred_element_type=f32)
        o = jnp.where(lane_v == 0, res[0:R], 0.0)
        for h in range(1, H_A):
            o = o + jnp.where(lane_v == h, res[h * R:(h + 1) * R], 0.0)
        qbb = qb.astype(bf16)
        zero = jnp.zeros((), bf16)
        v_exp = jnp.concatenate([jnp.where(row_sub == i, vb, zero) for i in range(nsub)], axis=1)
        kv_all = lax.dot_general(kd, v_exp, TN, preferred_element_type=f32)
        decay = sum(lax.dot_general(part, sub_mean, TN, preferred_element_type=f32)
                    for part in _split_bf16(jnp.exp(bt), 3))
        q_exp = jnp.concatenate([jnp.where(row_sub == i, qbb, zero) for i in range(nsub)], axis=1)
        return o, kv_all, decay, q_exp, _silu(r)

    def finish(c, st, o, kv_all, decay, q_exp, gate):
        states = []
        for i in range(nsub):
            states.append(st.astype(bf16))
            st = st * decay[:, i:i + 1] + jnp.where(bd, kv_all[:, i * VA:(i + 1) * VA], 0.0)
        o = o + jnp.dot(q_exp, jnp.concatenate(states, axis=0), preferred_element_type=f32)
        ms = sum(jnp.dot(part, hmean, preferred_element_type=f32) for part in _split_bf16(o * o, 2))
        y = o * lax.rsqrt(ms + EPS) * gn_ref[...]
        o_ref[c * R:(c + 1) * R, :] = (y * gate).astype(o_ref.dtype)
        return st

    nblk = a_ref.shape[0] // R
    prepared = [prepare(c) for c in range(nblk)]
    st = st_ref[...]
    for c in range(nblk):
        st = finish(c, st, *prepared[c])
    st_ref[...] = st
    s_ref[...] = st


def _gla_prompt_call(a, wg, bg, gn, nb, tm):
    n = a.shape[0]
    tpb = n // nb // tm
    return pl.pallas_call(
        _gla_prompt_kernel,
        grid=(nb, tpb),
        in_specs=[
            pl.BlockSpec((tm, A_WIDTH), lambda b, j: (b * tpb + j, 0)),
            pl.BlockSpec((128, QA), lambda b, j: (0, 0)),
            pl.BlockSpec((1, QA), lambda b, j: (0, 0)),
            pl.BlockSpec((1, VA), lambda b, j: (0, 0)),
        ],
        out_specs=[
            pl.BlockSpec((tm, VA), lambda b, j: (b * tpb + j, 0)),
            pl.BlockSpec((None, QA, VA), lambda b, j: (b, 0, 0)),
        ],
        out_shape=[jax.ShapeDtypeStruct((n, VA), bf16), jax.ShapeDtypeStruct((nb, QA, VA), f32)],
        scratch_shapes=[pltpu.VMEM((QA, VA), f32)],
        compiler_params=_cparams("arbitrary", "arbitrary"),
        name="gla_prompt",
    )(a, wg, bg, gn)


def _state_from_blockdiag(st):
    b = st.shape[0]
    st5 = st.reshape(b, H_A, DK_A, H_A, DV_A)
    return jnp.stack([st5[:, h, :, h, :] for h in range(H_A)], axis=1)


def _gla_sample_kernel(a_ref, at_ref, wgt_ref, bgt_ref, gn_ref, s_ref, o_ref, so_ref):
    nb = a_ref.shape[0]
    qt = at_ref[0:QA, :] * (DK_A ** -0.5)
    kt = at_ref[QA:2 * QA, :]
    xt = jnp.dot(wgt_ref[...].astype(bf16), at_ref[2 * QA + 2 * VA:, :].astype(bf16), preferred_element_type=f32) + bgt_ref[...]
    et = jnp.exp((jnp.minimum(xt, 0.0) - jnp.log1p(jnp.exp(-jnp.abs(xt)))) * (1.0 / GATE_TAU))
    gn = gn_ref[...]
    for t in range(nb):
        row = a_ref[t:t + 1, :]
        vt = jnp.concatenate(
            [jnp.broadcast_to(row[:, 2 * QA + h * DV_A:2 * QA + (h + 1) * DV_A], (DK_A, DV_A)) for h in range(H_A)], axis=0)
        s_new = s_ref[t] * et[:, t:t + 1] + kt[:, t:t + 1] * vt
        so_ref[t] = s_new
        o = jnp.sum((qt[:, t:t + 1] * s_new).reshape(H_A, DK_A, DV_A), axis=1)
        y = _rmsnorm(o, gn)
        r = jnp.concatenate([row[:, 2 * QA + VA + h * DV_A:2 * QA + VA + (h + 1) * DV_A] for h in range(H_A)], axis=0)
        o_ref[t] = y * _silu(r)


def _gla_sample_call(a, wg, bg, gn, state):
    nb = a.shape[0]
    return pl.pallas_call(
        _gla_sample_kernel,
        out_shape=[jax.ShapeDtypeStruct((nb, H_A, DV_A), f32), jax.ShapeDtypeStruct((nb, QA, DV_A), f32)],
        compiler_params=pltpu.CompilerParams(vmem_limit_bytes=VMEM_LIMIT),
        name="gla_sample",
    )(a, a.T, wg.T, bg.reshape(QA, 1), gn.reshape(H_A, DV_A), state.reshape(nb, QA, DV_A))


def _rel_bucket_np(dist):
    max_exact = N_BUCKETS // 2
    d = np.maximum(dist, 0)
    df = np.maximum(d, 1).astype(np.float32)
    large = max_exact + (np.log(df / np.float32(max_exact)) / np.float32(math.log(MAX_DISTANCE / max_exact))
                         * np.float32(N_BUCKETS - max_exact)).astype(np.int32)
    large = np.minimum(large, N_BUCKETS - 1)
    return np.where(d < max_exact, d, large).astype(np.int32)


def _bias_lookup(rel_bias, buckets):
    onehot = np.zeros((buckets.size, N_BUCKETS), np.float32)
    onehot[np.arange(buckets.size), buckets.reshape(-1)] = 1.0
    out = jnp.dot(jnp.asarray(onehot), rel_bias.astype(f32), precision=HIGHEST)
    return out.reshape(buckets.shape + (rel_bias.shape[-1],))


def _prompt_bias(rel_bias, dil):
    i = np.arange(BLK)[:, None]
    j = np.arange(2 * BLK)[None, :]
    delta = BLK + i - j
    band = (delta >= 0) & (delta <= BLK)
    tbl = _bias_lookup(rel_bias, _rel_bucket_np(delta * dil))
    tbl = jnp.where(jnp.asarray(band)[:, :, None], tbl, NEG)
    first = jnp.where(jnp.asarray(j >= BLK)[:, :, None], tbl, NEG)
    pairs = lambda t: jnp.transpose(t, (2, 0, 1)).reshape(N_PAIRS, 2 * BLK, 2 * BLK)
    return jnp.concatenate([pairs(tbl), pairs(first)], axis=0)


def _dil_prompt_kernel(q_ref, kp_ref, kc_ref, vp_ref, vc_ref, bias_ref, o_ref, lse_ref):
    lane = lax.broadcasted_iota(jnp.int32, (1, 2 * DH_D), 1)
    lo = lane < DH_D
    first = jnp.where(pl.program_id(2) == 0, N_PAIRS, 0)
    lane_h = lax.broadcasted_iota(jnp.int32, (1, 128), 1)
    zero = jnp.zeros((), bf16)
    nq = q_ref.shape[0] // BLK
    lse_acc = [jnp.zeros((BLK, 128), f32) for _ in range(nq)]
    for p in range(N_PAIRS):
        cols = slice(p * 2 * DH_D, (p + 1) * 2 * DH_D)
        k_all = jnp.concatenate([kp_ref[:, cols], kc_ref[:, cols]], axis=0)
        v_all = jnp.concatenate([vp_ref[:, cols], vc_ref[:, cols]], axis=0)
        for qb in range(nq):
            rows = slice(qb * BLK, (qb + 1) * BLK)
            qp = q_ref[rows, cols]
            k2, v2 = k_all[qb * BLK:(qb + 2) * BLK], v_all[qb * BLK:(qb + 2) * BLK]
            variant = first if qb == 0 else 0
            halves = []
            for hh, qh in enumerate((jnp.where(lo, qp, zero), jnp.where(lo, zero, qp))):
                s = lax.dot_general(qh, k2, NT, preferred_element_type=f32) + bias_ref[variant + p, hh * BLK:(hh + 1) * BLK, :]
                m = jnp.max(s, axis=-1, keepdims=True)
                e = jnp.exp(s - m)
                den = jnp.sum(e, axis=-1, keepdims=True)
                halves.append(jnp.dot(e.astype(bf16), v2, preferred_element_type=f32) / den)
                lse_acc[qb] = jnp.where(lane_h == 2 * p + hh, m + jnp.log(den), lse_acc[qb])
            o_ref[rows, cols] = jnp.where(lo, halves[0], halves[1]).astype(o_ref.dtype)
    for qb in range(nq):
        lse_ref[qb * BLK:(qb + 1) * BLK, :] = lse_acc[qb]


def _dil_prompt_call(q, k, v, bias, nb, dil):
    rows = q.shape[0] // nb
    nq = ATTN_QUERY_BLOCKS
    nblk = rows // (nq * BLK)
    view = lambda t: t.reshape(nb, rows, t.shape[-1])
    cur = pl.BlockSpec((None, nq * BLK, QD), lambda b, r, i: (b, i, r))
    prev = pl.BlockSpec((None, BLK, QD), lambda b, r, i: (b, jnp.maximum(nq * i - 1, 0), r))
    o, lse = pl.pallas_call(
        _dil_prompt_kernel,
        grid=(nb, dil, nblk),
        in_specs=[cur, prev, cur, prev, cur, _resident((2 * N_PAIRS, 2 * BLK, 2 * BLK), lambda b, r, i: (0, 0, 0))],
        out_specs=[cur, pl.BlockSpec((None, nq * BLK, 128), lambda b, r, i: (b, i, r))],
        out_shape=[jax.ShapeDtypeStruct((nb, rows, dil * QD), bf16), jax.ShapeDtypeStruct((nb, rows, dil * 128), f32)],
        compiler_params=_cparams("arbitrary", "arbitrary", "arbitrary"),
        name=f"dilated_prompt_d{dil}",
    )(view(q), view(k), view(k), view(v), view(v), bias)
    return o.reshape(nb * rows, dil * QD), lse.reshape(nb * rows, dil * 128)


def _head_expand_np():
    e = np.zeros((128, QD), np.float32)
    for h in range(H_D):
        e[h, h * DH_D:(h + 1) * DH_D] = 1.0
    return e


def _expand(w, e_ref):
    hi = w.astype(bf16)
    lo = (w - hi.astype(f32)).astype(bf16)
    return jnp.dot(hi, e_ref[...], preferred_element_type=f32) + jnp.dot(lo, e_ref[...], preferred_element_type=f32)


def _combine_kernel(*refs):
    nbr = len(DIL_CONFIGS)
    o_refs, l_refs = refs[:nbr], refs[nbr:2 * nbr]
    e_ref, out_ref = refs[2 * nbr], refs[2 * nbr + 1]
    scratch = refs[2 * nbr + 2:]
    o_planes, lses, used = [], [], 0
    for c, (_, dil) in enumerate(DIL_CONFIGS):
        if dil == 1:
            o_planes.append(None)
            lses.append(l_refs[c][...])
        else:
            op, lp = scratch[used], scratch[used + 1]
            used += 2
            ocp = lcp = None
            if dil != STRIDED_DILS[0]:
                ocp, lcp = scratch[used], scratch[used + 1]
                used += 2
            _from_class_major(o_refs[c], op, dil, QD, ocp)
            _from_class_major(l_refs[c], lp, dil, 128, lcp)
            o_planes.append(op)
            lses.append(lp[0])
    m = functools.reduce(jnp.maximum, lses)
    es = [jnp.exp(l - m) for l in lses]
    z = functools.reduce(lambda a, b: a + b, es)
    ws = [_expand(e / z, e_ref) for e in es]
    for t in range(LANE_TILES):
        cols = slice(t * 128, (t + 1) * 128)
        acc = None
        for c in range(nbr):
            o = o_refs[c][:, cols].astype(f32) if o_planes[c] is None else o_planes[c][t]
            acc = ws[c][:, cols] * o if acc is None else acc + ws[c][:, cols] * o
        out_ref[:, cols] = acc.astype(out_ref.dtype)


def _combine_call(os, ls, tm):
    n = os[0].shape[0] * DIL_CONFIGS[0][1]
    row = lambda i: (i, 0)
    o_specs = [pl.BlockSpec((tm // dil, dil * QD), row) for (_, dil) in DIL_CONFIGS]
    l_specs = [pl.BlockSpec((tm // dil, dil * 128), row) for (_, dil) in DIL_CONFIGS]
    scratch = []
    for (_, dil) in DIL_CONFIGS:
        if dil > 1:
            scratch += [pltpu.VMEM((LANE_TILES, tm, 128), f32), pltpu.VMEM((1, tm, 128), f32)]
            d0 = STRIDED_DILS[0]
            if dil != d0:
                scratch += [pltpu.VMEM((d0 * LANE_TILES, tm // d0, 128), f32), pltpu.VMEM((d0, tm // d0, 128), f32)]
    return pl.pallas_call(
        _combine_kernel,
        grid=(n // tm,),
        in_specs=o_specs + l_specs + [pl.BlockSpec((128, QD), lambda i: (0, 0))],
        out_specs=pl.BlockSpec((tm, QD), row),
        out_shape=jax.ShapeDtypeStruct((n, QD), bf16),
        scratch_shapes=scratch,
        compiler_params=_cparams("arbitrary"),
        name="branch_combine",
    )(*os, *ls, jnp.asarray(_head_expand_np(), bf16))


SAMPLE_HEADS = 12


def _sample_bias(rel_bias, length):
    dist = length - np.arange(length)
    tbl = _bias_lookup(rel_bias, _rel_bucket_np(dist)).T
    rows = []
    for (_, dil) in DIL_CONFIGS:
        valid = (dist % dil == 0) & (dist <= BLK * dil)
        rows.append(jnp.where(jnp.asarray(valid)[None, :], tbl, NEG))
    b0 = _bias_lookup(rel_bias, _rel_bucket_np(np.zeros((1,), np.int64)))
    groups = H_D // SAMPLE_HEADS
    return (jnp.stack(rows).reshape(len(DIL_CONFIGS), groups, SAMPLE_HEADS, length),
            jnp.broadcast_to(b0.reshape(groups, SAMPLE_HEADS, 1), (groups, SAMPLE_HEADS, 128)))


def _dil_sample_kernel(qkv_ref, kt_ref, vt_ref, bias_ref, b0_ref, o_ref):
    nh = SAMPLE_HEADS
    eye = jnp.where(lax.broadcasted_iota(jnp.int32, (DH_D, DH_D), 0) == lax.broadcasted_iota(jnp.int32, (DH_D, DH_D), 1),
                    1.0, 0.0).astype(f32)
    cols = lax.dot_general(eye, qkv_ref[...], NT, precision=HIGHEST, preferred_element_type=f32)
    qc = cols[:, 0:nh] * (DH_D ** -0.5)
    lg = jnp.concatenate([jnp.sum(kt_ref[h] * qc[:, h:h + 1], axis=0, keepdims=True) for h in range(nh)], axis=0)
    l0 = jnp.concatenate([jnp.sum(qc[:, h:h + 1] * cols[:, nh + h:nh + h + 1], axis=0, keepdims=True) for h in range(nh)],
                         axis=0) + b0_ref[:, 0:1]
    ps, p0s, dens, lses = [], [], [], []
    for c in range(len(DIL_CONFIGS)):
        s = lg + bias_ref[c]
        m = jnp.maximum(jnp.max(s, axis=-1, keepdims=True), l0)
        p = jnp.exp(s - m)
        p0 = jnp.exp(l0 - m)
        den = jnp.sum(p, axis=-1, keepdims=True) + p0
        ps.append(p), p0s.append(p0), dens.append(den), lses.append(m + jnp.log(den))
    m = jnp.maximum(jnp.maximum(lses[0], lses[1]), lses[2])
    es = [jnp.exp(l - m) for l in lses]
    z = es[0] + es[1] + es[2]
    coef = [e / z / den for e, den in zip(es, dens)]
    pmix = coef[0] * ps[0] + coef[1] * ps[1] + coef[2] * ps[2]
    p0mix = coef[0] * p0s[0] + coef[1] * p0s[1] + coef[2] * p0s[2]
    lane = lax.broadcasted_iota(jnp.int32, (1, 128), 1)
    ocols = jnp.zeros((DH_D, 128), f32)
    for h in range(nh):
        oc = jnp.sum(vt_ref[h] * pmix[h:h + 1, :], axis=1, keepdims=True)
        oc = oc + p0mix[h:h + 1, 0:1] * cols[:, 2 * nh + h:2 * nh + h + 1]
        ocols = jnp.where(lane == h, oc, ocols)
    o = lax.dot_general(ocols, eye, TN, precision=HIGHEST, preferred_element_type=f32)
    o_ref[...] = o[0:nh]


def _dil_sample_call(q, kn, vn, cache_k, cache_v, layer, bias, b0):
    nb = q.shape[0]
    depth, _, length = cache_k.shape[:3]
    groups = H_D // SAMPLE_HEADS
    hd = pl.BlockSpec((None, None, SAMPLE_HEADS, DH_D), lambda b, g: (b, g, 0, 0))
    cache_spec = pl.BlockSpec((None, None, None, SAMPLE_HEADS, DH_D, length), lambda b, g: (layer, b, g, 0, 0, 0))
    view = lambda c: jnp.transpose(c, (0, 1, 3, 4, 2)).reshape(depth, nb, groups, SAMPLE_HEADS, DH_D, length)
    to_hd = lambda t: t.reshape(nb, groups, SAMPLE_HEADS, DH_D)
    qkv = jnp.concatenate([to_hd(q), to_hd(kn), to_hd(vn), jnp.zeros_like(to_hd(q))], axis=2)
    out = pl.pallas_call(
        _dil_sample_kernel,
        grid=(nb, groups),
        in_specs=[pl.BlockSpec((None, None, 4 * SAMPLE_HEADS, DH_D), lambda b, g: (b, g, 0, 0)), cache_spec, cache_spec,
                  pl.BlockSpec((len(DIL_CONFIGS), None, SAMPLE_HEADS, length), lambda b, g: (0, g, 0, 0)),
                  pl.BlockSpec((None, SAMPLE_HEADS, 128), lambda b, g: (g, 0, 0))],
        out_specs=hd,
        out_shape=jax.ShapeDtypeStruct((nb, groups, SAMPLE_HEADS, DH_D), f32),
        compiler_params=_cparams("arbitrary", "arbitrary"),
        name="dilated_sample",
    )(qkv, view(cache_k), view(cache_v), bias, b0)
    return out.reshape(nb, QD)


def _route(logits_t, rb):
    per_group = N_EXPERTS // N_EXPERT_GROUPS
    sc = jax.nn.sigmoid(logits_t)
    sel = sc + rb
    sel_r = [sel[e:e + 1, :] for e in range(N_EXPERTS)]
    sc_r = [sc[e:e + 1, :] for e in range(N_EXPERTS)]

    def beats(j, i, vals):
        return (vals[j] >= vals[i]) if j < i else (vals[j] > vals[i])

    gs = []
    for g in range(N_EXPERT_GROUPS):
        a, b, c, d = sel_r[per_group * g:per_group * (g + 1)]
        m1, n1, m2, n2 = jnp.maximum(a, b), jnp.minimum(a, b), jnp.maximum(c, d), jnp.minimum(c, d)
        gs.append(jnp.maximum(m1, m2) + jnp.maximum(jnp.minimum(m1, m2), jnp.maximum(n1, n2)))
    nums = []
    for g in range(N_EXPERT_GROUPS):
        grank = sum(beats(j, g, gs).astype(f32) for j in range(N_EXPERT_GROUPS) if j != g)
        vals = sel_r[per_group * g:per_group * (g + 1)]
        for i in range(per_group):
            rank = sum(beats(j, i, vals).astype(f32) for j in range(per_group) if j != i)
            pick = (grank == 0.0) & (rank < 2.0)
            nums.append(jnp.where(pick, sc_r[per_group * g + i], 0.0))
    den = sum(nums)
    return jnp.concatenate(nums, axis=0) / den


def _wo_kernel(x_ref, oa_ref, md_ref, wo_ref, g1_ref, ng_ref, sc_ref, sh_ref, rwt_ref, rb_ref,
               x1_ref, h_ref, gate_ref):
    y = jnp.dot(oa_ref[...].astype(bf16), wo_ref[0:VA, :], preferred_element_type=f32)
    y = y + jnp.dot(md_ref[...].astype(bf16), wo_ref[VA:, :], preferred_element_type=f32)
    x1 = x_ref[...] + g1_ref[...] * y
    x1_ref[...] = x1
    h = _rmsnorm(x1, ng_ref[...]) * (1.0 + sc_ref[...]) + sh_ref[...]
    h_ref[...] = h.astype(bf16)
    r_hi, r_lo = _split_bf16(rwt_ref[...], 2)
    h_hi, h_lo = _split_bf16(h, 2)
    logits_t = (lax.dot_general(r_hi, h_hi, NT, preferred_element_type=f32)
                + lax.dot_general(r_hi, h_lo, NT, preferred_element_type=f32)
                + lax.dot_general(r_lo, h_hi, NT, preferred_element_type=f32))
    gate_ref[...] = _route(logits_t, rb_ref[...])


def _wo_call(x2d, oa, md, wo, g1, ng, sc, sh, rwt, rb, tm, tiles_per_batch):
    n, d = x2d.shape
    row = lambda i: (i, 0)
    const = lambda shape: pl.BlockSpec(shape, lambda i: (0,) * len(shape))
    return pl.pallas_call(
        _wo_kernel,
        grid=(n // tm,),
        in_specs=[
            pl.BlockSpec((tm, d), row),
            pl.BlockSpec((tm, VA), row),
            pl.BlockSpec((tm, QD), row),
            _resident((d, d), lambda i: (0, 0)),
            _mod_spec(g1, tiles_per_batch),
            const((1, d)),
            _mod_spec(sc, tiles_per_batch),
            _mod_spec(sh, tiles_per_batch),
            const((N_EXPERTS, d)),
            const((N_EXPERTS, 1)),
        ],
        out_specs=[pl.BlockSpec((tm, d), row), pl.BlockSpec((tm, d), row), pl.BlockSpec((N_EXPERTS, tm), lambda i: (0, i))],
        out_shape=[jax.ShapeDtypeStruct((n, d), f32), jax.ShapeDtypeStruct((n, d), bf16),
                   jax.ShapeDtypeStruct((N_EXPERTS, n), f32)],
        compiler_params=_cparams("arbitrary"),
        name="wo_router",
    )(x2d, oa, md, wo, g1, ng, sc, sh, rwt, rb)


def _moe_dense_kernel(final, h_ref, gate_ref, wg_ref, wu_ref, wd_ref, x1_ref, g2_ref, fg_ref, out_ref, acc_ref):
    e = pl.program_id(1)

    @pl.when(e == 0)
    def _():
        acc_ref[...] = jnp.zeros_like(acc_ref)

    h = h_ref[...]
    hid = _silu(jnp.dot(h, wg_ref[...], preferred_element_type=f32)) * jnp.dot(h, wu_ref[...], preferred_element_type=f32)
    lane = lax.broadcasted_iota(jnp.int32, (1, N_EXPERTS), 1)
    ge = jnp.sum(jnp.where(lane == e, gate_ref[...], 0.0), axis=-1, keepdims=True)
    acc_ref[...] += jnp.dot((hid * ge).astype(bf16), wd_ref[...], preferred_element_type=f32)

    @pl.when(e == pl.num_programs(1) - 1)
    def _():
        x2 = x1_ref[...] + g2_ref[...] * acc_ref[...]
        out_ref[...] = _rmsnorm(x2, fg_ref[...]) if final else x2


def _moe_dense_call(h, gate, wg, wu, wd, x1, g2, fg, final, tm, tiles_per_batch):
    n, d = x1.shape
    ne, _, ff = wg.shape
    row = lambda i, e: (i, 0)
    return pl.pallas_call(
        functools.partial(_moe_dense_kernel, final),
        grid=(n // tm, ne),
        in_specs=[
            pl.BlockSpec((tm, d), row),
            pl.BlockSpec((tm, N_EXPERTS), row),
            pl.BlockSpec((None, d, ff), lambda i, e: (e, 0, 0)),
            pl.BlockSpec((None, d, ff), lambda i, e: (e, 0, 0)),
            pl.BlockSpec((None, ff, d), lambda i, e: (e, 0, 0)),
            pl.BlockSpec((tm, d), row),
            _mod_spec(g2, tiles_per_batch),
            pl.BlockSpec((1, d), lambda i, e: (0, 0)),
        ],
        out_specs=pl.BlockSpec((tm, d), row),
        out_shape=jax.ShapeDtypeStruct((n, d), f32),
        scratch_shapes=[pltpu.VMEM((tm, d), f32)],
        compiler_params=_cparams("arbitrary", "arbitrary"),
        name="moe_dense",
    )(h, gate, wg, wu, wd, x1, g2, fg)


MOE_TILE = 512
MOE_ALIGN = 16
MOE_WINDOW = 80
MOE_ROWS = 1344
MOE_SELECT_ROWS = 672
MOE_COMBINE_ROWS = 512
MOE_VMEM_LIMIT = 56 * 1024 * 1024
MOE_GROUP = 4
MOE_STEP_EXPERTS = 2


def _plan_kernel(gate_ref, slot_ref, off_ref, cnt_ref):
    t = gate_ref.shape[1]
    gate = gate_ref[...]
    sel = jnp.where(gate > 0.0, 1.0, 0.0)
    before = lax.broadcasted_iota(jnp.int32, (t, t), 0) < lax.broadcasted_iota(jnp.int32, (t, t), 1)
    rank = jnp.dot(sel.astype(bf16), jnp.where(before, 1.0, 0.0).astype(bf16), preferred_element_type=f32)
    cnt = jnp.sum(sel, axis=1, keepdims=True)
    cpad = jnp.floor((cnt + (MOE_ALIGN - 1)) * (1.0 / MOE_ALIGN)) * MOE_ALIGN
    offs, run = [], jnp.zeros((1, 1), f32)
    for e in range(N_EXPERTS):
        offs.append(run)
        run = run + cpad[e:e + 1]
    off = jnp.concatenate(offs, axis=0)
    dest = off + rank
    seen = jnp.zeros((1, t), f32)
    dest_a = dest_b = w_a = w_b = jnp.zeros((1, t), f32)
    n_a = n_b = jnp.zeros((1, t), f32)
    for e in range(N_EXPERTS):
        s_e = sel[e:e + 1]
        is_a = s_e * jnp.where(seen == 0.0, 1.0, 0.0)
        is_b = s_e * jnp.where(seen == 1.0, 1.0, 0.0)
        dest_a, w_a, n_a = dest_a + is_a * dest[e:e + 1], w_a + is_a * gate[e:e + 1], n_a + is_a
        dest_b, w_b, n_b = dest_b + is_b * dest[e:e + 1], w_b + is_b * gate[e:e + 1], n_b + is_b
        seen = seen + s_e
    dest_a = jnp.where(n_a > 0.0, dest_a, -1.0)
    dest_b = jnp.where(n_b > 0.0, dest_b, -1.0)
    slot_ref[...] = jnp.concatenate([dest_a, dest_b, w_a, w_b, jnp.zeros((4, t), f32)], axis=0)
    off_ref[...] = jnp.broadcast_to(off, (N_EXPERTS, 128))
    cnt_ref[...] = jnp.broadcast_to(cnt, (N_EXPERTS, 128))


def _plan_call(gate_t):
    n = gate_t.shape[1]
    nt = n // MOE_TILE
    slots, off, cnt = pl.pallas_call(
        _plan_kernel,
        grid=(nt,),
        in_specs=[pl.BlockSpec((N_EXPERTS, MOE_TILE), lambda i: (0, i))],
        out_specs=[pl.BlockSpec((8, MOE_TILE), lambda i: (0, i)),
                   pl.BlockSpec((None, N_EXPERTS, 128), lambda i: (i, 0, 0)),
                   pl.BlockSpec((None, N_EXPERTS, 128), lambda i: (i, 0, 0))],
        out_shape=[jax.ShapeDtypeStruct((8, n), f32), jax.ShapeDtypeStruct((nt, N_EXPERTS, 128), f32),
                   jax.ShapeDtypeStruct((nt, N_EXPERTS, 128), f32)],
        compiler_params=_cparams("arbitrary"),
        name="moe_plan",
    )(gate_t)
    to_smem = lambda a: a[:, :, 0].astype(jnp.int32).reshape(nt * N_EXPERTS)
    return slots, to_smem(off), to_smem(cnt)


def _moe_sparse_kernel(off_ref, cnt_ref, h_ref, srow_ref, scol_ref, wg_ref, wu_ref, wd_ref, y_ref, xs_ref, ys_ref):
    i, e = pl.program_id(0), pl.program_id(1)
    t = MOE_TILE
    nt = h_ref.shape[0] // t

    @pl.when(e == 0)
    def _():
        for s in range(nt):
            tok = slice(s * t, (s + 1) * t)
            dest_a, dest_b = srow_ref[0:1, tok], srow_ref[1:2, tok]
            for c in range(MOE_ROWS // MOE_SELECT_ROWS):
                r = (lax.broadcasted_iota(jnp.int32, (MOE_SELECT_ROWS, t), 0) + c * MOE_SELECT_ROWS).astype(f32)
                sel = (jnp.where(r == dest_a, 1.0, 0.0) + jnp.where(r == dest_b, 1.0, 0.0)).astype(bf16)
                rows = slice(c * MOE_SELECT_ROWS, (c + 1) * MOE_SELECT_ROWS)
                xs_ref[s, rows, :] = jnp.dot(sel, h_ref[tok, :], preferred_element_type=f32).astype(bf16)
        ys_ref[...] = jnp.zeros_like(ys_ref)

    for k in range(MOE_STEP_EXPERTS):
        ex = e * MOE_STEP_EXPERTS + k
        offs = [off_ref[(i * nt + s) * N_EXPERTS + ex] for s in range(nt)]
        nwin = [(cnt_ref[(i * nt + s) * N_EXPERTS + ex] + MOE_WINDOW - 1) // MOE_WINDOW for s in range(nt)]

        def window(j, carry, k=k, offs=offs, nwin=nwin):
            starts = [pl.multiple_of(offs[s] + jnp.maximum(jnp.minimum(j, nwin[s] - 1), 0) * MOE_WINDOW, MOE_ALIGN)
                      for s in range(nt)]
            xw = jnp.concatenate([xs_ref[s, pl.ds(starts[s], MOE_WINDOW), :] for s in range(nt)], axis=0)
            hid = _silu(jnp.dot(xw, wg_ref[k], preferred_element_type=f32)) * jnp.dot(xw, wu_ref[k], preferred_element_type=f32)
            yw = jnp.dot(hid.astype(bf16), wd_ref[k], preferred_element_type=f32).astype(bf16)
            for s in range(nt):
                ys_ref[s, pl.ds(starts[s], MOE_WINDOW), :] = yw[s * MOE_WINDOW:(s + 1) * MOE_WINDOW]
            return carry

        lax.fori_loop(0, functools.reduce(jnp.maximum, nwin), window, 0)

    @pl.when(e == pl.num_programs(1) - 1)
    def _():
        for s in range(nt):
            for c in range(t // MOE_COMBINE_ROWS):
                rows = slice(s * t + c * MOE_COMBINE_ROWS, s * t + (c + 1) * MOE_COMBINE_ROWS)
                sc = scol_ref[rows, :]
                r = lax.broadcasted_iota(jnp.int32, (MOE_COMBINE_ROWS, MOE_ROWS), 1).astype(f32)
                wsel = jnp.where(r == sc[:, 0:1], sc[:, 2:3], 0.0) + jnp.where(r == sc[:, 1:2], sc[:, 3:4], 0.0)
                y_ref[rows, :] = jnp.dot(wsel.astype(bf16), ys_ref[s], preferred_element_type=f32).astype(y_ref.dtype)


def _moe_sparse_call(h, gate_t, wg, wu, wd):
    n, d = h.shape
    ne, _, ff = wg.shape
    t = MOE_TILE
    tg = MOE_GROUP * t
    assert MOE_ROWS >= 2 * t + (N_EXPERTS - 1) * (MOE_ALIGN - 1) + MOE_WINDOW and MOE_ROWS % MOE_SELECT_ROWS == 0
    slots, off, cnt = _plan_call(gate_t)
    row = lambda i, e, *_: (i, 0)
    grid_spec = pltpu.PrefetchScalarGridSpec(
        num_scalar_prefetch=2,
        grid=(n // tg, ne // MOE_STEP_EXPERTS),
        in_specs=[
            pl.BlockSpec((tg, d), row),
            pl.BlockSpec((8, tg), lambda i, e, *_: (0, i)),
            pl.BlockSpec((tg, 8), row),
            pl.BlockSpec((MOE_STEP_EXPERTS, d, ff), lambda i, e, *_: (e, 0, 0)),
            pl.BlockSpec((MOE_STEP_EXPERTS, d, ff), lambda i, e, *_: (e, 0, 0)),
            pl.BlockSpec((MOE_STEP_EXPERTS, ff, d), lambda i, e, *_: (e, 0, 0)),
        ],
        out_specs=pl.BlockSpec((tg, d), row),
        scratch_shapes=[pltpu.VMEM((MOE_GROUP, MOE_ROWS, d), bf16), pltpu.VMEM((MOE_GROUP, MOE_ROWS, d), bf16)],
    )
    return pl.pallas_call(
        _moe_sparse_kernel,
        grid_spec=grid_spec,
        out_shape=jax.ShapeDtypeStruct((n, d), bf16),
        compiler_params=pltpu.CompilerParams(dimension_semantics=("arbitrary", "arbitrary"),
                                             vmem_limit_bytes=MOE_VMEM_LIMIT),
        name="moe_sparse",
    )(off, cnt, h, slots, slots.T, wg, wu, wd)


def _residual_norm_kernel(x1_ref, y_ref, g2_ref, fg_ref, out_ref):
    out_ref[...] = _rmsnorm(x1_ref[...] + g2_ref[...] * y_ref[...].astype(f32), fg_ref[...])


def _residual_norm_call(x1, y, g2, fg, tm, tiles_per_batch):
    n, d = x1.shape
    row = lambda i: (i, 0)
    return pl.pallas_call(
        _residual_norm_kernel,
        grid=(n // tm,),
        in_specs=[pl.BlockSpec((tm, d), row), pl.BlockSpec((tm, d), row), _mod_spec(g2, tiles_per_batch),
                  pl.BlockSpec((1, d), lambda i: (0, 0))],
        out_specs=pl.BlockSpec((tm, d), row),
        out_shape=jax.ShapeDtypeStruct((n, d), f32),
        compiler_params=_cparams("arbitrary"),
        name="residual_norm",
    )(x1, y, g2, fg)


def _reorder_w_in(w_in):
    o = np.cumsum((0, QA, QA, VA, GATE_RANK, VA, QD, QD, QD))
    pad = jnp.zeros(w_in.shape[:-1] + (128 - GATE_RANK,), w_in.dtype)
    parts = [w_in[..., o[0]:o[3]], w_in[..., o[4]:o[5]], w_in[..., o[3]:o[4]], pad, w_in[..., o[5]:o[8]]]
    return jnp.concatenate(parts, axis=-1).astype(bf16)


def _trunk(x, mods, prompt, gla_state, cache_k, cache_v, p):
    nb, s, d = x.shape
    n = nb * s
    depth = p["w_in"].shape[0]
    sh1, sc1, g1, sh2, sc2, g2 = mods
    if prompt:
        tm = 512
        tpb = s // tm
        keep = min(2048, s)
        keep_tiles = keep // tm
    else:
        tm, tpb, keep_tiles = n, 0, 0
    x2d = x.reshape(n, d)
    new_s, new_k, new_v = [], [], []
    resid = None
    for l in range(depth):
        outs = _inproj_call(x2d, sc1[l], sh1[l], p["norm1_g"][l], p["w_in"][l], tm, tpb, keep_tiles, resid)
        if resid is not None:
            x2d, outs = outs[0], outs[1:]
        a, q, k, v = outs[:4]
        if prompt:
            oa, st = _gla_prompt_call(a, p["wg"][l], p["bg"][l], p["gn"][l], nb, tm)
            new_s.append(_state_from_blockdiag(st))
            new_k.append(outs[4])
            new_v.append(outs[5])
            qkv = {1: (q, k, v)}
            for j, dil in enumerate(STRIDED_DILS):
                qkv[dil] = outs[6 + 3 * j:9 + 3 * j]
            os, ls = zip(*[_dil_prompt_call(*qkv[dil], p["bias_p"][c], nb, dil) for c, (_, dil) in enumerate(DIL_CONFIGS)])
            md = _combine_call(os, ls, tm)
        else:
            oa, st = _gla_sample_call(a, p["wg"][l], p["bg"][l], p["gn"][l], gla_state[l])
            oa = oa.reshape(n, VA)
            new_s.append(st.reshape(nb, H_A, DK_A, DV_A))
            new_k.append(k.reshape(nb, s, H_D, DH_D))
            new_v.append(v.reshape(nb, s, H_D, DH_D))
            md = _dil_sample_call(q, k, v, cache_k, cache_v, l, *p["bias_s"])
        x1, h2, gate_t = _wo_call(x2d, oa, md, p["w_o"][l], g1[l], p["norm2_g"][l], sc2[l], sh2[l],
                                  p["rwt"], p["rb"], tm, tpb)
        experts = (p["moe_wg"][l], p["moe_wu"][l], p["moe_wd"][l])
        if prompt:
            x2d, resid = x1, (_moe_sparse_call(h2, gate_t, *experts), g2[l])
        else:
            x2d = _moe_dense_call(h2, gate_t.T, *experts, x1, g2[l], p["final_g"], l == depth - 1, tm, tpb)
    new_k, new_v = jnp.stack(new_k), jnp.stack(new_v)
    if prompt:
        x2d = _residual_norm_call(x2d, *resid, p["final_g"], tm, tpb)
        new_k = new_k[:, :, 1:].reshape(depth, nb, keep, H_D, DH_D)
        new_v = new_v[:, :, 1:].reshape(depth, nb, keep, H_D, DH_D)
    return x2d.reshape(nb, s, d), jnp.stack(new_s), new_k, new_v


def kernel(x_prompt, x_sample, state_gla, cache_win_k, cache_win_v, c_prompt, c_sample, w_ada, b_ada, norm1_g, norm2_g, w_in, w_gate_up, b_gate, gla_norm_g, w_o, rel_bias, router_w, router_b, moe_w_gate, moe_w_up, moe_w_down, final_norm_g):
    nbp, seq, d = x_prompt.shape
    nbs, dec_seq, _ = x_sample.shape
    depth = w_in.shape[0]
    assert dec_seq == 1 and cache_win_k.shape[2] == DIL_CONFIGS[-1][0], "sample path: one new token over a full window"
    assert seq % (DIL_CONFIGS[-1][0] * ATTN_QUERY_BLOCKS) == 0, "prompt length must split into whole attention steps"

    c_all = jnp.concatenate([c_prompt, c_sample], axis=0)
    c_all = jnp.pad(c_all, ((0, -c_all.shape[0] % 8), (0, 0)))
    mod = _ada_call(c_all, w_ada, b_ada)
    mods_p = [m[:, :nbp, None, :] for m in jnp.split(mod, 6, axis=-1)]
    mods_s = [m[:, None, nbp:nbp + nbs, :] for m in jnp.split(mod, 6, axis=-1)]

    p = {
        "w_in": _reorder_w_in(w_in),
        "norm1_g": norm1_g.reshape(depth, 1, d),
        "norm2_g": norm2_g.reshape(depth, 1, d),
        "wg": jnp.pad(w_gate_up, ((0, 0), (0, 128 - GATE_RANK), (0, 0))),
        "bg": b_gate.reshape(depth, 1, QA),
        "gn": gla_norm_g.reshape(depth, 1, VA),
        "w_o": w_o.astype(bf16),
        "rwt": router_w.T,
        "rb": router_b.reshape(N_EXPERTS, 1),
        "moe_wg": moe_w_gate.astype(bf16),
        "moe_wu": moe_w_up.astype(bf16),
        "moe_wd": moe_w_down.astype(bf16),
        "final_g": final_norm_g.reshape(1, d),
        "bias_p": [_prompt_bias(rel_bias, dil) for (_, dil) in DIL_CONFIGS],
        "bias_s": _sample_bias(rel_bias, cache_win_k.shape[2]),
    }
    y_p, s_p, k_p, v_p = _trunk(x_prompt, mods_p, True, None, None, None, p)
    y_s, s_s, k_s, v_s = _trunk(x_sample, mods_s, False, state_gla, cache_win_k, cache_win_v, p)
    return (y_p, y_s, s_p, k_p, v_p, s_s, k_s, v_s)
```

```python
import functools
import math

import jax
import jax.numpy as jnp
import numpy as np
from jax import lax
from jax.experimental import pallas as pl
from jax.experimental.pallas import tpu as pltpu

f32 = jnp.float32
bf16 = jnp.bfloat16

H_A, DK_A, DV_A = 4, 32, 64
GATE_RANK = 16
GATE_TAU = 16.0
H_D, DH_D = 12, 64
DIL_CONFIGS = ((128, 1), (512, 4), (2048, 16))
N_BUCKETS = 32
MAX_DISTANCE = 2048
N_EXPERTS = 16
N_EXPERT_GROUPS = 4
EPS = 1e-6

QA = H_A * DK_A
VA = H_A * DV_A
QD = H_D * DH_D
A_WIDTH = 2 * QA + 2 * VA + 128
W_IN_COLS = A_WIDTH + 3 * QD
BLK = 128
N_PAIRS = H_D // 2
ATTN_QUERY_BLOCKS = 4
NEG = -1e30
GLA_SUB = 16
GLA_ROWS = 128
EXP_CLAMP = 80.0
VMEM_LIMIT = 48 * 1024 * 1024

HIGHEST = lax.Precision.HIGHEST
NT = (((1,), (1,)), ((), ()))
TN = (((0,), (0,)), ((), ()))


def _cparams(*sem):
    return pltpu.CompilerParams(dimension_semantics=sem, vmem_limit_bytes=VMEM_LIMIT)


def _resident(shape, index_map):
    return pl.BlockSpec(shape, index_map, pipeline_mode=pl.Buffered(1))


def _silu(x):
    return x * jax.nn.sigmoid(x)


def _split_bf16(x, terms):
    parts = []
    for _ in range(terms):
        p = x.astype(bf16)
        parts.append(p)
        x = x - p.astype(f32)
    return parts


def _rmsnorm(x, g):
    return x * lax.rsqrt(jnp.mean(x * x, axis=-1, keepdims=True) + EPS) * g


def _ada_kernel(c_ref, w_ref, b_ref, o_ref):
    cs = _silu(c_ref[...]).astype(bf16)
    o_ref[...] = jnp.dot(cs, w_ref[...].astype(bf16), preferred_element_type=f32) + b_ref[...]


def _ada_call(c, w_ada, b_ada):
    depth, d, n6 = w_ada.shape
    m = c.shape[0]
    tn = 1536
    return pl.pallas_call(
        _ada_kernel,
        grid=(depth, n6 // tn),
        in_specs=[
            pl.BlockSpec((m, d), lambda l, j: (0, 0)),
            pl.BlockSpec((None, d, tn), lambda l, j: (l, 0, j)),
            pl.BlockSpec((None, 1, tn), lambda l, j: (l, 0, j)),
        ],
        out_specs=pl.BlockSpec((None, m, tn), lambda l, j: (l, 0, j)),
        out_shape=jax.ShapeDtypeStruct((depth, m, n6), f32),
        compiler_params=_cparams("arbitrary", "arbitrary"),
        name="adaln",
    )(c, w_ada, b_ada.reshape(depth, 1, n6))


def _inproj_kernel(prompt, resid, x_ref, *refs):
    x = x_ref[...]
    if resid:
        y_ref, g2_ref, x2_ref = refs[0], refs[1], refs[6]
        x = x + g2_ref[...] * y_ref[...].astype(f32)
        x2_ref[...] = x
        refs = refs[2:6] + refs[7:]
    sc_ref, sh_ref, g_ref, w_ref = refs[:4]
    outs = refs[4:]
    h = _rmsnorm(x, g_ref[...])
    h = (h * (1.0 + sc_ref[...]) + sh_ref[...]).astype(bf16)

    def proj(lo, hi):
        return jnp.dot(h, w_ref[:, lo:hi], preferred_element_type=f32)

    a_ref, q_ref, k_ref, v_ref = outs[:4]
    a_ref[...] = proj(0, A_WIDTH)
    q = proj(A_WIDTH, A_WIDTH + QD)
    k = proj(A_WIDTH + QD, A_WIDTH + 2 * QD)
    v = proj(A_WIDTH + 2 * QD, A_WIDTH + 3 * QD)
    if prompt:
        n_cls = len(STRIDED_DILS)
        kw_ref, vw_ref = outs[4:6]
        cls_refs = outs[6:6 + 3 * n_cls]
        scratch = outs[6 + 3 * n_cls:]
        q = q * (DH_D ** -0.5)
        kw_ref[...] = k
        vw_ref[...] = v
        for j, (val, nat_ref) in enumerate(((q, q_ref), (k, k_ref), (v, v_ref))):
            nat_ref[...] = val.astype(bf16)
            _to_class_major(val, scratch[0], scratch[1], [cls_refs[3 * c + j] for c in range(n_cls)])
    else:
        q_ref[...] = q
        k_ref[...] = k
        v_ref[...] = v


STRIDED_DILS = tuple(dil for (_, dil) in DIL_CONFIGS if dil > 1)
LANE_TILES = QD // 128


def _to_class_major(val, planes, class_planes, dst_refs):
    rows = val.shape[0]
    d0, d1 = STRIDED_DILS
    f = d1 // d0
    dst0, dst1 = dst_refs
    for c in range(LANE_TILES):
        planes[c] = val[:, c * 128:(c + 1) * 128]
    for c in range(LANE_TILES):
        for r in range(d0):
            x = planes[c, pl.ds(r, rows // d0, stride=d0), :]
            class_planes[r * LANE_TILES + c] = x
            lo = r * QD + c * 128
            dst0[:, lo:lo + 128] = x.astype(dst0.dtype)
    for c in range(LANE_TILES):
        for r in range(d0):
            for r2 in range(f):
                lo = (d0 * r2 + r) * QD + c * 128
                dst1[:, lo:lo + 128] = class_planes[r * LANE_TILES + c, pl.ds(r2, rows // d1, stride=f), :].astype(dst1.dtype)


def _from_class_major(src_ref, planes, dil, width, class_planes=None):
    rows = planes.shape[1]
    tiles = width // 128
    if class_planes is None:
        for c in range(tiles):
            for r in range(dil):
                lo = r * width + c * 128
                planes[c, pl.ds(r, rows // dil, stride=dil), :] = src_ref[:, lo:lo + 128].astype(f32)
        return
    d0 = STRIDED_DILS[0]
    f = dil // d0
    for c in range(tiles):
        for r in range(d0):
            for r2 in range(f):
                lo = (d0 * r2 + r) * width + c * 128
                class_planes[r * tiles + c, pl.ds(r2, rows // dil, stride=f), :] = src_ref[:, lo:lo + 128].astype(f32)
    for c in range(tiles):
        for r in range(d0):
            planes[c, pl.ds(r, rows // d0, stride=d0), :] = class_planes[r * tiles + c]


def _mod_spec(arr, tiles_per_batch):
    rows, d = arr.shape[1], arr.shape[2]
    if tiles_per_batch:
        return pl.BlockSpec((None, rows, d), lambda i, *_: (i // tiles_per_batch, 0, 0))
    return pl.BlockSpec((None, rows, d), lambda i, *_: (0, 0, 0))


def _inproj_call(x2d, sc, sh, g, w, tm, tiles_per_batch, keep_tiles, resid=None):
    n, d = x2d.shape
    prompt = bool(tiles_per_batch)
    qkv_dt = bf16 if prompt else f32
    row = lambda i: (i, 0)
    out_specs = [pl.BlockSpec((tm, A_WIDTH), row)] + [pl.BlockSpec((tm, QD), row)] * 3
    out_shape = [jax.ShapeDtypeStruct((n, A_WIDTH), f32)] + [jax.ShapeDtypeStruct((n, QD), qkv_dt)] * 3
    scratch = []
    if prompt:
        nb = n // (tm * tiles_per_batch)
        first_keep = tiles_per_batch - keep_tiles

        def win(i):
            return (i // tiles_per_batch, jnp.maximum(i % tiles_per_batch - first_keep + 1, 0), 0, 0)

        out_specs += [pl.BlockSpec((None, None, tm, QD), win)] * 2
        out_shape += [jax.ShapeDtypeStruct((nb, keep_tiles + 1, tm, QD), f32)] * 2
        for dil in STRIDED_DILS:
            out_specs += [pl.BlockSpec((tm // dil, dil * QD), row)] * 3
            out_shape += [jax.ShapeDtypeStruct((n // dil, dil * QD), bf16)] * 3
        d0, d1 = STRIDED_DILS
        assert d1 % d0 == 0
        scratch = [pltpu.VMEM((LANE_TILES, tm, 128), f32), pltpu.VMEM((d0 * LANE_TILES, tm // d0, 128), f32)]
    in_specs = [
        _mod_spec(sc, tiles_per_batch),
        _mod_spec(sh, tiles_per_batch),
        pl.BlockSpec((1, d), lambda i: (0, 0)),
        _resident((d, W_IN_COLS), lambda i: (0, 0)),
    ]
    args = (sc, sh, g, w)
    if resid is not None:
        in_specs = [pl.BlockSpec((tm, d), row), _mod_spec(resid[1], tiles_per_batch)] + in_specs
        args = resid + args
        out_specs = [pl.BlockSpec((tm, d), row)] + out_specs
        out_shape = [jax.ShapeDtypeStruct((n, d), f32)] + out_shape
    return pl.pallas_call(
        functools.partial(_inproj_kernel, prompt, resid is not None),
        grid=(n // tm,),
        in_specs=[pl.BlockSpec((tm, d), row)] + in_specs,
        out_specs=out_specs,
        out_shape=out_shape,
        scratch_shapes=scratch,
        compiler_params=_cparams("arbitrary"),
        name="inproj_prompt" if prompt else "inproj_sample",
    )(x2d, *args)


def _log_gate(glr, wg_ref, bg_ref):
    x = jnp.dot(glr.astype(bf16), wg_ref[...].astype(bf16), preferred_element_type=f32) + bg_ref[...]
    return (jnp.minimum(x, 0.0) - jnp.log1p(jnp.exp(-jnp.abs(x)))) * (1.0 / GATE_TAU)


def _gla_prompt_kernel(a_ref, wg_ref, bg_ref, gn_ref, o_ref, s_ref, st_ref):
    R, C = GLA_ROWS, GLA_SUB

    @pl.when(pl.program_id(1) == 0)
    def _():
        st_ref[...] = jnp.zeros_like(st_ref)

    def iota(shape, dim, shift=0):
        return lax.shift_right_logical(lax.broadcasted_iota(jnp.int32, shape, dim), shift)

    sub, lk, lv = int(math.log2(C)), int(math.log2(DK_A)), int(math.log2(DV_A))
    ri = lax.bitwise_and(iota((H_A * R, R), 0), R - 1)
    ci = iota((H_A * R, R), 1)
    same4 = lax.shift_right_logical(ri, sub) == lax.shift_right_logical(ci, sub)
    causal4 = same4 & (ci <= ri)
    sums = jnp.concatenate([jnp.where(causal4, 1.0, 0.0)[0:R], jnp.where(same4, 1.0, 0.0)[0:R]], axis=0).astype(bf16)
    lane_q = iota((1, QA), 1, lk)
    lane_v = iota((1, VA), 1, lv)
    bd = iota((QA, VA), 0, lk) == iota((QA, VA), 1, lv)
    hmean = jnp.where(iota((VA, VA), 0, lv) == iota((VA, VA), 1, lv), 1.0 / DV_A, 0.0).astype(bf16)
    row_sub = iota((R, 1), 0, sub)
    sub_mean = jnp.where(iota((R, 128), 0, sub) == iota((R, 128), 1), 1.0 / C, 0.0).astype(bf16)
    nsub = R // C

    def prepare(c):
        rows = slice(c * R, (c + 1) * R)
        blk = a_ref[rows, :]
        q = blk[:, 0:QA] * (DK_A ** -0.5)
        k = blk[:, QA:2 * QA]
        v = blk[:, 2 * QA:2 * QA + VA]
        r = blk[:, 2 * QA + VA:2 * QA + 2 * VA]
        g = _log_gate(blk[:, 2 * QA + 2 * VA:], wg_ref, bg_ref)
        cums = sum(jnp.dot(sums, part, preferred_element_type=f32) for part in _split_bf16(g, 3))
        b, bt = cums[0:R], cums[R:2 * R]
        qb = q * jnp.exp(b)
        kd = (k * jnp.exp(bt - b)).astype(bf16)
        ki = (k * jnp.exp(jnp.minimum(-b, EXP_CLAMP))).astype(bf16)
        vb = v.astype(bf16)
        qs = jnp.concatenate([jnp.where(lane_q == h, qb, 0.0) for h in range(H_A)], axis=0).astype(bf16)
        att = lax.dot_general(qs, ki, NT, preferred_element_type=f32)
        att = jnp.where(causal4, att, 0.0).astype(bf16)
        res = jnp.dot(att, vb, preferred_element_type=f32)
        o = jnp.where(lane_v == 0, res[0:R], 0.0)
        for h in range(1, H_A):
            o = o + jnp.where(lane_v == h, res[h * R:(h + 1) * R], 0.0)
        qbb = qb.astype(bf16)
        zero = jnp.zeros((), bf16)
        v_exp = jnp.concatenate([jnp.where(row_sub == i, vb, zero) for i in range(nsub)], axis=1)
        kv_all = lax.dot_general(kd, v_exp, TN, preferred_element_type=f32)
        decay = sum(lax.dot_general(part, sub_mean, TN, preferred_element_type=f32)
                    for part in _split_bf16(jnp.exp(bt), 3))
        q_exp = jnp.concatenate([jnp.where(row_sub == i, qbb, zero) for i in range(nsub)], axis=1)
        return o, kv_all, decay, q_exp, _silu(r)

    def finish(c, st, o, kv_all, decay, q_exp, gate):
        states = []
        for i in range(nsub):
            states.append(st.astype(bf16))
            st = st * decay[:, i:i + 1] + jnp.where(bd, kv_all[:, i * VA:(i + 1) * VA], 0.0)
        o = o + jnp.dot(q_exp, jnp.concatenate(states, axis=0), preferred_element_type=f32)
        ms = sum(jnp.dot(part, hmean, preferred_element_type=f32) for part in _split_bf16(o * o, 2))
        y = o * lax.rsqrt(ms + EPS) * gn_ref[...]
        o_ref[c * R:(c + 1) * R, :] = (y * gate).astype(o_ref.dtype)
        return st

    nblk = a_ref.shape[0] // R
    prepared = [prepare(c) for c in range(nblk)]
    st = st_ref[...]
    for c in range(nblk):
        st = finish(c, st, *prepared[c])
    st_ref[...] = st
    s_ref[...] = st


def _gla_prompt_call(a, wg, bg, gn, nb, tm):
    n = a.shape[0]
    tpb = n // nb // tm
    return pl.pallas_call(
        _gla_prompt_kernel,
        grid=(nb, tpb),
        in_specs=[
            pl.BlockSpec((tm, A_WIDTH), lambda b, j: (b * tpb + j, 0)),
            pl.BlockSpec((128, QA), lambda b, j: (0, 0)),
            pl.BlockSpec((1, QA), lambda b, j: (0, 0)),
            pl.BlockSpec((1, VA), lambda b, j: (0, 0)),
        ],
        out_specs=[
            pl.BlockSpec((tm, VA), lambda b, j: (b * tpb + j, 0)),
            pl.BlockSpec((None, QA, VA), lambda b, j: (b, 0, 0)),
        ],
        out_shape=[jax.ShapeDtypeStruct((n, VA), bf16), jax.ShapeDtypeStruct((nb, QA, VA), f32)],
        scratch_shapes=[pltpu.VMEM((QA, VA), f32)],
        compiler_params=_cparams("arbitrary", "arbitrary"),
        name="gla_prompt",
    )(a, wg, bg, gn)


def _state_from_blockdiag(st):
    b = st.shape[0]
    st5 = st.reshape(b, H_A, DK_A, H_A, DV_A)
    return jnp.stack([st5[:, h, :, h, :] for h in range(H_A)], axis=1)


def _gla_sample_kernel(a_ref, at_ref, wgt_ref, bgt_ref, gn_ref, s_ref, o_ref, so_ref):
    nb = a_ref.shape[0]
    qt = at_ref[0:QA, :] * (DK_A ** -0.5)
    kt = at_ref[QA:2 * QA, :]
    xt = jnp.dot(wgt_ref[...].astype(bf16), at_ref[2 * QA + 2 * VA:, :].astype(bf16), preferred_element_type=f32) + bgt_ref[...]
    et = jnp.exp((jnp.minimum(xt, 0.0) - jnp.log1p(jnp.exp(-jnp.abs(xt)))) * (1.0 / GATE_TAU))
    gn = gn_ref[...]
    for t in range(nb):
        row = a_ref[t:t + 1, :]
        vt = jnp.concatenate(
            [jnp.broadcast_to(row[:, 2 * QA + h * DV_A:2 * QA + (h + 1) * DV_A], (DK_A, DV_A)) for h in range(H_A)], axis=0)
        s_new = s_ref[t] * et[:, t:t + 1] + kt[:, t:t + 1] * vt
        so_ref[t] = s_new
        o = jnp.sum((qt[:, t:t + 1] * s_new).reshape(H_A, DK_A, DV_A), axis=1)
        y = _rmsnorm(o, gn)
        r = jnp.concatenate([row[:, 2 * QA + VA + h * DV_A:2 * QA + VA + (h + 1) * DV_A] for h in range(H_A)], axis=0)
        o_ref[t] = y * _silu(r)


def _gla_sample_call(a, wg, bg, gn, state):
    nb = a.shape[0]
    return pl.pallas_call(
        _gla_sample_kernel,
        out_shape=[jax.ShapeDtypeStruct((nb, H_A, DV_A), f32), jax.ShapeDtypeStruct((nb, QA, DV_A), f32)],
        compiler_params=pltpu.CompilerParams(vmem_limit_bytes=VMEM_LIMIT),
        name="gla_sample",
    )(a, a.T, wg.T, bg.reshape(QA, 1), gn.reshape(H_A, DV_A), state.reshape(nb, QA, DV_A))


def _rel_bucket_np(dist):
    max_exact = N_BUCKETS // 2
    d = np.maximum(dist, 0)
    df = np.maximum(d, 1).astype(np.float32)
    large = max_exact + (np.log(df / np.float32(max_exact)) / np.float32(math.log(MAX_DISTANCE / max_exact))
                         * np.float32(N_BUCKETS - max_exact)).astype(np.int32)
    large = np.minimum(large, N_BUCKETS - 1)
    return np.where(d < max_exact, d, large).astype(np.int32)


def _bias_lookup(rel_bias, buckets):
    onehot = np.zeros((buckets.size, N_BUCKETS), np.float32)
    onehot[np.arange(buckets.size), buckets.reshape(-1)] = 1.0
    out = jnp.dot(jnp.asarray(onehot), rel_bias.astype(f32), precision=HIGHEST)
    return out.reshape(buckets.shape + (rel_bias.shape[-1],))


def _prompt_bias(rel_bias, dil):
    i = np.arange(BLK)[:, None]
    j = np.arange(2 * BLK)[None, :]
    delta = BLK + i - j
    band = (delta >= 0) & (delta <= BLK)
    tbl = _bias_lookup(rel_bias, _rel_bucket_np(delta * dil))
    tbl = jnp.where(jnp.asarray(band)[:, :, None], tbl, NEG)
    first = jnp.where(jnp.asarray(j >= BLK)[:, :, None], tbl, NEG)
    pairs = lambda t: jnp.transpose(t, (2, 0, 1)).reshape(N_PAIRS, 2 * BLK, 2 * BLK)
    return jnp.concatenate([pairs(tbl), pairs(first)], axis=0)


def _dil_prompt_kernel(q_ref, kp_ref, kc_ref, vp_ref, vc_ref, bias_ref, o_ref, lse_ref):
    lane = lax.broadcasted_iota(jnp.int32, (1, 2 * DH_D), 1)
    lo = lane < DH_D
    first = jnp.where(pl.program_id(2) == 0, N_PAIRS, 0)
    lane_h = lax.broadcasted_iota(jnp.int32, (1, 128), 1)
    zero = jnp.zeros((), bf16)
    nq = q_ref.shape[0] // BLK
    lse_acc = [jnp.zeros((BLK, 128), f32) for _ in range(nq)]
    for p in range(N_PAIRS):
        cols = slice(p * 2 * DH_D, (p + 1) * 2 * DH_D)
        k_all = jnp.concatenate([kp_ref[:, cols], kc_ref[:, cols]], axis=0)
        v_all = jnp.concatenate([vp_ref[:, cols], vc_ref[:, cols]], axis=0)
        for qb in range(nq):
            rows = slice(qb * BLK, (qb + 1) * BLK)
            qp = q_ref[rows, cols]
            k2, v2 = k_all[qb * BLK:(qb + 2) * BLK], v_all[qb * BLK:(qb + 2) * BLK]
            variant = first if qb == 0 else 0
            halves = []
            for hh, qh in enumerate((jnp.where(lo, qp, zero), jnp.where(lo, zero, qp))):
                s = lax.dot_general(qh, k2, NT, preferred_element_type=f32) + bias_ref[variant + p, hh * BLK:(hh + 1) * BLK, :]
                m = jnp.max(s, axis=-1, keepdims=True)
                e = jnp.exp(s - m)
                den = jnp.sum(e, axis=-1, keepdims=True)
                halves.append(jnp.dot(e.astype(bf16), v2, preferred_element_type=f32) / den)
                lse_acc[qb] = jnp.where(lane_h == 2 * p + hh, m + jnp.log(den), lse_acc[qb])
            o_ref[rows, cols] = jnp.where(lo, halves[0], halves[1]).astype(o_ref.dtype)
    for qb in range(nq):
        lse_ref[qb * BLK:(qb + 1) * BLK, :] = lse_acc[qb]


def _dil_prompt_call(q, k, v, bias, nb, dil):
    rows = q.shape[0] // nb
    nq = ATTN_QUERY_BLOCKS
    nblk = rows // (nq * BLK)
    view = lambda t: t.reshape(nb, rows, t.shape[-1])
    cur = pl.BlockSpec((None, nq * BLK, QD), lambda b, r, i: (b, i, r))
    prev = pl.BlockSpec((None, BLK, QD), lambda b, r, i: (b, jnp.maximum(nq * i - 1, 0), r))
    o, lse = pl.pallas_call(
        _dil_prompt_kernel,
        grid=(nb, dil, nblk),
        in_specs=[cur, prev, cur, prev, cur, _resident((2 * N_PAIRS, 2 * BLK, 2 * BLK), lambda b, r, i: (0, 0, 0))],
        out_specs=[cur, pl.BlockSpec((None, nq * BLK, 128), lambda b, r, i: (b, i, r))],
        out_shape=[jax.ShapeDtypeStruct((nb, rows, dil * QD), bf16), jax.ShapeDtypeStruct((nb, rows, dil * 128), f32)],
        compiler_params=_cparams("arbitrary", "arbitrary", "arbitrary"),
        name=f"dilated_prompt_d{dil}",
    )(view(q), view(k), view(k), view(v), view(v), bias)
    return o.reshape(nb * rows, dil * QD), lse.reshape(nb * rows, dil * 128)


def _head_expand_np():
    e = np.zeros((128, QD), np.float32)
    for h in range(H_D):
        e[h, h * DH_D:(h + 1) * DH_D] = 1.0
    return e


def _expand(w, e_ref):
    hi = w.astype(bf16)
    lo = (w - hi.astype(f32)).astype(bf16)
    return jnp.dot(hi, e_ref[...], preferred_element_type=f32) + jnp.dot(lo, e_ref[...], preferred_element_type=f32)


def _mix_branches(o_refs, l_refs, e_ref, scratch):
    nbr = len(DIL_CONFIGS)
    o_planes, lses, used = [], [], 0
    for c, (_, dil) in enumerate(DIL_CONFIGS):
        if dil == 1:
            o_planes.append(None)
            lses.append(l_refs[c][...])
        else:
            op, lp = scratch[used], scratch[used + 1]
            used += 2
            ocp = lcp = None
            if dil != STRIDED_DILS[0]:
                ocp, lcp = scratch[used], scratch[used + 1]
                used += 2
            _from_class_major(o_refs[c], op, dil, QD, ocp)
            _from_class_major(l_refs[c], lp, dil, 128, lcp)
            o_planes.append(op)
            lses.append(lp[0])
    m = functools.reduce(jnp.maximum, lses)
    es = [jnp.exp(l - m) for l in lses]
    z = functools.reduce(lambda a, b: a + b, es)
    ws = [_expand(e / z, e_ref) for e in es]
    tiles = []
    for t in range(LANE_TILES):
        cols = slice(t * 128, (t + 1) * 128)
        acc = None
        for c in range(nbr):
            o = o_refs[c][:, cols].astype(f32) if o_planes[c] is None else o_planes[c][t]
            acc = ws[c][:, cols] * o if acc is None else acc + ws[c][:, cols] * o
        tiles.append(acc)
    return jnp.concatenate(tiles, axis=1)


def _mix_specs(tm):
    row = lambda i: (i, 0)
    o_specs = [pl.BlockSpec((tm // dil, dil * QD), row) for (_, dil) in DIL_CONFIGS]
    l_specs = [pl.BlockSpec((tm // dil, dil * 128), row) for (_, dil) in DIL_CONFIGS]
    scratch = []
    for (_, dil) in DIL_CONFIGS:
        if dil > 1:
            scratch += [pltpu.VMEM((LANE_TILES, tm, 128), f32), pltpu.VMEM((1, tm, 128), f32)]
            d0 = STRIDED_DILS[0]
            if dil != d0:
                scratch += [pltpu.VMEM((d0 * LANE_TILES, tm // d0, 128), f32), pltpu.VMEM((d0, tm // d0, 128), f32)]
    return o_specs + l_specs + [pl.BlockSpec((128, QD), lambda i: (0, 0))], scratch


SAMPLE_HEADS = 12


def _sample_bias(rel_bias, length):
    dist = length - np.arange(length)
    tbl = _bias_lookup(rel_bias, _rel_bucket_np(dist)).T
    rows = []
    for (_, dil) in DIL_CONFIGS:
        valid = (dist % dil == 0) & (dist <= BLK * dil)
        rows.append(jnp.where(jnp.asarray(valid)[None, :], tbl, NEG))
    b0 = _bias_lookup(rel_bias, _rel_bucket_np(np.zeros((1,), np.int64)))
    groups = H_D // SAMPLE_HEADS
    return (jnp.stack(rows).reshape(len(DIL_CONFIGS), groups, SAMPLE_HEADS, length),
            jnp.broadcast_to(b0.reshape(groups, SAMPLE_HEADS, 1), (groups, SAMPLE_HEADS, 128)))


def _dil_sample_kernel(qkv_ref, kt_ref, vt_ref, bias_ref, b0_ref, o_ref):
    nh = SAMPLE_HEADS
    eye = jnp.where(lax.broadcasted_iota(jnp.int32, (DH_D, DH_D), 0) == lax.broadcasted_iota(jnp.int32, (DH_D, DH_D), 1),
                    1.0, 0.0).astype(f32)
    cols = lax.dot_general(eye, qkv_ref[...], NT, precision=HIGHEST, preferred_element_type=f32)
    qc = cols[:, 0:nh] * (DH_D ** -0.5)
    lg = jnp.concatenate([jnp.sum(kt_ref[h] * qc[:, h:h + 1], axis=0, keepdims=True) for h in range(nh)], axis=0)
    l0 = jnp.concatenate([jnp.sum(qc[:, h:h + 1] * cols[:, nh + h:nh + h + 1], axis=0, keepdims=True) for h in range(nh)],
                         axis=0) + b0_ref[:, 0:1]
    ps, p0s, dens, lses = [], [], [], []
    for c in range(len(DIL_CONFIGS)):
        s = lg + bias_ref[c]
        m = jnp.maximum(jnp.max(s, axis=-1, keepdims=True), l0)
        p = jnp.exp(s - m)
        p0 = jnp.exp(l0 - m)
        den = jnp.sum(p, axis=-1, keepdims=True) + p0
        ps.append(p), p0s.append(p0), dens.append(den), lses.append(m + jnp.log(den))
    m = jnp.maximum(jnp.maximum(lses[0], lses[1]), lses[2])
    es = [jnp.exp(l - m) for l in lses]
    z = es[0] + es[1] + es[2]
    coef = [e / z / den for e, den in zip(es, dens)]
    pmix = coef[0] * ps[0] + coef[1] * ps[1] + coef[2] * ps[2]
    p0mix = coef[0] * p0s[0] + coef[1] * p0s[1] + coef[2] * p0s[2]
    lane = lax.broadcasted_iota(jnp.int32, (1, 128), 1)
    ocols = jnp.zeros((DH_D, 128), f32)
    for h in range(nh):
        oc = jnp.sum(vt_ref[h] * pmix[h:h + 1, :], axis=1, keepdims=True)
        oc = oc + p0mix[h:h + 1, 0:1] * cols[:, 2 * nh + h:2 * nh + h + 1]
        ocols = jnp.where(lane == h, oc, ocols)
    o = lax.dot_general(ocols, eye, TN, precision=HIGHEST, preferred_element_type=f32)
    o_ref[...] = o[0:nh]


def _dil_sample_call(q, kn, vn, cache_k, cache_v, layer, bias, b0):
    nb = q.shape[0]
    depth, _, length = cache_k.shape[:3]
    groups = H_D // SAMPLE_HEADS
    hd = pl.BlockSpec((None, None, SAMPLE_HEADS, DH_D), lambda b, g: (b, g, 0, 0))
    cache_spec = pl.BlockSpec((None, None, None, SAMPLE_HEADS, DH_D, length), lambda b, g: (layer, b, g, 0, 0, 0))
    view = lambda c: jnp.transpose(c, (0, 1, 3, 4, 2)).reshape(depth, nb, groups, SAMPLE_HEADS, DH_D, length)
    to_hd = lambda t: t.reshape(nb, groups, SAMPLE_HEADS, DH_D)
    qkv = jnp.concatenate([to_hd(q), to_hd(kn), to_hd(vn), jnp.zeros_like(to_hd(q))], axis=2)
    out = pl.pallas_call(
        _dil_sample_kernel,
        grid=(nb, groups),
        in_specs=[pl.BlockSpec((None, None, 4 * SAMPLE_HEADS, DH_D), lambda b, g: (b, g, 0, 0)), cache_spec, cache_spec,
                  pl.BlockSpec((len(DIL_CONFIGS), None, SAMPLE_HEADS, length), lambda b, g: (0, g, 0, 0)),
                  pl.BlockSpec((None, SAMPLE_HEADS, 128), lambda b, g: (g, 0, 0))],
        out_specs=hd,
        out_shape=jax.ShapeDtypeStruct((nb, groups, SAMPLE_HEADS, DH_D), f32),
        compiler_params=_cparams("arbitrary", "arbitrary"),
        name="dilated_sample",
    )(qkv, view(cache_k), view(cache_v), bias, b0)
    return out.reshape(nb, QD)


def _route(logits_t, rb):
    per_group = N_EXPERTS // N_EXPERT_GROUPS
    sc = jax.nn.sigmoid(logits_t)
    sel = sc + rb
    sel_r = [sel[e:e + 1, :] for e in range(N_EXPERTS)]
    sc_r = [sc[e:e + 1, :] for e in range(N_EXPERTS)]

    def beats(j, i, vals):
        return (vals[j] >= vals[i]) if j < i else (vals[j] > vals[i])

    gs = []
    for g in range(N_EXPERT_GROUPS):
        a, b, c, d = sel_r[per_group * g:per_group * (g + 1)]
        m1, n1, m2, n2 = jnp.maximum(a, b), jnp.minimum(a, b), jnp.maximum(c, d), jnp.minimum(c, d)
        gs.append(jnp.maximum(m1, m2) + jnp.maximum(jnp.minimum(m1, m2), jnp.maximum(n1, n2)))
    nums = []
    for g in range(N_EXPERT_GROUPS):
        grank = sum(beats(j, g, gs).astype(f32) for j in range(N_EXPERT_GROUPS) if j != g)
        vals = sel_r[per_group * g:per_group * (g + 1)]
        for i in range(per_group):
            rank = sum(beats(j, i, vals).astype(f32) for j in range(per_group) if j != i)
            pick = (grank == 0.0) & (rank < 2.0)
            nums.append(jnp.where(pick, sc_r[per_group * g + i], 0.0))
    den = sum(nums)
    return jnp.concatenate(nums, axis=0) / den


def _wo_kernel(mix, x_ref, oa_ref, *refs):
    if mix:
        nbr = len(DIL_CONFIGS)
        md = _mix_branches(refs[:nbr], refs[nbr:2 * nbr], refs[2 * nbr], refs[2 * nbr + 11:])
        refs = refs[2 * nbr + 1:2 * nbr + 11]
    else:
        md = refs[0][...]
        refs = refs[1:]
    wo_ref, g1_ref, ng_ref, sc_ref, sh_ref, rwt_ref, rb_ref, x1_ref, h_ref, gate_ref = refs
    y = jnp.dot(oa_ref[...].astype(bf16), wo_ref[0:VA, :], preferred_element_type=f32)
    y = y + jnp.dot(md.astype(bf16), wo_ref[VA:, :], preferred_element_type=f32)
    x1 = x_ref[...] + g1_ref[...] * y
    x1_ref[...] = x1
    h = _rmsnorm(x1, ng_ref[...]) * (1.0 + sc_ref[...]) + sh_ref[...]
    h_ref[...] = h.astype(bf16)
    r_hi, r_lo = _split_bf16(rwt_ref[...], 2)
    h_hi, h_lo = _split_bf16(h, 2)
    logits_t = (lax.dot_general(r_hi, h_hi, NT, preferred_element_type=f32)
                + lax.dot_general(r_hi, h_lo, NT, preferred_element_type=f32)
                + lax.dot_general(r_lo, h_hi, NT, preferred_element_type=f32))
    gate_ref[...] = _route(logits_t, rb_ref[...])


def _wo_call(x2d, oa, md, wo, g1, ng, sc, sh, rwt, rb, tm, tiles_per_batch):
    n, d = x2d.shape
    row = lambda i: (i, 0)
    const = lambda shape: pl.BlockSpec(shape, lambda i: (0,) * len(shape))
    mix = isinstance(md, tuple)
    if mix:
        md_specs, scratch = _mix_specs(tm)
        md_args = (*md[0], *md[1], jnp.asarray(_head_expand_np(), bf16))
    else:
        md_specs, scratch, md_args = [pl.BlockSpec((tm, QD), row)], [], (md,)
    return pl.pallas_call(
        functools.partial(_wo_kernel, mix),
        grid=(n // tm,),
        in_specs=[
            pl.BlockSpec((tm, d), row),
            pl.BlockSpec((tm, VA), row),
            *md_specs,
            _resident((d, d), lambda i: (0, 0)),
            _mod_spec(g1, tiles_per_batch),
            const((1, d)),
            _mod_spec(sc, tiles_per_batch),
            _mod_spec(sh, tiles_per_batch),
            const((N_EXPERTS, d)),
            const((N_EXPERTS, 1)),
        ],
        out_specs=[pl.BlockSpec((tm, d), row), pl.BlockSpec((tm, d), row), pl.BlockSpec((N_EXPERTS, tm), lambda i: (0, i))],
        out_shape=[jax.ShapeDtypeStruct((n, d), f32), jax.ShapeDtypeStruct((n, d), bf16),
                   jax.ShapeDtypeStruct((N_EXPERTS, n), f32)],
        scratch_shapes=scratch,
        compiler_params=_cparams("arbitrary"),
        name="wo_router",
    )(x2d, oa, *md_args, wo, g1, ng, sc, sh, rwt, rb)


def _moe_dense_kernel(final, h_ref, gate_ref, wg_ref, wu_ref, wd_ref, x1_ref, g2_ref, fg_ref, out_ref, acc_ref):
    e = pl.program_id(1)

    @pl.when(e == 0)
    def _():
        acc_ref[...] = jnp.zeros_like(acc_ref)

    h = h_ref[...]
    hid = _silu(jnp.dot(h, wg_ref[...], preferred_element_type=f32)) * jnp.dot(h, wu_ref[...], preferred_element_type=f32)
    lane = lax.broadcasted_iota(jnp.int32, (1, N_EXPERTS), 1)
    ge = jnp.sum(jnp.where(lane == e, gate_ref[...], 0.0), axis=-1, keepdims=True)
    acc_ref[...] += jnp.dot((hid * ge).astype(bf16), wd_ref[...], preferred_element_type=f32)

    @pl.when(e == pl.num_programs(1) - 1)
    def _():
        x2 = x1_ref[...] + g2_ref[...] * acc_ref[...]
        out_ref[...] = _rmsnorm(x2, fg_ref[...]) if final else x2


def _moe_dense_call(h, gate, wg, wu, wd, x1, g2, fg, final, tm, tiles_per_batch):
    n, d = x1.shape
    ne, _, ff = wg.shape
    row = lambda i, e: (i, 0)
    return pl.pallas_call(
        functools.partial(_moe_dense_kernel, final),
        grid=(n // tm, ne),
        in_specs=[
            pl.BlockSpec((tm, d), row),
            pl.BlockSpec((tm, N_EXPERTS), row),
            pl.BlockSpec((None, d, ff), lambda i, e: (e, 0, 0)),
            pl.BlockSpec((None, d, ff), lambda i, e: (e, 0, 0)),
            pl.BlockSpec((None, ff, d), lambda i, e: (e, 0, 0)),
            pl.BlockSpec((tm, d), row),
            _mod_spec(g2, tiles_per_batch),
            pl.BlockSpec((1, d), lambda i, e: (0, 0)),
        ],
        out_specs=pl.BlockSpec((tm, d), row),
        out_shape=jax.ShapeDtypeStruct((n, d), f32),
        scratch_shapes=[pltpu.VMEM((tm, d), f32)],
        compiler_params=_cparams("arbitrary", "arbitrary"),
        name="moe_dense",
    )(h, gate, wg, wu, wd, x1, g2, fg)


MOE_TILE = 512
MOE_ALIGN = 16
MOE_WINDOW = 80
MOE_ROWS = 1344
MOE_SELECT_ROWS = 672
MOE_COMBINE_ROWS = 512
MOE_VMEM_LIMIT = 56 * 1024 * 1024
MOE_GROUP = 4
MOE_STEP_EXPERTS = 2


def _plan_kernel(gate_ref, slot_ref, off_ref, cnt_ref):
    t = gate_ref.shape[1]
    gate = gate_ref[...]
    sel = jnp.where(gate > 0.0, 1.0, 0.0)
    before = lax.broadcasted_iota(jnp.int32, (t, t), 0) < lax.broadcasted_iota(jnp.int32, (t, t), 1)
    rank = jnp.dot(sel.astype(bf16), jnp.where(before, 1.0, 0.0).astype(bf16), preferred_element_type=f32)
    cnt = jnp.sum(sel, axis=1, keepdims=True)
    cpad = jnp.floor((cnt + (MOE_ALIGN - 1)) * (1.0 / MOE_ALIGN)) * MOE_ALIGN
    offs, run = [], jnp.zeros((1, 1), f32)
    for e in range(N_EXPERTS):
        offs.append(run)
        run = run + cpad[e:e + 1]
    off = jnp.concatenate(offs, axis=0)
    dest = off + rank
    seen = jnp.zeros((1, t), f32)
    dest_a = dest_b = w_a = w_b = jnp.zeros((1, t), f32)
    n_a = n_b = jnp.zeros((1, t), f32)
    for e in range(N_EXPERTS):
        s_e = sel[e:e + 1]
        is_a = s_e * jnp.where(seen == 0.0, 1.0, 0.0)
        is_b = s_e * jnp.where(seen == 1.0, 1.0, 0.0)
        dest_a, w_a, n_a = dest_a + is_a * dest[e:e + 1], w_a + is_a * gate[e:e + 1], n_a + is_a
        dest_b, w_b, n_b = dest_b + is_b * dest[e:e + 1], w_b + is_b * gate[e:e + 1], n_b + is_b
        seen = seen + s_e
    dest_a = jnp.where(n_a > 0.0, dest_a, -1.0)
    dest_b = jnp.where(n_b > 0.0, dest_b, -1.0)
    slot_ref[...] = jnp.concatenate([dest_a, dest_b, w_a, w_b, jnp.zeros((4, t), f32)], axis=0)
    off_ref[...] = jnp.broadcast_to(off, (N_EXPERTS, 128))
    cnt_ref[...] = jnp.broadcast_to(cnt, (N_EXPERTS, 128))


def _plan_call(gate_t):
    n = gate_t.shape[1]
    nt = n // MOE_TILE
    slots, off, cnt = pl.pallas_call(
        _plan_kernel,
        grid=(nt,),
        in_specs=[pl.BlockSpec((N_EXPERTS, MOE_TILE), lambda i: (0, i))],
        out_specs=[pl.BlockSpec((8, MOE_TILE), lambda i: (0, i)),
                   pl.BlockSpec((None, N_EXPERTS, 128), lambda i: (i, 0, 0)),
                   pl.BlockSpec((None, N_EXPERTS, 128), lambda i: (i, 0, 0))],
        out_shape=[jax.ShapeDtypeStruct((8, n), f32), jax.ShapeDtypeStruct((nt, N_EXPERTS, 128), f32),
                   jax.ShapeDtypeStruct((nt, N_EXPERTS, 128), f32)],
        compiler_params=_cparams("arbitrary"),
        name="moe_plan",
    )(gate_t)
    to_smem = lambda a: a[:, :, 0].astype(jnp.int32).reshape(nt * N_EXPERTS)
    return slots, to_smem(off), to_smem(cnt)


def _moe_sparse_kernel(off_ref, cnt_ref, h_ref, srow_ref, scol_ref, wg_ref, wu_ref, wd_ref, y_ref, xs_ref, ys_ref):
    i, e = pl.program_id(0), pl.program_id(1)
    t = MOE_TILE
    nt = h_ref.shape[0] // t

    @pl.when(e == 0)
    def _():
        for s in range(nt):
            tok = slice(s * t, (s + 1) * t)
            dest_a, dest_b = srow_ref[0:1, tok], srow_ref[1:2, tok]
            for c in range(MOE_ROWS // MOE_SELECT_ROWS):
                r = (lax.broadcasted_iota(jnp.int32, (MOE_SELECT_ROWS, t), 0) + c * MOE_SELECT_ROWS).astype(f32)
                sel = (jnp.where(r == dest_a, 1.0, 0.0) + jnp.where(r == dest_b, 1.0, 0.0)).astype(bf16)
                rows = slice(c * MOE_SELECT_ROWS, (c + 1) * MOE_SELECT_ROWS)
                xs_ref[s, rows, :] = jnp.dot(sel, h_ref[tok, :], preferred_element_type=f32).astype(bf16)
        ys_ref[...] = jnp.zeros_like(ys_ref)

    for k in range(MOE_STEP_EXPERTS):
        ex = e * MOE_STEP_EXPERTS + k
        offs = [off_ref[(i * nt + s) * N_EXPERTS + ex] for s in range(nt)]
        nwin = [(cnt_ref[(i * nt + s) * N_EXPERTS + ex] + MOE_WINDOW - 1) // MOE_WINDOW for s in range(nt)]

        def window(j, carry, k=k, offs=offs, nwin=nwin):
            starts = [pl.multiple_of(offs[s] + jnp.maximum(jnp.minimum(j, nwin[s] - 1), 0) * MOE_WINDOW, MOE_ALIGN)
                      for s in range(nt)]
            xw = jnp.concatenate([xs_ref[s, pl.ds(starts[s], MOE_WINDOW), :] for s in range(nt)], axis=0)
            hid = _silu(jnp.dot(xw, wg_ref[k], preferred_element_type=f32)) * jnp.dot(xw, wu_ref[k], preferred_element_type=f32)
            yw = jnp.dot(hid.astype(bf16), wd_ref[k], preferred_element_type=f32).astype(bf16)
            for s in range(nt):
                ys_ref[s, pl.ds(starts[s], MOE_WINDOW), :] = yw[s * MOE_WINDOW:(s + 1) * MOE_WINDOW]
            return carry

        lax.fori_loop(0, functools.reduce(jnp.maximum, nwin), window, 0)

    @pl.when(e == pl.num_programs(1) - 1)
    def _():
        for s in range(nt):
            for c in range(t // MOE_COMBINE_ROWS):
                rows = slice(s * t + c * MOE_COMBINE_ROWS, s * t + (c + 1) * MOE_COMBINE_ROWS)
                sc = scol_ref[rows, :]
                r = lax.broadcasted_iota(jnp.int32, (MOE_COMBINE_ROWS, MOE_ROWS), 1).astype(f32)
                wsel = jnp.where(r == sc[:, 0:1], sc[:, 2:3], 0.0) + jnp.where(r == sc[:, 1:2], sc[:, 3:4], 0.0)
                y_ref[rows, :] = jnp.dot(wsel.astype(bf16), ys_ref[s], preferred_element_type=f32).astype(y_ref.dtype)


def _moe_sparse_call(h, gate_t, wg, wu, wd):
    n, d = h.shape
    ne, _, ff = wg.shape
    t = MOE_TILE
    tg = MOE_GROUP * t
    assert MOE_ROWS >= 2 * t + (N_EXPERTS - 1) * (MOE_ALIGN - 1) + MOE_WINDOW and MOE_ROWS % MOE_SELECT_ROWS == 0
    slots, off, cnt = _plan_call(gate_t)
    row = lambda i, e, *_: (i, 0)
    grid_spec = pltpu.PrefetchScalarGridSpec(
        num_scalar_prefetch=2,
        grid=(n // tg, ne // MOE_STEP_EXPERTS),
        in_specs=[
            pl.BlockSpec((tg, d), row),
            pl.BlockSpec((8, tg), lambda i, e, *_: (0, i)),
            pl.BlockSpec((tg, 8), row),
            pl.BlockSpec((MOE_STEP_EXPERTS, d, ff), lambda i, e, *_: (e, 0, 0)),
            pl.BlockSpec((MOE_STEP_EXPERTS, d, ff), lambda i, e, *_: (e, 0, 0)),
            pl.BlockSpec((MOE_STEP_EXPERTS, ff, d), lambda i, e, *_: (e, 0, 0)),
        ],
        out_specs=pl.BlockSpec((tg, d), row),
        scratch_shapes=[pltpu.VMEM((MOE_GROUP, MOE_ROWS, d), bf16), pltpu.VMEM((MOE_GROUP, MOE_ROWS, d), bf16)],
    )
    return pl.pallas_call(
        _moe_sparse_kernel,
        grid_spec=grid_spec,
        out_shape=jax.ShapeDtypeStruct((n, d), bf16),
        compiler_params=pltpu.CompilerParams(dimension_semantics=("arbitrary", "arbitrary"),
                                             vmem_limit_bytes=MOE_VMEM_LIMIT),
        name="moe_sparse",
    )(off, cnt, h, slots, slots.T, wg, wu, wd)


def _residual_norm_kernel(x1_ref, y_ref, g2_ref, fg_ref, out_ref):
    out_ref[...] = _rmsnorm(x1_ref[...] + g2_ref[...] * y_ref[...].astype(f32), fg_ref[...])


def _residual_norm_call(x1, y, g2, fg, tm, tiles_per_batch):
    n, d = x1.shape
    row = lambda i: (i, 0)
    return pl.pallas_call(
        _residual_norm_kernel,
        grid=(n // tm,),
        in_specs=[pl.BlockSpec((tm, d), row), pl.BlockSpec((tm, d), row), _mod_spec(g2, tiles_per_batch),
                  pl.BlockSpec((1, d), lambda i: (0, 0))],
        out_specs=pl.BlockSpec((tm, d), row),
        out_shape=jax.ShapeDtypeStruct((n, d), f32),
        compiler_params=_cparams("arbitrary"),
        name="residual_norm",
    )(x1, y, g2, fg)


def _reorder_w_in(w_in):
    o = np.cumsum((0, QA, QA, VA, GATE_RANK, VA, QD, QD, QD))
    pad = jnp.zeros(w_in.shape[:-1] + (128 - GATE_RANK,), w_in.dtype)
    parts = [w_in[..., o[0]:o[3]], w_in[..., o[4]:o[5]], w_in[..., o[3]:o[4]], pad, w_in[..., o[5]:o[8]]]
    return jnp.concatenate(parts, axis=-1).astype(bf16)


def _trunk(x, mods, prompt, gla_state, cache_k, cache_v, p):
    nb, s, d = x.shape
    n = nb * s
    depth = p["w_in"].shape[0]
    sh1, sc1, g1, sh2, sc2, g2 = mods
    if prompt:
        tm = 512
        tpb = s // tm
        keep = min(2048, s)
        keep_tiles = keep // tm
    else:
        tm, tpb, keep_tiles = n, 0, 0
    x2d = x.reshape(n, d)
    new_s, new_k, new_v = [], [], []
    resid = None
    for l in range(depth):
        outs = _inproj_call(x2d, sc1[l], sh1[l], p["norm1_g"][l], p["w_in"][l], tm, tpb, keep_tiles, resid)
        if resid is not None:
            x2d, outs = outs[0], outs[1:]
        a, q, k, v = outs[:4]
        if prompt:
            oa, st = _gla_prompt_call(a, p["wg"][l], p["bg"][l], p["gn"][l], nb, tm)
            new_s.append(_state_from_blockdiag(st))
            new_k.append(outs[4])
            new_v.append(outs[5])
            qkv = {1: (q, k, v)}
            for j, dil in enumerate(STRIDED_DILS):
                qkv[dil] = outs[6 + 3 * j:9 + 3 * j]
            os, ls = zip(*[_dil_prompt_call(*qkv[dil], p["bias_p"][c], nb, dil) for c, (_, dil) in enumerate(DIL_CONFIGS)])
            md = (os, ls)
        else:
            oa, st = _gla_sample_call(a, p["wg"][l], p["bg"][l], p["gn"][l], gla_state[l])
            oa = oa.reshape(n, VA)
            new_s.append(st.reshape(nb, H_A, DK_A, DV_A))
            new_k.append(k.reshape(nb, s, H_D, DH_D))
            new_v.append(v.reshape(nb, s, H_D, DH_D))
            md = _dil_sample_call(q, k, v, cache_k, cache_v, l, *p["bias_s"])
        x1, h2, gate_t = _wo_call(x2d, oa, md, p["w_o"][l], g1[l], p["norm2_g"][l], sc2[l], sh2[l],
                                  p["rwt"], p["rb"], tm, tpb)
        experts = (p["moe_wg"][l], p["moe_wu"][l], p["moe_wd"][l])
        if prompt:
            x2d, resid = x1, (_moe_sparse_call(h2, gate_t, *experts), g2[l])
        else:
            x2d = _moe_dense_call(h2, gate_t.T, *experts, x1, g2[l], p["final_g"], l == depth - 1, tm, tpb)
    new_k, new_v = jnp.stack(new_k), jnp.stack(new_v)
    if prompt:
        x2d = _residual_norm_call(x2d, *resid, p["final_g"], tm, tpb)
        new_k = new_k[:, :, 1:].reshape(depth, nb, keep, H_D, DH_D)
        new_v = new_v[:, :, 1:].reshape(depth, nb, keep, H_D, DH_D)
    return x2d.reshape(nb, s, d), jnp.stack(new_s), new_k, new_v


def kernel(x_prompt, x_sample, state_gla, cache_win_k, cache_win_v, c_prompt, c_sample, w_ada, b_ada, norm1_g, norm2_g, w_in, w_gate_up, b_gate, gla_norm_g, w_o, rel_bias, router_w, router_b, moe_w_gate, moe_w_up, moe_w_down, final_norm_g):
    nbp, seq, d = x_prompt.shape
    nbs, dec_seq, _ = x_sample.shape
    depth = w_in.shape[0]
    assert dec_seq == 1 and cache_win_k.shape[2] == DIL_CONFIGS[-1][0], "sample path: one new token over a full window"
    assert seq % (DIL_CONFIGS[-1][0] * ATTN_QUERY_BLOCKS) == 0, "prompt length must split into whole attention steps"

    c_all = jnp.concatenate([c_prompt, c_sample], axis=0)
    c_all = jnp.pad(c_all, ((0, -c_all.shape[0] % 8), (0, 0)))
    mod = _ada_call(c_all, w_ada, b_ada)
    mods_p = [m[:, :nbp, None, :] for m in jnp.split(mod, 6, axis=-1)]
    mods_s = [m[:, None, nbp:nbp + nbs, :] for m in jnp.split(mod, 6, axis=-1)]

    p = {
        "w_in": _reorder_w_in(w_in),
        "norm1_g": norm1_g.reshape(depth, 1, d),
        "norm2_g": norm2_g.reshape(depth, 1, d),
        "wg": jnp.pad(w_gate_up, ((0, 0), (0, 128 - GATE_RANK), (0, 0))),
        "bg": b_gate.reshape(depth, 1, QA),
        "gn": gla_norm_g.reshape(depth, 1, VA),
        "w_o": w_o.astype(bf16),
        "rwt": router_w.T,
        "rb": router_b.reshape(N_EXPERTS, 1),
        "moe_wg": moe_w_gate.astype(bf16),
        "moe_wu": moe_w_up.astype(bf16),
        "moe_wd": moe_w_down.astype(bf16),
        "final_g": final_norm_g.reshape(1, d),
        "bias_p": [_prompt_bias(rel_bias, dil) for (_, dil) in DIL_CONFIGS],
        "bias_s": _sample_bias(rel_bias, cache_win_k.shape[2]),
    }
    y_p, s_p, k_p, v_p = _trunk(x_prompt, mods_p, True, None, None, None, p)
    y_s, s_s, k_s, v_s = _trunk(x_sample, mods_s, False, state_gla, cache_win_k, cache_win_v, p)
    return (y_p, y_s, s_p, k_p, v_p, s_s, k_s, v_s)
```
